```python
import math
import jax, jax.numpy as jnp
from jax import lax
import numpy as np

D_MODEL = 1024
BATCH = 2
SEQ = 8192
DEPTH = 1

HEAD_DIM = 64
MIX_WIDTH = D_MODEL
RWKV_WIDTH = MIX_WIDTH // 2
ATTN_WIDTH = MIX_WIDTH - RWKV_WIDTH
RWKV_HEADS = RWKV_WIDTH // HEAD_DIM
ATTN_Q_HEADS = ATTN_WIDTH // HEAD_DIM
ATTN_KV_HEADS = 2
ATTN_GROUP = ATTN_Q_HEADS // ATTN_KV_HEADS
DECAY_LORA = 64
ICLR_LORA = 64
GATE_LORA = 128
N_DIRS = 2
WINDOW = 128
BLOCK = 128
REL_BUCKETS = 32
REL_MAX_DIST = 128
D_FF = ((8 * D_MODEL // 3 + 127) // 128) * 128
CONV_WIDTH = 3
NORM_EPS = 1e-6
LNX_EPS = 64e-5

RWKV_SPLITS = [RWKV_WIDTH, 2 * RWKV_WIDTH, 3 * RWKV_WIDTH,
               3 * RWKV_WIDTH + DECAY_LORA, 3 * RWKV_WIDTH + DECAY_LORA + ICLR_LORA]
RWKV_COLS = 3 * RWKV_WIDTH + DECAY_LORA + ICLR_LORA + GATE_LORA
KV_COLS = ATTN_KV_HEADS * HEAD_DIM
ATTN_COLS = ATTN_WIDTH + 2 * KV_COLS
N_IN = RWKV_COLS + ATTN_COLS

kernel_name = "hybrid_rwkv7_swa_convglu_encoder"


def rms_norm(x, g, eps=NORM_EPS):
    xf = x.astype(jnp.float32)
    y = xf * lax.rsqrt(jnp.mean(xf * xf, axis=-1, keepdims=True) + eps)
    return (y * g.astype(jnp.float32)).astype(x.dtype)


def t5_bucket(rel):
    nb = REL_BUCKETS // 2
    max_exact = nb // 2
    ret = jnp.where(rel > 0, nb, 0)
    n = jnp.abs(rel)
    large = max_exact + (jnp.log(jnp.maximum(n, 1).astype(jnp.float32) / max_exact)
                         / math.log(REL_MAX_DIST / max_exact) * (nb - max_exact)).astype(jnp.int32)
    large = jnp.minimum(large, nb - 1)
    return ret + jnp.where(n < max_exact, n, large)


def rwkv7_step(S, inp):
    r, w, k, v, a, b = inp
    sa = jnp.einsum('dbhvk,dbhk->dbhv', S, a)
    S = S * w[..., None, :] + sa[..., None] * b[..., None, :] + v[..., :, None] * k[..., None, :]
    y = jnp.einsum('dbhvk,dbhk->dbhv', S, r)
    return S, y


def rwkv7_bidir(p, mu_prev, mu_next, w0, w2, a0, a2, g2, k_k, k_a, r_k, lnx_w, lnx_b):
    B, T, _ = p.shape
    H, N, C = RWKV_HEADS, HEAD_DIM, RWKV_WIDTH
    pf = p.astype(jnp.float32)
    prev = jnp.pad(pf, ((0, 0), (1, 0), (0, 0)))[:, :-1]
    nxt = jnp.pad(pf, ((0, 0), (0, 1), (0, 0)))[:, 1:]
    pf = pf + mu_prev * (prev - pf) + mu_next * (nxt - pf)
    r, k, v, xw, xa, xg = jnp.split(pf, RWKV_SPLITS, axis=-1)
    w_raw = w0[:, None, None, :] + jnp.einsum('btr,drc->dbtc', jnp.tanh(xw), w2)
    decay = jnp.exp(-jnp.exp(-jax.nn.softplus(-w_raw) - 0.5))
    iclr = jax.nn.sigmoid(a0[:, None, None, :] + jnp.einsum('btr,drc->dbtc', xa, a2))
    gate = jax.nn.sigmoid(xg) @ g2
    heads = lambda t: t.reshape(t.shape[:-1] + (H, N))
    kk = heads(k * k_k)
    kk = kk * lax.rsqrt(jnp.sum(kk * kk, axis=-1, keepdims=True) + 1e-12)
    kd = k[None] * (1.0 + (iclr - 1.0) * k_a)
    b = heads(iclr) * kk[None]

    def dirs(t):
        t = jnp.stack([t[0], jnp.flip(t[1], axis=1)])
        return jnp.moveaxis(t, 2, 0)

    shared = lambda t: jnp.broadcast_to(t[None], (N_DIRS,) + t.shape)
    xs = (dirs(shared(heads(r))), dirs(heads(decay)), dirs(heads(kd)),
          dirs(shared(heads(v))), dirs(shared(-kk)), dirs(b))
    S0 = jnp.zeros((N_DIRS, B, H, N, N), jnp.float32)
    _, y = lax.scan(rwkv7_step, S0, xs)
    y = jnp.moveaxis(y, 0, 2)
    y = y[0] + jnp.flip(y[1], axis=1)
    mu = jnp.mean(y, axis=-1, keepdims=True)
    var = jnp.mean(jnp.square(y - mu), axis=-1, keepdims=True)
    y = ((y - mu) * lax.rsqrt(var + LNX_EPS)).reshape(B, T, C) * lnx_w + lnx_b
    bonus = jnp.sum(heads(r) * heads(kd[0] + kd[1]) * r_k, axis=-1, keepdims=True) * heads(v)
    return ((y + bonus.reshape(B, T, C)) * gate).astype(p.dtype)


def banded_attention(q, k, v, q_gain, k_gain, rel_bias, sink):
    B, T, _ = q.shape
    nb = T // BLOCK
    q = rms_norm(q.reshape(B, T, ATTN_Q_HEADS, HEAD_DIM), q_gain)
    k = rms_norm(k.reshape(B, T, ATTN_KV_HEADS, HEAD_DIM), k_gain)
    v = v.reshape(B, T, ATTN_KV_HEADS, HEAD_DIM)
    qb = q.reshape(B, nb, BLOCK, ATTN_KV_HEADS, ATTN_GROUP, HEAD_DIM)

    def windows(t):
        tp = jnp.pad(t, ((0, 0), (BLOCK, BLOCK), (0, 0), (0, 0)))
        tp = tp.reshape(B, nb + 2, BLOCK, ATTN_KV_HEADS, HEAD_DIM)
        return jnp.concatenate([tp[:, :-2], tp[:, 1:-1], tp[:, 2:]], axis=2)

    kw, vw = windows(k), windows(v)
    s = jnp.einsum('bnqhgd,bnkhd->bnhgqk', qb, kw).astype(jnp.float32) * (HEAD_DIM ** -0.5)
    q_off = jnp.arange(BLOCK)[:, None]
    k_off = jnp.arange(3 * BLOCK)[None, :] - BLOCK
    rel = k_off - q_off
    bias = rel_bias[t5_bucket(rel)].astype(jnp.float32)
    bias = jnp.transpose(bias, (2, 0, 1)).reshape(ATTN_KV_HEADS, ATTN_GROUP, BLOCK, 3 * BLOCK)
    k_pos = jnp.arange(nb)[:, None] * BLOCK + k_off
    valid = (jnp.abs(rel) <= WINDOW)[None] & ((k_pos >= 0) & (k_pos < T))[:, None, :]
    s = jnp.where(valid[None, :, None, None], s + bias, -jnp.inf)
    sink_l = sink.astype(jnp.float32).reshape(ATTN_KV_HEADS, ATTN_GROUP)[None, None, :, :, None, None]
    m = jnp.maximum(jnp.max(s, axis=-1, keepdims=True), sink_l)
    pr = jnp.exp(s - m)
    denom = jnp.sum(pr, axis=-1, keepdims=True) + jnp.exp(sink_l - m)
    o = jnp.einsum('bnhgqk,bnkhd->bnqhgd', (pr / denom).astype(v.dtype), vw)
    return o.reshape(B, T, ATTN_WIDTH)


def conv_glu(h, w_up, conv_w, conv_b, w_down):
    T = h.shape[1]
    u = h @ w_up
    pad = CONV_WIDTH // 2
    up = jnp.pad(u, ((0, 0), (pad, pad), (0, 0)))
    u = sum(up[:, i:i + T] * conv_w[i] for i in range(CONV_WIDTH)) + conv_b
    g, val = jnp.split(u, 2, axis=-1)
    return (jax.nn.silu(g) * val) @ w_down


def setup_inputs(seed: int = 0) -> dict:
    key = jax.random.key(seed)
    ks = jax.random.split(key, 26)
    L, C = DEPTH, RWKV_WIDTH
    nrm = lambda k, shape, s: jax.random.normal(k, shape, jnp.float32) * s
    return {
        "x": nrm(ks[0], (BATCH, SEQ, D_MODEL), 1.0),
        "g_mix": 1.0 + nrm(ks[1], (L, D_MODEL), 0.02),
        "w_in": nrm(ks[2], (L, D_MODEL, N_IN), D_MODEL ** -0.5),
        "mu_prev": jax.random.uniform(ks[3], (L, RWKV_COLS), jnp.float32, 0.0, 0.5),
        "mu_next": jax.random.uniform(ks[4], (L, RWKV_COLS), jnp.float32, 0.0, 0.5),
        "w0": jax.random.uniform(ks[5], (L, N_DIRS, C), jnp.float32, -6.5, -1.0),
        "w2": nrm(ks[6], (L, N_DIRS, DECAY_LORA, C), 0.1),
        "a0": nrm(ks[7], (L, N_DIRS, C), 0.3),
        "a2": nrm(ks[8], (L, N_DIRS, ICLR_LORA, C), 0.5 * ICLR_LORA ** -0.5),
        "g2": nrm(ks[9], (L, GATE_LORA, C), GATE_LORA ** -0.5),
        "k_k": 0.85 + nrm(ks[10], (L, C), 0.05),
        "k_a": 1.0 + nrm(ks[11], (L, C), 0.05),
        "r_k": nrm(ks[12], (L, RWKV_HEADS, HEAD_DIM), 0.1),
        "lnx_w": 1.0 + nrm(ks[13], (L, C), 0.02),
        "lnx_b": nrm(ks[14], (L, C), 0.01),
        "q_gain": 1.0 + nrm(ks[15], (L, HEAD_DIM), 0.02),
        "k_gain": 1.0 + nrm(ks[16], (L, HEAD_DIM), 0.02),
        "rel_bias": nrm(ks[17], (REL_BUCKETS, ATTN_Q_HEADS), 0.3),
        "sink": nrm(ks[18], (L, ATTN_Q_HEADS), 0.5),
        "w_out": nrm(ks[19], (L, MIX_WIDTH, D_MODEL), MIX_WIDTH ** -0.5),
        "g_ffn": 1.0 + nrm(ks[20], (L, D_MODEL), 0.02),
        "w_up": nrm(ks[21], (L, D_MODEL, 2 * D_FF), D_MODEL ** -0.5),
        "conv_w": nrm(ks[22], (L, CONV_WIDTH, 2 * D_FF), CONV_WIDTH ** -0.5),
        "conv_b": nrm(ks[23], (L, 2 * D_FF), 0.01),
        "w_down": nrm(ks[24], (L, D_FF, D_MODEL), D_FF ** -0.5),
    }


def reference(x, g_mix, w_in, mu_prev, mu_next, w0, w2, a0, a2, g2, k_k, k_a, r_k, lnx_w, lnx_b,
              q_gain, k_gain, rel_bias, sink, w_out, g_ffn, w_up, conv_w, conv_b, w_down):
    for l in range(DEPTH):
        h = rms_norm(x, g_mix[l])
        proj = h @ w_in[l]
        p_rwkv = proj[..., :RWKV_COLS]
        q = proj[..., RWKV_COLS:RWKV_COLS + ATTN_WIDTH]
        k = proj[..., RWKV_COLS + ATTN_WIDTH:RWKV_COLS + ATTN_WIDTH + KV_COLS]
        v = proj[..., RWKV_COLS + ATTN_WIDTH + KV_COLS:]
        y_rwkv = rwkv7_bidir(p_rwkv, mu_prev[l], mu_next[l], w0[l], w2[l], a0[l], a2[l], g2[l],
                             k_k[l], k_a[l], r_k[l], lnx_w[l], lnx_b[l])
        y_attn = banded_attention(q, k, v, q_gain[l], k_gain[l], rel_bias, sink[l])
        x = x + jnp.concatenate([y_rwkv, y_attn], axis=-1) @ w_out[l]
        x = x + conv_glu(rms_norm(x, g_ffn[l]), w_up[l], conv_w[l], conv_b[l], w_down[l])
    return x
```

```python
import functools
import math

import jax
import jax.numpy as jnp
from jax import lax
from jax.experimental import pallas as pl
from jax.experimental.pallas import tpu as pltpu

F32 = jnp.float32
BF16 = jnp.bfloat16
HIGHEST = lax.Precision.HIGHEST

HEAD_DIM = 64
RWKV_WIDTH = 512
ATTN_WIDTH = 512
KV_HEADS = 2
Q_HEADS = 8
DECAY_LORA = 64
ICLR_LORA = 64
GATE_LORA = 128
RWKV_COLS = 3 * RWKV_WIDTH + DECAY_LORA + ICLR_LORA + GATE_LORA
KV_COLS = KV_HEADS * HEAD_DIM
WINDOW = 128
BLOCK = 128
REL_BUCKETS = 32
REL_MAX_DIST = 128
NORM_EPS = 1e-6
LNX_EPS = 64e-5
KK_EPS = 1e-12

LANES = 128
SUBLANES_F32 = 8
SUBLANES_BF16 = 16
VMEM_LIMIT = 48 * 1024 * 1024

CHUNK = 64
PAIR = 2 * HEAD_DIM
assert PAIR == LANES
HEAD_SHIFT = HEAD_DIM.bit_length() - 1
assert 1 << HEAD_SHIFT == HEAD_DIM


def _dot(a, b, precision=None):
    return jnp.dot(a, b, preferred_element_type=F32, precision=precision)


def _dot_nt(a, b, precision=None):
    return lax.dot_general(a, b, (((1,), (1,)), ((), ())), preferred_element_type=F32, precision=precision)


def _dot_tn(a, b, precision=None):
    return lax.dot_general(a, b, (((0,), (0,)), ((), ())), preferred_element_type=F32, precision=precision)


def _sigmoid(x):
    return 1.0 / (1.0 + jnp.exp(-x))


def _head_sum_matrix(width):
    r = lax.broadcasted_iota(jnp.int32, (width, width), 0) >> HEAD_SHIFT
    c = lax.broadcasted_iota(jnp.int32, (width, width), 1) >> HEAD_SHIFT
    return (r == c).astype(F32)


def _inproj_body(x_ref, g_ref, wr_ref, wq_ref, wkv_ref, p_ref, q_ref, kv_ref):
    x = x_ref[...]
    h = x * lax.rsqrt(jnp.mean(x * x, axis=-1, keepdims=True) + NORM_EPS) * g_ref[...]
    hb = h.astype(BF16)
    p_ref[...] = _dot(hb, wr_ref[...])
    q_ref[...] = _dot(hb, wq_ref[...])
    kv_ref[...] = _dot(hb, wkv_ref[...])


def _inproj(x2, g_mix, w_in, tile):
    rows, d = x2.shape
    wb = w_in.astype(BF16)
    wr, wq, wkv = wb[:, :RWKV_COLS], wb[:, RWKV_COLS:RWKV_COLS + ATTN_WIDTH], wb[:, RWKV_COLS + ATTN_WIDTH:]
    const = lambda i: (0, 0)
    row = lambda i: (i, 0)
    return pl.pallas_call(
        _inproj_body,
        grid=(rows // tile,),
        in_specs=[pl.BlockSpec((tile, d), row), pl.BlockSpec((1, d), const),
                  pl.BlockSpec(wr.shape, const), pl.BlockSpec(wq.shape, const), pl.BlockSpec(wkv.shape, const)],
        out_specs=[pl.BlockSpec((tile, RWKV_COLS), row), pl.BlockSpec((tile, ATTN_WIDTH), row),
                   pl.BlockSpec((tile, 2 * KV_COLS), row)],
        out_shape=[jax.ShapeDtypeStruct((rows, RWKV_COLS), F32), jax.ShapeDtypeStruct((rows, ATTN_WIDTH), F32),
                   jax.ShapeDtypeStruct((rows, 2 * KV_COLS), F32)],
        compiler_params=pltpu.CompilerParams(dimension_semantics=("parallel",), vmem_limit_bytes=VMEM_LIMIT),
        name="inproj",
    )(x2, g_mix.reshape(1, d), wr, wq, wkv)


def _prep_body(p_ref, pp_ref, pn_ref, mup_ref, mun_ref, w0_ref, w2_ref, a0_ref, a2_ref, g2_ref,
               kk_ref, ka_ref, rk_ref,
               r_o, v_o, kk_o, lw0_o, lw1_o, kd0_o, kd1_o, b0_o, b1_o, gate_o, bonus_o):
    i = pl.program_id(1)
    last = pl.num_programs(1) - 1
    p = p_ref[0]
    tt = p.shape[0]
    prev_row = jnp.where(i == 0, 0.0, pp_ref[0, SUBLANES_F32 - 1:SUBLANES_F32, :])
    next_row = jnp.where(i == last, 0.0, pn_ref[0, 0:1, :])
    row = lax.broadcasted_iota(jnp.int32, p.shape, 0)
    prev = jnp.where(row == 0, prev_row, pltpu.roll(p, 1, 0))
    nxt = jnp.where(row == tt - 1, next_row, pltpu.roll(p, tt - 1, 0))
    pf = p + mup_ref[...] * (prev - p) + mun_ref[...] * (nxt - p)

    c = RWKV_WIDTH
    r, k, v = pf[:, :c], pf[:, c:2 * c], pf[:, 2 * c:3 * c]
    lora = pf[:, 3 * c:3 * c + LANES]
    xg = pf[:, 3 * c + LANES:]
    lora_t = jnp.tanh(lora).astype(BF16)
    lora_b = lora.astype(BF16)

    head_sum = _head_sum_matrix(c)
    kx = k * kk_ref[...]
    kk = kx * lax.rsqrt(_dot(kx * kx, head_sum, HIGHEST) + KK_EPS)
    gate = _dot(_sigmoid(xg).astype(BF16), g2_ref[...])

    kds = []
    for d, (lw_o, kd_o, b_o) in enumerate(((lw0_o, kd0_o, b0_o), (lw1_o, kd1_o, b1_o))):
        w_raw = w0_ref[d:d + 1, :] + _dot(lora_t, w2_ref[d])
        lw_o[0] = (-math.exp(-0.5)) * _sigmoid(w_raw)
        iclr = _sigmoid(a0_ref[d:d + 1, :] + _dot(lora_b, a2_ref[d]))
        kd = k * (1.0 + (iclr - 1.0) * ka_ref[...])
        kd_o[0] = kd
        b_o[0] = iclr * kk
        kds.append(kd)

    r_o[0] = r
    v_o[0] = v
    kk_o[0] = kk
    gate_o[0] = gate
    bonus_o[0] = _dot(r * (kds[0] + kds[1]) * rk_ref[...], head_sum, HIGHEST) * v


def _rwkv_prep(p3, mu_prev, mu_next, w0, w2, a0, a2, g2, k_k, k_a, r_k, tile):
    b, t, cols = p3.shape
    c = RWKV_WIDTH
    nt = t // tile
    per8 = tile // SUBLANES_F32
    zeros = jnp.zeros((2, DECAY_LORA, c), F32)
    w2p = jnp.concatenate([w2, zeros], axis=1).astype(BF16)
    a2p = jnp.concatenate([zeros, a2], axis=1).astype(BF16)
    tok = lambda bi, i: (bi, i, 0)
    prev8 = lambda bi, i: (bi, jnp.maximum(i * per8 - 1, 0), 0)
    next8 = lambda bi, i: (bi, jnp.minimum((i + 1) * per8, t // SUBLANES_F32 - 1), 0)
    c2 = lambda bi, i: (0, 0)
    c3 = lambda bi, i: (0, 0, 0)
    out = jax.ShapeDtypeStruct((b, t, c), F32)
    return pl.pallas_call(
        _prep_body,
        grid=(b, nt),
        in_specs=[pl.BlockSpec((1, tile, cols), tok), pl.BlockSpec((1, SUBLANES_F32, cols), prev8),
                  pl.BlockSpec((1, SUBLANES_F32, cols), next8),
                  pl.BlockSpec((1, cols), c2), pl.BlockSpec((1, cols), c2),
                  pl.BlockSpec((2, c), c2), pl.BlockSpec((2, LANES, c), c3),
                  pl.BlockSpec((2, c), c2), pl.BlockSpec((2, LANES, c), c3),
                  pl.BlockSpec((GATE_LORA, c), c2),
                  pl.BlockSpec((1, c), c2), pl.BlockSpec((1, c), c2), pl.BlockSpec((1, c), c2)],
        out_specs=[pl.BlockSpec((1, tile, c), tok)] * 11,
        out_shape=[out] * 11,
        compiler_params=pltpu.CompilerParams(dimension_semantics=("parallel", "parallel"),
                                             vmem_limit_bytes=VMEM_LIMIT),
        name="rwkv_prep",
    )(p3, p3, p3, mu_prev.reshape(1, cols), mu_next.reshape(1, cols), w0, w2p, a0, a2p, g2.astype(BF16),
      k_k.reshape(1, c), k_a.reshape(1, c), r_k.reshape(1, c))


def _scan_body(rf, vf, kkf, lwf, kdf, bf, rb, vb, kkb, lwb, kdb, bb, yf_o, yb_o, h_scr):
    j = pl.program_id(1)

    @pl.when(j == 0)
    def _():
        h_scr[...] = jnp.zeros_like(h_scr)

    cs = rf.shape[1]
    n_pairs = rf.shape[2] // PAIR
    prec = HIGHEST

    ri = lax.broadcasted_iota(jnp.int32, (2 * cs, 2 * cs), 0)
    ci = lax.broadcasted_iota(jnp.int32, (2 * cs, 2 * cs), 1)
    cs_shift = cs.bit_length() - 1
    assert 1 << cs_shift == cs
    same_head = (ri >> cs_shift) == (ci >> cs_shift)
    t_row, t_col = ri & (cs - 1), ci & (cs - 1)
    eye = (ri == ci).astype(F32)
    tr = lax.broadcasted_iota(jnp.int32, (cs, cs), 0)
    tc = lax.broadcasted_iota(jnp.int32, (cs, cs), 1)
    lane_even = lax.broadcasted_iota(jnp.int32, (cs, PAIR), 1) < HEAD_DIM

    def stack(x):
        return jnp.concatenate([jnp.where(lane_even, x, 0.0), jnp.where(lane_even, 0.0, x)], axis=0)

    def unstack(x):
        return x[:cs] + x[cs:]

    dirs = ((0, rf, vf, kkf, lwf, kdf, bf, yf_o), (1, rb, vb, kkb, lwb, kdb, bb, yb_o))
    for d, r_ref, v_ref, kk_ref, lw_ref, kd_ref, b_ref, y_o in dirs:
        fwd = d == 0
        before = (t_col < t_row) if fwd else (t_col > t_row)
        strict = same_head & before
        incl = same_head & (before | (t_col == t_row))
        cum = ((tc <= tr) if fwd else (tc >= tr)).astype(F32)

        lw = lw_ref[0]
        c_in = _dot(cum, lw, HIGHEST)
        c_ex = c_in - lw
        c_tot = jnp.sum(lw, axis=0, keepdims=True)
        e_neg = jnp.exp(-c_in)
        e_end = jnp.exp(c_tot - c_in)
        gam = jnp.exp(c_tot)
        kk = kk_ref[0]
        kd = kd_ref[0]
        bv = b_ref[0]
        a_t = -kk * jnp.exp(c_ex)
        r_t = r_ref[0] * jnp.exp(c_in)
        b_t = bv * e_neg
        k_t = kd * e_neg
        b_h = bv * e_end
        k_h = kd * e_end
        vv = v_ref[0]

        for pr in range(n_pairs):
            sl = slice(pr * PAIR, (pr + 1) * PAIR)
            a_st, r_st, v_st = stack(a_t[:, sl]), stack(r_t[:, sl]), stack(vv[:, sl])
            bt, kt = b_t[:, sl], k_t[:, sl]
            lhs = jnp.concatenate([a_st, r_st], axis=0)
            rhs = jnp.concatenate([bt, bt, kt, kt], axis=0)
            gram = _dot_nt(lhs, rhs, prec)
            h2 = 2 * cs
            a_ab = jnp.where(strict, gram[:h2, :h2], 0.0)
            a_ak = jnp.where(strict, gram[:h2, h2:], 0.0)
            a_rb = jnp.where(incl, gram[h2:, :h2], 0.0)
            a_rk = jnp.where(incl, gram[h2:, h2:], 0.0)

            xpow = a_ab
            tinv = eye + a_ab
            for _ in range(cs_shift - 1):
                xpow = _dot(xpow, xpow, prec)
                tinv = tinv + _dot(tinv, xpow, prec)

            w1 = _dot(a_ak, v_st, prec)
            pq = _dot(tinv, jnp.concatenate([a_st, w1], axis=1), prec)
            ry = _dot(a_rb, pq, prec)
            r_hat = unstack(r_st + ry[:, :PAIR])
            y_hat = unstack(ry[:, PAIR:] + _dot(a_rk, v_st, prec))

            h0 = h_scr[d, pr]
            y_o[0, :, sl] = _dot(r_hat, h0, prec) + y_hat

            b_st, k_st = stack(b_h[:, sl]), stack(k_h[:, sl])
            g = eye * gam[:, sl] + _dot_tn(b_st, pq[:, :PAIR], prec)
            dd = _dot_tn(b_st, pq[:, PAIR:], prec) + _dot_tn(k_st, v_st, prec)
            h_scr[d, pr] = _dot(g, h0, prec) + dd


def _rwkv_scan(r, v, kk, lw0, lw1, kd0, kd1, b0, b1):
    b, t, c = r.shape
    nc = t // CHUNK
    fw = lambda bi, j: (bi, j, 0)
    bw = lambda bi, j: (bi, nc - 1 - j, 0)
    blk = (1, CHUNK, c)
    out = jax.ShapeDtypeStruct((b, t, c), F32)
    return pl.pallas_call(
        _scan_body,
        grid=(b, nc),
        in_specs=[pl.BlockSpec(blk, fw)] * 6 + [pl.BlockSpec(blk, bw)] * 6,
        out_specs=[pl.BlockSpec(blk, fw), pl.BlockSpec(blk, bw)],
        out_shape=[out, out],
        scratch_shapes=[pltpu.VMEM((2, c // PAIR, PAIR, PAIR), F32)],
        compiler_params=pltpu.CompilerParams(dimension_semantics=("arbitrary", "arbitrary"),
                                             vmem_limit_bytes=VMEM_LIMIT),
        name="rwkv_scan",
    )(r, v, kk, lw0, kd0, b0, r, v, kk, lw1, kd1, b1)


def _attn_body(q_ref, kvp_ref, kvc_ref, kvn_ref, qg_ref, kg_ref, bias_ref, sink_ref, o_ref):
    n = pl.program_id(1)
    nb = pl.num_programs(1)
    blk = q_ref.shape[1]
    avg = _head_sum_matrix(LANES) * (1.0 / HEAD_DIM)

    k_win = jnp.concatenate([kvp_ref[0, :, :KV_COLS], kvc_ref[0, :, :KV_COLS], kvn_ref[0, :, :KV_COLS]], axis=0)
    v_win = jnp.concatenate([kvp_ref[0, :, KV_COLS:], kvc_ref[0, :, KV_COLS:], kvn_ref[0, :, KV_COLS:]], axis=0)
    kn = k_win * lax.rsqrt(_dot(k_win * k_win, avg, HIGHEST) + NORM_EPS) * kg_ref[...]
    v_b = v_win.astype(BF16)

    lane = lax.broadcasted_iota(jnp.int32, kn.shape, 1)
    k_at = []
    for g in range(KV_HEADS):
        own = jnp.where((lane >> HEAD_SHIFT) == g, kn, 0.0)
        other = pltpu.roll(own, HEAD_DIM, 1)
        k_at.append([own if p == g else other for p in range(2)])

    row = lax.broadcasted_iota(jnp.int32, (blk, 3 * blk), 0)
    col = lax.broadcasted_iota(jnp.int32, (blk, 3 * blk), 1)
    rel = col - blk - row
    valid = (jnp.abs(rel) <= WINDOW)
    valid &= (col >= blk) | (n > 0)
    valid &= (col < 2 * blk) | (n < nb - 1)
    out_lane_even = lax.broadcasted_iota(jnp.int32, (blk, LANES), 1) < HEAD_DIM

    group = Q_HEADS // KV_HEADS
    for s in range(ATTN_WIDTH // LANES):
        g = (2 * s) // group
        sl = slice(s * LANES, (s + 1) * LANES)
        q = q_ref[0, :, sl]
        qn = q * lax.rsqrt(_dot(q * q, avg, HIGHEST) + NORM_EPS) * qg_ref[...] * (HEAD_DIM ** -0.5)
        kcat = jnp.concatenate([k_at[g][0], k_at[g][1]], axis=0).astype(BF16)
        scores = _dot_nt(qn.astype(BF16), kcat)
        halves = []
        for p in range(2):
            h = 2 * s + p
            sc = scores[:, p * 3 * blk:(p + 1) * 3 * blk] + bias_ref[h]
            sc = jnp.where(valid, sc, -jnp.inf)
            sink = sink_ref[0:1, h:h + 1]
            m = jnp.maximum(jnp.max(sc, axis=-1, keepdims=True), sink)
            e = jnp.exp(sc - m)
            denom = jnp.sum(e, axis=-1, keepdims=True) + jnp.exp(sink - m)
            o = _dot((e / denom).astype(BF16), v_b)
            halves.append(o if p == g else pltpu.roll(o, HEAD_DIM, 1))
        o_ref[0, :, sl] = jnp.where(out_lane_even, halves[0], halves[1])


def _t5_bucket_table():
    nb = REL_BUCKETS // 2
    max_exact = nb // 2
    rel = (jnp.arange(3 * BLOCK)[None, :] - BLOCK) - jnp.arange(BLOCK)[:, None]
    ret = jnp.where(rel > 0, nb, 0)
    n = jnp.abs(rel)
    large = max_exact + (jnp.log(jnp.maximum(n, 1).astype(F32) / max_exact)
                         / math.log(REL_MAX_DIST / max_exact) * (nb - max_exact)).astype(jnp.int32)
    large = jnp.minimum(large, nb - 1)
    return ret + jnp.where(n < max_exact, n, large)


def _attention(q, kv, q_gain, k_gain, rel_bias, sink):
    b, t, _ = q.shape
    nb = t // BLOCK
    bias = jnp.transpose(rel_bias[_t5_bucket_table()].astype(F32), (2, 0, 1))
    cur = lambda bi, n: (bi, n, 0)
    prv = lambda bi, n: (bi, jnp.maximum(n - 1, 0), 0)
    nxt = lambda bi, n: (bi, jnp.minimum(n + 1, nb - 1), 0)
    c2 = lambda bi, n: (0, 0)
    c3 = lambda bi, n: (0, 0, 0)
    kvblk = (1, BLOCK, 2 * KV_COLS)
    return pl.pallas_call(
        _attn_body,
        grid=(b, nb),
        in_specs=[pl.BlockSpec((1, BLOCK, ATTN_WIDTH), cur), pl.BlockSpec(kvblk, prv), pl.BlockSpec(kvblk, cur),
                  pl.BlockSpec(kvblk, nxt), pl.BlockSpec((1, LANES), c2), pl.BlockSpec((1, LANES), c2),
                  pl.BlockSpec((Q_HEADS, BLOCK, 3 * BLOCK), c3), pl.BlockSpec((1, Q_HEADS), c2)],
        out_specs=pl.BlockSpec((1, BLOCK, ATTN_WIDTH), cur),
        out_shape=jax.ShapeDtypeStruct((b, t, ATTN_WIDTH), F32),
        compiler_params=pltpu.CompilerParams(dimension_semantics=("parallel", "parallel"),
                                             vmem_limit_bytes=VMEM_LIMIT),
        name="band_attn",
    )(q, kv, kv, kv, jnp.tile(q_gain, 2).reshape(1, LANES), jnp.tile(k_gain, 2).reshape(1, LANES),
      bias, sink.reshape(1, Q_HEADS))


def _outproj_body(x_ref, yf_ref, yb_ref, bonus_ref, gate_ref, attn_ref, lw_ref, lb_ref, wo_ref, gf_ref,
                  x1_ref, h2_ref):
    c = RWKV_WIDTH
    avg = _head_sum_matrix(c) * (1.0 / HEAD_DIM)
    y = yf_ref[...] + yb_ref[...]
    mu = _dot(y, avg, HIGHEST)
    yc = y - mu
    var = _dot(yc * yc, avg, HIGHEST)
    yn = yc * lax.rsqrt(var + LNX_EPS) * lw_ref[...] + lb_ref[...]
    mix_r = (yn + bonus_ref[...]) * gate_ref[...]
    x1 = (x_ref[...] + _dot(mix_r.astype(BF16), wo_ref[:c, :])
          + _dot(attn_ref[...].astype(BF16), wo_ref[c:, :]))
    x1_ref[...] = x1
    h2 = x1 * lax.rsqrt(jnp.mean(x1 * x1, axis=-1, keepdims=True) + NORM_EPS) * gf_ref[...]
    h2_ref[...] = h2.astype(BF16)


def _outproj(x2, yf, yb, bonus, gate, attn, lnx_w, lnx_b, w_out, g_ffn, tile):
    rows, d = x2.shape
    c = RWKV_WIDTH
    row = lambda i: (i, 0)
    const = lambda i: (0, 0)
    return pl.pallas_call(
        _outproj_body,
        grid=(rows // tile,),
        in_specs=[pl.BlockSpec((tile, d), row)] + [pl.BlockSpec((tile, c), row)] * 5
                 + [pl.BlockSpec((1, c), const), pl.BlockSpec((1, c), const),
                    pl.BlockSpec(w_out.shape, const), pl.BlockSpec((1, d), const)],
        out_specs=[pl.BlockSpec((tile, d), row), pl.BlockSpec((tile, d), row)],
        out_shape=[jax.ShapeDtypeStruct((rows, d), F32), jax.ShapeDtypeStruct((rows, d), BF16)],
        compiler_params=pltpu.CompilerParams(dimension_semantics=("parallel",), vmem_limit_bytes=VMEM_LIMIT),
        name="outproj",
    )(x2, yf, yb, bonus, gate, attn, lnx_w.reshape(1, c), lnx_b.reshape(1, c), w_out.astype(BF16),
      g_ffn.reshape(1, d))


def _ffn_body(x1_ref, h_ref, hp_ref, hn_ref, wg_ref, wv_ref, cwg_ref, cwv_ref, cbg_ref, cbv_ref, wd_ref,
              o_ref, *, tiles_per_seq):
    i = pl.program_id(0)
    j = pl.program_id(1)
    tile = h_ref.shape[0]
    halo = hp_ref.shape[0]
    first = (i % tiles_per_seq) == 0
    last = (i % tiles_per_seq) == tiles_per_seq - 1
    hp = hp_ref[...]
    hn = hn_ref[...]
    hp = jnp.where(first, jnp.zeros_like(hp), hp)
    hn = jnp.where(last, jnp.zeros_like(hn), hn)
    hcat = jnp.concatenate([hp, h_ref[...], hn], axis=0)
    total = tile + 2 * halo

    def conv(u, cw_ref, cb_ref):
        up = pltpu.roll(u, 1, 0)[halo:halo + tile]
        un = pltpu.roll(u, total - 1, 0)[halo:halo + tile]
        return up * cw_ref[0:1, :] + u[halo:halo + tile] * cw_ref[1:2, :] + un * cw_ref[2:3, :] + cb_ref[...]

    g = conv(_dot(hcat, wg_ref[...]), cwg_ref, cbg_ref)
    val = conv(_dot(hcat, wv_ref[...]), cwv_ref, cbv_ref)
    act = (g * _sigmoid(g) * val).astype(BF16)
    part = _dot(act, wd_ref[...])

    @pl.when(j == 0)
    def _():
        o_ref[...] = x1_ref[...] + part

    @pl.when(j != 0)
    def _():
        o_ref[...] += part


def _ffn(x1, h2, w_up, conv_w, conv_b, w_down, seq, tile, ff_tile):
    rows, d = x1.shape
    d_ff = w_down.shape[0]
    nj = d_ff // ff_tile
    halo = SUBLANES_BF16
    per = tile // halo
    wub = w_up.astype(BF16)
    wdb = w_down.astype(BF16)
    cb = conv_b.reshape(1, 2 * d_ff)
    row = lambda i, j: (i, 0)
    prv = lambda i, j: (jnp.maximum(i * per - 1, 0), 0)
    nxt = lambda i, j: (jnp.minimum((i + 1) * per, rows // halo - 1), 0)
    gcol = lambda i, j: (0, j)
    vcol = lambda i, j: (0, nj + j)
    body = functools.partial(_ffn_body, tiles_per_seq=seq // tile)
    return pl.pallas_call(
        body,
        grid=(rows // tile, nj),
        in_specs=[pl.BlockSpec((tile, d), row), pl.BlockSpec((tile, d), row),
                  pl.BlockSpec((halo, d), prv), pl.BlockSpec((halo, d), nxt),
                  pl.BlockSpec((d, ff_tile), gcol), pl.BlockSpec((d, ff_tile), vcol),
                  pl.BlockSpec((3, ff_tile), gcol), pl.BlockSpec((3, ff_tile), vcol),
                  pl.BlockSpec((1, ff_tile), gcol), pl.BlockSpec((1, ff_tile), vcol),
                  pl.BlockSpec((ff_tile, d), lambda i, j: (j, 0))],
        out_specs=pl.BlockSpec((tile, d), row),
        out_shape=jax.ShapeDtypeStruct((rows, d), F32),
        compiler_params=pltpu.CompilerParams(dimension_semantics=("parallel", "arbitrary"),
                                             vmem_limit_bytes=VMEM_LIMIT),
        name="convglu",
    )(x1, h2, h2, h2, wub, wub, conv_w, conv_w, cb, cb, wdb)


def _layer(x, g_mix, w_in, mu_prev, mu_next, w0, w2, a0, a2, g2, k_k, k_a, r_k, lnx_w, lnx_b,
           q_gain, k_gain, rel_bias, sink, w_out, g_ffn, w_up, conv_w, conv_b, w_down):
    b, t, d = x.shape
    rows = b * t
    x2 = x.reshape(rows, d)
    p, q, kv = _inproj(x2, g_mix, w_in, tile=min(512, t))
    ops = _rwkv_prep(p.reshape(b, t, RWKV_COLS), mu_prev, mu_next, w0, w2, a0, a2, g2, k_k, k_a,
                     r_k.reshape(-1), tile=min(256, t))
    r, v, kk, lw0, lw1, kd0, kd1, b0, b1, gate, bonus = ops
    yf, yb = _rwkv_scan(r, v, kk, lw0, lw1, kd0, kd1, b0, b1)
    attn = _attention(q.reshape(b, t, ATTN_WIDTH), kv.reshape(b, t, 2 * KV_COLS), q_gain, k_gain, rel_bias, sink)
    flat = lambda z: z.reshape(rows, z.shape[-1])
    x1, h2 = _outproj(x2, flat(yf), flat(yb), flat(bonus), flat(gate), flat(attn), lnx_w, lnx_b, w_out, g_ffn,
                      tile=min(512, t))
    out = _ffn(x1, h2, w_up, conv_w, conv_b, w_down, seq=t, tile=min(512, t), ff_tile=256)
    return out.reshape(b, t, d)


def kernel(x, g_mix, w_in, mu_prev, mu_next, w0, w2, a0, a2, g2, k_k, k_a, r_k, lnx_w, lnx_b, q_gain, k_gain,
           rel_bias, sink, w_out, g_ffn, w_up, conv_w, conv_b, w_down):
    depth = g_mix.shape[0]
    for l in range(depth):
        x = _layer(x, g_mix[l], w_in[l], mu_prev[l], mu_next[l], w0[l], w2[l], a0[l], a2[l], g2[l], k_k[l], k_a[l],
                   r_k[l], lnx_w[l], lnx_b[l], q_gain[l], k_gain[l], rel_bias, sink[l], w_out[l], g_ffn[l],
                   w_up[l], conv_w[l], conv_b[l], w_down[l])
    return x
```

```python
import functools
import math

import jax
import jax.numpy as jnp
from jax import lax
from jax.experimental import pallas as pl
from jax.experimental.pallas import tpu as pltpu

F32 = jnp.float32
BF16 = jnp.bfloat16

HEAD_DIM = 64
RWKV_WIDTH = 512
ATTN_WIDTH = 512
KV_HEADS = 2
Q_HEADS = 8
DECAY_LORA = 64
ICLR_LORA = 64
GATE_LORA = 128
RWKV_COLS = 3 * RWKV_WIDTH + DECAY_LORA + ICLR_LORA + GATE_LORA
KV_COLS = KV_HEADS * HEAD_DIM
WINDOW = 128
BLOCK = 128
REL_BUCKETS = 32
REL_MAX_DIST = 128
NORM_EPS = 1e-6
LNX_EPS = 64e-5
KK_EPS = 1e-12

LANES = 128
SUBLANES_F32 = 8
SUBLANES_BF16 = 16
VMEM_LIMIT = 48 * 1024 * 1024

CHUNK = 64
PAIR = 2 * HEAD_DIM
assert PAIR == LANES
HEAD_SHIFT = HEAD_DIM.bit_length() - 1
assert 1 << HEAD_SHIFT == HEAD_DIM


def _dot(a, b, precision=None):
    return jnp.dot(a, b, preferred_element_type=F32, precision=precision)


def _dot_nt(a, b, precision=None):
    return lax.dot_general(a, b, (((1,), (1,)), ((), ())), preferred_element_type=F32, precision=precision)


def _dot_tn(a, b, precision=None):
    return lax.dot_general(a, b, (((0,), (0,)), ((), ())), preferred_element_type=F32, precision=precision)


def _sigmoid(x):
    return 1.0 / (1.0 + jnp.exp(-x))


def _split_bf16(x, parts):
    out = []
    for _ in range(parts):
        h = x.astype(BF16)
        out.append(h)
        x = x - h.astype(F32)
    return out


def _dot_exact_rhs(x, m, parts=2):
    acc = None
    for h in _split_bf16(x, parts):
        t = _dot(h, m)
        acc = t if acc is None else acc + t
    return acc


def _head_sum_matrix(width):
    r = lax.broadcasted_iota(jnp.int32, (width, width), 0) >> HEAD_SHIFT
    c = lax.broadcasted_iota(jnp.int32, (width, width), 1) >> HEAD_SHIFT
    return (r == c).astype(BF16)


def _inproj_body(x_ref, g_ref, wr_ref, wq_ref, wkv_ref, p_ref, q_ref, kv_ref):
    x = x_ref[...]
    h = x * lax.rsqrt(jnp.mean(x * x, axis=-1, keepdims=True) + NORM_EPS) * g_ref[...]
    hb = h.astype(BF16)
    p_ref[...] = _dot(hb, wr_ref[...])
    q_ref[...] = _dot(hb, wq_ref[...])
    kv_ref[...] = _dot(hb, wkv_ref[...])


def _inproj(x2, g_mix, w_in, tile):
    rows, d = x2.shape
    wb = w_in.astype(BF16)
    wr, wq, wkv = wb[:, :RWKV_COLS], wb[:, RWKV_COLS:RWKV_COLS + ATTN_WIDTH], wb[:, RWKV_COLS + ATTN_WIDTH:]
    const = lambda i: (0, 0)
    row = lambda i: (i, 0)
    return pl.pallas_call(
        _inproj_body,
        grid=(rows // tile,),
        in_specs=[pl.BlockSpec((tile, d), row), pl.BlockSpec((1, d), const),
                  pl.BlockSpec(wr.shape, const), pl.BlockSpec(wq.shape, const), pl.BlockSpec(wkv.shape, const)],
        out_specs=[pl.BlockSpec((tile, RWKV_COLS), row), pl.BlockSpec((tile, ATTN_WIDTH), row),
                   pl.BlockSpec((tile, 2 * KV_COLS), row)],
        out_shape=[jax.ShapeDtypeStruct((rows, RWKV_COLS), F32), jax.ShapeDtypeStruct((rows, ATTN_WIDTH), F32),
                   jax.ShapeDtypeStruct((rows, 2 * KV_COLS), F32)],
        compiler_params=pltpu.CompilerParams(dimension_semantics=("parallel",), vmem_limit_bytes=VMEM_LIMIT),
        name="inproj",
    )(x2, g_mix.reshape(1, d), wr, wq, wkv)


def _prep_body(p_ref, pp_ref, pn_ref, mup_ref, mun_ref, w0_ref, w2_ref, a0_ref, a2_ref, g2_ref,
               kk_ref, ka_ref, rk_ref,
               r_o, v_o, kk_o, lw0_o, lw1_o, kd0_o, kd1_o, b0_o, b1_o, gate_o, bonus_o):
    i = pl.program_id(1)
    last = pl.num_programs(1) - 1
    p = p_ref[0]
    tt = p.shape[0]
    prev_row = jnp.where(i == 0, 0.0, pp_ref[0, SUBLANES_F32 - 1:SUBLANES_F32, :])
    next_row = jnp.where(i == last, 0.0, pn_ref[0, 0:1, :])
    row = lax.broadcasted_iota(jnp.int32, p.shape, 0)
    prev = jnp.where(row == 0, prev_row, pltpu.roll(p, 1, 0))
    nxt = jnp.where(row == tt - 1, next_row, pltpu.roll(p, tt - 1, 0))
    pf = p + mup_ref[...] * (prev - p) + mun_ref[...] * (nxt - p)

    c = RWKV_WIDTH
    r, k, v = pf[:, :c], pf[:, c:2 * c], pf[:, 2 * c:3 * c]
    lora = pf[:, 3 * c:3 * c + LANES]
    xg = pf[:, 3 * c + LANES:]
    lora_t = jnp.tanh(lora).astype(BF16)
    lora_b = lora.astype(BF16)

    head_sum = _head_sum_matrix(c)
    kx = k * kk_ref[...]
    kk = kx * lax.rsqrt(_dot_exact_rhs(kx * kx, head_sum) + KK_EPS)
    gate = _dot(_sigmoid(xg).astype(BF16), g2_ref[...])

    kds = []
    for d, (lw_o, kd_o, b_o) in enumerate(((lw0_o, kd0_o, b0_o), (lw1_o, kd1_o, b1_o))):
        w_raw = w0_ref[d:d + 1, :] + _dot(lora_t, w2_ref[d])
        lw_o[0] = (-math.exp(-0.5)) * _sigmoid(w_raw)
        iclr = _sigmoid(a0_ref[d:d + 1, :] + _dot(lora_b, a2_ref[d]))
        kd = k * (1.0 + (iclr - 1.0) * ka_ref[...])
        kd_o[0] = kd
        b_o[0] = iclr * kk
        kds.append(kd)

    r_o[0] = r
    v_o[0] = v
    kk_o[0] = kk
    gate_o[0] = gate
    bonus_o[0] = _dot_exact_rhs(r * (kds[0] + kds[1]) * rk_ref[...], head_sum) * v


def _rwkv_prep(p3, mu_prev, mu_next, w0, w2, a0, a2, g2, k_k, k_a, r_k, tile):
    b, t, cols = p3.shape
    c = RWKV_WIDTH
    nt = t // tile
    per8 = tile // SUBLANES_F32
    zeros = jnp.zeros((2, DECAY_LORA, c), F32)
    w2p = jnp.concatenate([w2, zeros], axis=1).astype(BF16)
    a2p = jnp.concatenate([zeros, a2], axis=1).astype(BF16)
    tok = lambda bi, i: (bi, i, 0)
    prev8 = lambda bi, i: (bi, jnp.maximum(i * per8 - 1, 0), 0)
    next8 = lambda bi, i: (bi, jnp.minimum((i + 1) * per8, t // SUBLANES_F32 - 1), 0)
    c2 = lambda bi, i: (0, 0)
    c3 = lambda bi, i: (0, 0, 0)
    out = jax.ShapeDtypeStruct((b, t, c), F32)
    return pl.pallas_call(
        _prep_body,
        grid=(b, nt),
        in_specs=[pl.BlockSpec((1, tile, cols), tok), pl.BlockSpec((1, SUBLANES_F32, cols), prev8),
                  pl.BlockSpec((1, SUBLANES_F32, cols), next8),
                  pl.BlockSpec((1, cols), c2), pl.BlockSpec((1, cols), c2),
                  pl.BlockSpec((2, c), c2), pl.BlockSpec((2, LANES, c), c3),
                  pl.BlockSpec((2, c), c2), pl.BlockSpec((2, LANES, c), c3),
                  pl.BlockSpec((GATE_LORA, c), c2),
                  pl.BlockSpec((1, c), c2), pl.BlockSpec((1, c), c2), pl.BlockSpec((1, c), c2)],
        out_specs=[pl.BlockSpec((1, tile, c), tok)] * 11,
        out_shape=[out] * 11,
        compiler_params=pltpu.CompilerParams(dimension_semantics=("parallel", "parallel"),
                                             vmem_limit_bytes=VMEM_LIMIT),
        name="rwkv_prep",
    )(p3, p3, p3, mu_prev.reshape(1, cols), mu_next.reshape(1, cols), w0, w2p, a0, a2p, g2.astype(BF16),
      k_k.reshape(1, c), k_a.reshape(1, c), r_k.reshape(1, c))


def _scan_body(rf, vf, kkf, lwf, kdf, bf, rb, vb, kkb, lwb, kdb, bb, yf_o, yb_o, h_scr):
    j = pl.program_id(1)

    @pl.when(j == 0)
    def _():
        h_scr[...] = jnp.zeros_like(h_scr)

    cs = rf.shape[1]
    n_pairs = rf.shape[2] // PAIR

    ri = lax.broadcasted_iota(jnp.int32, (2 * cs, 2 * cs), 0)
    ci = lax.broadcasted_iota(jnp.int32, (2 * cs, 2 * cs), 1)
    cs_shift = cs.bit_length() - 1
    assert 1 << cs_shift == cs
    same_head = (ri >> cs_shift) == (ci >> cs_shift)
    t_row, t_col = ri & (cs - 1), ci & (cs - 1)
    eye = (ri == ci).astype(F32)
    tr = lax.broadcasted_iota(jnp.int32, (cs, cs), 0)
    tc = lax.broadcasted_iota(jnp.int32, (cs, cs), 1)
    lane_even = lax.broadcasted_iota(jnp.int32, (cs, PAIR), 1) < HEAD_DIM

    def stack(x):
        return jnp.concatenate([jnp.where(lane_even, x, 0.0), jnp.where(lane_even, 0.0, x)], axis=0)

    def unstack(x):
        return x[:cs] + x[cs:]

    dirs = ((0, rf, vf, kkf, lwf, kdf, bf, yf_o), (1, rb, vb, kkb, lwb, kdb, bb, yb_o))
    chains = []
    for d, r_ref, v_ref, kk_ref, lw_ref, kd_ref, b_ref, y_o in dirs:
        fwd = d == 0
        before = (t_col < t_row) if fwd else (t_col > t_row)
        strict = same_head & before
        incl = same_head & (before | (t_col == t_row))
        cum = ((tc <= tr) if fwd else (tc >= tr)).astype(BF16)

        lw = lw_ref[0]
        c_in = sum(_dot(cum, piece) for piece in _split_bf16(lw, 3))
        c_ex = c_in - lw
        c_tot = jnp.sum(lw, axis=0, keepdims=True)
        e_neg = jnp.exp(-c_in)
        e_end = jnp.exp(c_tot - c_in)
        gam = jnp.exp(c_tot)
        kd = kd_ref[0]
        bv = b_ref[0]
        a_t = -kk_ref[0] * jnp.exp(c_ex)
        r_t = r_ref[0] * jnp.exp(c_in)
        b_t = bv * e_neg
        k_t = kd * e_neg
        b_h = bv * e_end
        k_h = kd * e_end
        vv = v_ref[0]
        for pr in range(n_pairs):
            sl = slice(pr * PAIR, (pr + 1) * PAIR)
            r_st = stack(r_t[:, sl])
            chains.append(dict(
                d=d, pr=pr, sl=sl, y_o=y_o, strict=strict, incl=incl, r_st=r_st, gam=gam[:, sl],
                a_sb=stack(a_t[:, sl]).astype(BF16), r_sb=r_st.astype(BF16), v_sb=stack(vv[:, sl]).astype(BF16),
                bt=b_t[:, sl].astype(BF16), kt=k_t[:, sl].astype(BF16),
                b_sb=stack(b_h[:, sl]).astype(BF16), k_sb=stack(k_h[:, sl]).astype(BF16)))

    h2 = 2 * cs
    for c in chains:
        lhs = jnp.concatenate([c["a_sb"], c["r_sb"]], axis=0)
        rhs = jnp.concatenate([c["bt"], c["bt"], c["kt"], c["kt"]], axis=0)
        gram = _dot_nt(lhs, rhs)
        a_ab = jnp.where(c["strict"], gram[:h2, :h2], 0.0)
        c["a_ak"] = jnp.where(c["strict"], gram[:h2, h2:], 0.0).astype(BF16)
        c["a_rb"] = jnp.where(c["incl"], gram[h2:, :h2], 0.0).astype(BF16)
        c["a_rk"] = jnp.where(c["incl"], gram[h2:, h2:], 0.0).astype(BF16)
        c["xb"] = a_ab.astype(BF16)
        c["tinv"] = eye + a_ab

    for _ in range(cs_shift - 1):
        for c in chains:
            c["xb"] = _dot(c["xb"], c["xb"]).astype(BF16)
        for c in chains:
            c["tinv"] = c["tinv"] + _dot(c["tinv"].astype(BF16), c["xb"])

    for c in chains:
        c["w1"] = _dot(c["a_ak"], c["v_sb"]).astype(BF16)
        c["av"] = _dot(c["a_rk"], c["v_sb"])
        c["kv"] = _dot_tn(c["k_sb"], c["v_sb"])
        c["h"] = _split_bf16(h_scr[c["d"], c["pr"]], 2)
    for c in chains:
        c["pq"] = _dot(c["tinv"].astype(BF16), jnp.concatenate([c["a_sb"], c["w1"]], axis=1)).astype(BF16)
    for c in chains:
        ry = _dot(c["a_rb"], c["pq"])
        c["r_hat"] = unstack(c["r_st"] + ry[:, :PAIR]).astype(BF16)
        c["y_hat"] = unstack(ry[:, PAIR:] + c["av"])
        gd = _dot_tn(c["b_sb"], c["pq"])
        c["g"] = _split_bf16(eye * c["gam"] + gd[:, :PAIR], 2)
        c["dd"] = gd[:, PAIR:] + c["kv"]
    for c in chains:
        h_hi, h_lo = c["h"]
        g_hi, g_lo = c["g"]
        c["y_o"][0, :, c["sl"]] = _dot(c["r_hat"], h_hi) + c["y_hat"]
        h_scr[c["d"], c["pr"]] = _dot(g_hi, h_hi) + _dot(g_lo, h_hi) + _dot(g_hi, h_lo) + c["dd"]


def _rwkv_scan(r, v, kk, lw0, lw1, kd0, kd1, b0, b1):
    b, t, c = r.shape
    nc = t // CHUNK
    fw = lambda bi, j: (bi, j, 0)
    bw = lambda bi, j: (bi, nc - 1 - j, 0)
    blk = (1, CHUNK, c)
    out = jax.ShapeDtypeStruct((b, t, c), F32)
    return pl.pallas_call(
        _scan_body,
        grid=(b, nc),
        in_specs=[pl.BlockSpec(blk, fw)] * 6 + [pl.BlockSpec(blk, bw)] * 6,
        out_specs=[pl.BlockSpec(blk, fw), pl.BlockSpec(blk, bw)],
        out_shape=[out, out],
        scratch_shapes=[pltpu.VMEM((2, c // PAIR, PAIR, PAIR), F32)],
        compiler_params=pltpu.CompilerParams(dimension_semantics=("arbitrary", "arbitrary"),
                                             vmem_limit_bytes=VMEM_LIMIT),
        name="rwkv_scan",
    )(r, v, kk, lw0, kd0, b0, r, v, kk, lw1, kd1, b1)


def _attn_body(q_ref, kvp_ref, kvc_ref, kvn_ref, qg_ref, kg_ref, bias_ref, sink_ref, o_ref):
    n = pl.program_id(1)
    nb = pl.num_programs(1)
    blk = q_ref.shape[1]
    head_sum = _head_sum_matrix(LANES)
    head_mean = lambda z: _dot_exact_rhs(z, head_sum) * (1.0 / HEAD_DIM)

    k_win = jnp.concatenate([kvp_ref[0, :, :KV_COLS], kvc_ref[0, :, :KV_COLS], kvn_ref[0, :, :KV_COLS]], axis=0)
    v_win = jnp.concatenate([kvp_ref[0, :, KV_COLS:], kvc_ref[0, :, KV_COLS:], kvn_ref[0, :, KV_COLS:]], axis=0)
    kn = k_win * lax.rsqrt(head_mean(k_win * k_win) + NORM_EPS) * kg_ref[...]
    v_b = v_win.astype(BF16)

    lane = lax.broadcasted_iota(jnp.int32, kn.shape, 1)
    k_at = []
    for g in range(KV_HEADS):
        own = jnp.where((lane >> HEAD_SHIFT) == g, kn, 0.0)
        other = pltpu.roll(own, HEAD_DIM, 1)
        k_at.append([own if p == g else other for p in range(2)])

    row = lax.broadcasted_iota(jnp.int32, (blk, 3 * blk), 0)
    col = lax.broadcasted_iota(jnp.int32, (blk, 3 * blk), 1)
    rel = col - blk - row
    valid = (jnp.abs(rel) <= WINDOW)
    valid &= (col >= blk) | (n > 0)
    valid &= (col < 2 * blk) | (n < nb - 1)
    out_lane_even = lax.broadcasted_iota(jnp.int32, (blk, LANES), 1) < HEAD_DIM

    group = Q_HEADS // KV_HEADS
    for s in range(ATTN_WIDTH // LANES):
        g = (2 * s) // group
        sl = slice(s * LANES, (s + 1) * LANES)
        q = q_ref[0, :, sl]
        qn = q * lax.rsqrt(head_mean(q * q) + NORM_EPS) * qg_ref[...] * (HEAD_DIM ** -0.5)
        kcat = jnp.concatenate([k_at[g][0], k_at[g][1]], axis=0).astype(BF16)
        scores = _dot_nt(qn.astype(BF16), kcat)
        halves = []
        for p in range(2):
            h = 2 * s + p
            sc = scores[:, p * 3 * blk:(p + 1) * 3 * blk] + bias_ref[h]
            sc = jnp.where(valid, sc, -jnp.inf)
            sink = sink_ref[0:1, h:h + 1]
            m = jnp.maximum(jnp.max(sc, axis=-1, keepdims=True), sink)
            e = jnp.exp(sc - m)
            denom = jnp.sum(e, axis=-1, keepdims=True) + jnp.exp(sink - m)
            o = _dot((e / denom).astype(BF16), v_b)
            halves.append(o if p == g else pltpu.roll(o, HEAD_DIM, 1))
        o_ref[0, :, sl] = jnp.where(out_lane_even, halves[0], halves[1])


def _t5_bucket_table():
    nb = REL_BUCKETS // 2
    max_exact = nb // 2
    rel = (jnp.arange(3 * BLOCK)[None, :] - BLOCK) - jnp.arange(BLOCK)[:, None]
    ret = jnp.where(rel > 0, nb, 0)
    n = jnp.abs(rel)
    large = max_exact + (jnp.log(jnp.maximum(n, 1).astype(F32) / max_exact)
                         / math.log(REL_MAX_DIST / max_exact) * (nb - max_exact)).astype(jnp.int32)
    large = jnp.minimum(large, nb - 1)
    return ret + jnp.where(n < max_exact, n, large)


def _bias_body(bucket_ref, rb_ref, o_ref):
    h = pl.program_id(0)
    bucket = bucket_ref[...]
    acc = jnp.zeros(bucket.shape, F32)
    for bkt in range(REL_BUCKETS):
        acc = jnp.where(bucket == bkt, rb_ref[bkt, h], acc)
    o_ref[0] = acc


def _bias_table(rel_bias):
    shape = (BLOCK, 3 * BLOCK)
    return pl.pallas_call(
        _bias_body,
        grid=(Q_HEADS,),
        in_specs=[pl.BlockSpec(shape, lambda h: (0, 0)), pl.BlockSpec(memory_space=pltpu.SMEM)],
        out_specs=pl.BlockSpec((1,) + shape, lambda h: (h, 0, 0)),
        out_shape=jax.ShapeDtypeStruct((Q_HEADS,) + shape, F32),
        name="bias_table",
    )(_t5_bucket_table().astype(jnp.int32), rel_bias.astype(F32))


def _attention(q, kv, q_gain, k_gain, rel_bias, sink):
    b, t, _ = q.shape
    nb = t // BLOCK
    bias = _bias_table(rel_bias)
    cur = lambda bi, n: (bi, n, 0)
    prv = lambda bi, n: (bi, jnp.maximum(n - 1, 0), 0)
    nxt = lambda bi, n: (bi, jnp.minimum(n + 1, nb - 1), 0)
    c2 = lambda bi, n: (0, 0)
    c3 = lambda bi, n: (0, 0, 0)
    kvblk = (1, BLOCK, 2 * KV_COLS)
    return pl.pallas_call(
        _attn_body,
        grid=(b, nb),
        in_specs=[pl.BlockSpec((1, BLOCK, ATTN_WIDTH), cur), pl.BlockSpec(kvblk, prv), pl.BlockSpec(kvblk, cur),
                  pl.BlockSpec(kvblk, nxt), pl.BlockSpec((1, LANES), c2), pl.BlockSpec((1, LANES), c2),
                  pl.BlockSpec((Q_HEADS, BLOCK, 3 * BLOCK), c3), pl.BlockSpec((1, Q_HEADS), c2)],
        out_specs=pl.BlockSpec((1, BLOCK, ATTN_WIDTH), cur),
        out_shape=jax.ShapeDtypeStruct((b, t, ATTN_WIDTH), F32),
        compiler_params=pltpu.CompilerParams(dimension_semantics=("parallel", "parallel"),
                                             vmem_limit_bytes=VMEM_LIMIT),
        name="band_attn",
    )(q, kv, kv, kv, jnp.tile(q_gain, 2).reshape(1, LANES), jnp.tile(k_gain, 2).reshape(1, LANES),
      bias, sink.reshape(1, Q_HEADS))


def _outproj_body(x_ref, yf_ref, yb_ref, bonus_ref, gate_ref, attn_ref, lw_ref, lb_ref, wo_ref, gf_ref,
                  x1_ref, h2_ref):
    c = RWKV_WIDTH
    head_sum = _head_sum_matrix(c)
    y = yf_ref[...] + yb_ref[...]
    mu = _dot_exact_rhs(y, head_sum) * (1.0 / HEAD_DIM)
    yc = y - mu
    var = _dot_exact_rhs(yc * yc, head_sum) * (1.0 / HEAD_DIM)
    yn = yc * lax.rsqrt(var + LNX_EPS) * lw_ref[...] + lb_ref[...]
    mix_r = (yn + bonus_ref[...]) * gate_ref[...]
    x1 = (x_ref[...] + _dot(mix_r.astype(BF16), wo_ref[:c, :])
          + _dot(attn_ref[...].astype(BF16), wo_ref[c:, :]))
    x1_ref[...] = x1
    h2 = x1 * lax.rsqrt(jnp.mean(x1 * x1, axis=-1, keepdims=True) + NORM_EPS) * gf_ref[...]
    h2_ref[...] = h2.astype(BF16)


def _outproj(x2, yf, yb, bonus, gate, attn, lnx_w, lnx_b, w_out, g_ffn, tile):
    rows, d = x2.shape
    c = RWKV_WIDTH
    row = lambda i: (i, 0)
    const = lambda i: (0, 0)
    return pl.pallas_call(
        _outproj_body,
        grid=(rows // tile,),
        in_specs=[pl.BlockSpec((tile, d), row)] + [pl.BlockSpec((tile, c), row)] * 5
                 + [pl.BlockSpec((1, c), const), pl.BlockSpec((1, c), const),
                    pl.BlockSpec(w_out.shape, const), pl.BlockSpec((1, d), const)],
        out_specs=[pl.BlockSpec((tile, d), row), pl.BlockSpec((tile, d), row)],
        out_shape=[jax.ShapeDtypeStruct((rows, d), F32), jax.ShapeDtypeStruct((rows, d), BF16)],
        compiler_params=pltpu.CompilerParams(dimension_semantics=("parallel",), vmem_limit_bytes=VMEM_LIMIT),
        name="outproj",
    )(x2, yf, yb, bonus, gate, attn, lnx_w.reshape(1, c), lnx_b.reshape(1, c), w_out.astype(BF16),
      g_ffn.reshape(1, d))


def _ffn_body(x1_ref, h_ref, hp_ref, hn_ref, wg_ref, wv_ref, cwg_ref, cwv_ref, cbg_ref, cbv_ref, wd_ref,
              o_ref, *, tiles_per_seq):
    i = pl.program_id(0)
    j = pl.program_id(1)
    tile = h_ref.shape[0]
    halo = hp_ref.shape[0]
    first = (i % tiles_per_seq) == 0
    last = (i % tiles_per_seq) == tiles_per_seq - 1
    hp = hp_ref[...]
    hn = hn_ref[...]
    hp = jnp.where(first, jnp.zeros_like(hp), hp)
    hn = jnp.where(last, jnp.zeros_like(hn), hn)
    hcat = jnp.concatenate([hp, h_ref[...], hn], axis=0)
    total = tile + 2 * halo

    def conv(u, cw_ref, cb_ref):
        up = pltpu.roll(u, 1, 0)[halo:halo + tile]
        un = pltpu.roll(u, total - 1, 0)[halo:halo + tile]
        return up * cw_ref[0:1, :] + u[halo:halo + tile] * cw_ref[1:2, :] + un * cw_ref[2:3, :] + cb_ref[...]

    g = conv(_dot(hcat, wg_ref[...]), cwg_ref, cbg_ref)
    val = conv(_dot(hcat, wv_ref[...]), cwv_ref, cbv_ref)
    act = (g * _sigmoid(g) * val).astype(BF16)
    part = _dot(act, wd_ref[...])

    @pl.when(j == 0)
    def _():
        o_ref[...] = x1_ref[...] + part

    @pl.when(j != 0)
    def _():
        o_ref[...] += part


def _ffn(x1, h2, w_up, conv_w, conv_b, w_down, seq, tile, ff_tile):
    rows, d = x1.shape
    d_ff = w_down.shape[0]
    nj = d_ff // ff_tile
    halo = SUBLANES_BF16
    per = tile // halo
    wub = w_up.astype(BF16)
    wdb = w_down.astype(BF16)
    cb = conv_b.reshape(1, 2 * d_ff)
    row = lambda i, j: (i, 0)
    prv = lambda i, j: (jnp.maximum(i * per - 1, 0), 0)
    nxt = lambda i, j: (jnp.minimum((i + 1) * per, rows // halo - 1), 0)
    gcol = lambda i, j: (0, j)
    vcol = lambda i, j: (0, nj + j)
    body = functools.partial(_ffn_body, tiles_per_seq=seq // tile)
    return pl.pallas_call(
        body,
        grid=(rows // tile, nj),
        in_specs=[pl.BlockSpec((tile, d), row), pl.BlockSpec((tile, d), row),
                  pl.BlockSpec((halo, d), prv), pl.BlockSpec((halo, d), nxt),
                  pl.BlockSpec((d, ff_tile), gcol), pl.BlockSpec((d, ff_tile), vcol),
                  pl.BlockSpec((3, ff_tile), gcol), pl.BlockSpec((3, ff_tile), vcol),
                  pl.BlockSpec((1, ff_tile), gcol), pl.BlockSpec((1, ff_tile), vcol),
                  pl.BlockSpec((ff_tile, d), lambda i, j: (j, 0))],
        out_specs=pl.BlockSpec((tile, d), row),
        out_shape=jax.ShapeDtypeStruct((rows, d), F32),
        compiler_params=pltpu.CompilerParams(dimension_semantics=("parallel", "arbitrary"),
                                             vmem_limit_bytes=VMEM_LIMIT),
        name="convglu",
    )(x1, h2, h2, h2, wub, wub, conv_w, conv_w, cb, cb, wdb)


def _layer(x, g_mix, w_in, mu_prev, mu_next, w0, w2, a0, a2, g2, k_k, k_a, r_k, lnx_w, lnx_b,
           q_gain, k_gain, rel_bias, sink, w_out, g_ffn, w_up, conv_w, conv_b, w_down):
    b, t, d = x.shape
    rows = b * t
    x2 = x.reshape(rows, d)
    p, q, kv = _inproj(x2, g_mix, w_in, tile=min(512, t))
    ops = _rwkv_prep(p.reshape(b, t, RWKV_COLS), mu_prev, mu_next, w0, w2, a0, a2, g2, k_k, k_a,
                     r_k.reshape(-1), tile=min(256, t))
    r, v, kk, lw0, lw1, kd0, kd1, b0, b1, gate, bonus = ops
    yf, yb = _rwkv_scan(r, v, kk, lw0, lw1, kd0, kd1, b0, b1)
    attn = _attention(q.reshape(b, t, ATTN_WIDTH), kv.reshape(b, t, 2 * KV_COLS), q_gain, k_gain, rel_bias, sink)
    flat = lambda z: z.reshape(rows, z.shape[-1])
    x1, h2 = _outproj(x2, flat(yf), flat(yb), flat(bonus), flat(gate), flat(attn), lnx_w, lnx_b, w_out, g_ffn,
                      tile=min(512, t))
    out = _ffn(x1, h2, w_up, conv_w, conv_b, w_down, seq=t, tile=min(512, t), ff_tile=256)
    return out.reshape(b, t, d)


def kernel(x, g_mix, w_in, mu_prev, mu_next, w0, w2, a0, a2, g2, k_k, k_a, r_k, lnx_w, lnx_b, q_gain, k_gain,
           rel_bias, sink, w_out, g_ffn, w_up, conv_w, conv_b, w_down):
    depth = g_mix.shape[0]
    for l in range(depth):
        x = _layer(x, g_mix[l], w_in[l], mu_prev[l], mu_next[l], w0[l], w2[l], a0[l], a2[l], g2[l], k_k[l], k_a[l],
                   r_k[l], lnx_w[l], lnx_b[l], q_gain[l], k_gain[l], rel_bias, sink[l], w_out[l], g_ffn[l],
                   w_up[l], conv_w[l], conv_b[l], w_down[l])
    return x
```

```python
import functools
import math

import jax
import jax.numpy as jnp
from jax import lax
from jax.experimental import pallas as pl
from jax.experimental.pallas import tpu as pltpu

F32 = jnp.float32
BF16 = jnp.bfloat16

HEAD_DIM = 64
RWKV_WIDTH = 512
ATTN_WIDTH = 512
KV_HEADS = 2
Q_HEADS = 8
DECAY_LORA = 64
ICLR_LORA = 64
GATE_LORA = 128
RWKV_COLS = 3 * RWKV_WIDTH + DECAY_LORA + ICLR_LORA + GATE_LORA
KV_COLS = KV_HEADS * HEAD_DIM
WINDOW = 128
BLOCK = 128
REL_BUCKETS = 32
REL_MAX_DIST = 128
NORM_EPS = 1e-6
LNX_EPS = 64e-5
KK_EPS = 1e-12

LANES = 128
SUBLANES_F32 = 8
SUBLANES_BF16 = 16
VMEM_LIMIT = 48 * 1024 * 1024

CHUNK = 64
PAIR = 2 * HEAD_DIM
assert PAIR == LANES
HEAD_SHIFT = HEAD_DIM.bit_length() - 1
assert 1 << HEAD_SHIFT == HEAD_DIM


def _dot(a, b, precision=None):
    return jnp.dot(a, b, preferred_element_type=F32, precision=precision)


def _dot_nt(a, b, precision=None):
    return lax.dot_general(a, b, (((1,), (1,)), ((), ())), preferred_element_type=F32, precision=precision)


def _dot_tn(a, b, precision=None):
    return lax.dot_general(a, b, (((0,), (0,)), ((), ())), preferred_element_type=F32, precision=precision)


def _sigmoid(x):
    return 1.0 / (1.0 + jnp.exp(-x))


def _split_bf16(x, parts):
    out = []
    for _ in range(parts):
        h = x.astype(BF16)
        out.append(h)
        x = x - h.astype(F32)
    return out


def _dot_exact_rhs(x, m, parts=2):
    acc = None
    for h in _split_bf16(x, parts):
        t = _dot(h, m)
        acc = t if acc is None else acc + t
    return acc


def _head_sum_matrix(width):
    r = lax.broadcasted_iota(jnp.int32, (width, width), 0) >> HEAD_SHIFT
    c = lax.broadcasted_iota(jnp.int32, (width, width), 1) >> HEAD_SHIFT
    return (r == c).astype(BF16)


def _inproj_body(x_ref, g_ref, wr_ref, wq_ref, wkv_ref, p_ref, q_ref, kv_ref):
    x = x_ref[...]
    h = x * lax.rsqrt(jnp.mean(x * x, axis=-1, keepdims=True) + NORM_EPS) * g_ref[...]
    hb = h.astype(BF16)
    p_ref[...] = _dot(hb, wr_ref[...])
    q_ref[...] = _dot(hb, wq_ref[...])
    kv_ref[...] = _dot(hb, wkv_ref[...])


def _inproj(x2, g_mix, w_in, tile):
    rows, d = x2.shape
    wb = w_in.astype(BF16)
    wr, wq, wkv = wb[:, :RWKV_COLS], wb[:, RWKV_COLS:RWKV_COLS + ATTN_WIDTH], wb[:, RWKV_COLS + ATTN_WIDTH:]
    const = lambda i: (0, 0)
    row = lambda i: (i, 0)
    return pl.pallas_call(
        _inproj_body,
        grid=(rows // tile,),
        in_specs=[pl.BlockSpec((tile, d), row), pl.BlockSpec((1, d), const),
                  pl.BlockSpec(wr.shape, const), pl.BlockSpec(wq.shape, const), pl.BlockSpec(wkv.shape, const)],
        out_specs=[pl.BlockSpec((tile, RWKV_COLS), row), pl.BlockSpec((tile, ATTN_WIDTH), row),
                   pl.BlockSpec((tile, 2 * KV_COLS), row)],
        out_shape=[jax.ShapeDtypeStruct((rows, RWKV_COLS), F32), jax.ShapeDtypeStruct((rows, ATTN_WIDTH), F32),
                   jax.ShapeDtypeStruct((rows, 2 * KV_COLS), F32)],
        compiler_params=pltpu.CompilerParams(dimension_semantics=("parallel",), vmem_limit_bytes=VMEM_LIMIT),
        name="inproj",
    )(x2, g_mix.reshape(1, d), wr, wq, wkv)


def _prep_body(p_ref, pp_ref, pn_ref, mup_ref, mun_ref, w0_ref, w2_ref, a0_ref, a2_ref, g2_ref,
               kk_ref, ka_ref, rk_ref,
               r_o, v_o, kk_o, lw0_o, lw1_o, kd0_o, kd1_o, b0_o, b1_o, gate_o, bonus_o):
    i = pl.program_id(1)
    last = pl.num_programs(1) - 1
    p = p_ref[0]
    tt = p.shape[0]
    prev_row = jnp.where(i == 0, 0.0, pp_ref[0, SUBLANES_F32 - 1:SUBLANES_F32, :])
    next_row = jnp.where(i == last, 0.0, pn_ref[0, 0:1, :])
    row = lax.broadcasted_iota(jnp.int32, p.shape, 0)
    prev = jnp.where(row == 0, prev_row, pltpu.roll(p, 1, 0))
    nxt = jnp.where(row == tt - 1, next_row, pltpu.roll(p, tt - 1, 0))
    pf = p + mup_ref[...] * (prev - p) + mun_ref[...] * (nxt - p)

    c = RWKV_WIDTH
    r, k, v = pf[:, :c], pf[:, c:2 * c], pf[:, 2 * c:3 * c]
    lora = pf[:, 3 * c:3 * c + LANES]
    xg = pf[:, 3 * c + LANES:]
    lora_t = jnp.tanh(lora).astype(BF16)
    lora_b = lora.astype(BF16)

    head_sum = _head_sum_matrix(c)
    kx = k * kk_ref[...]
    kk = kx * lax.rsqrt(_dot_exact_rhs(kx * kx, head_sum) + KK_EPS)
    gate = _dot(_sigmoid(xg).astype(BF16), g2_ref[...])

    kds = []
    for d, (lw_o, kd_o, b_o) in enumerate(((lw0_o, kd0_o, b0_o), (lw1_o, kd1_o, b1_o))):
        w_raw = w0_ref[d:d + 1, :] + _dot(lora_t, w2_ref[d])
        lw_o[0] = (-math.exp(-0.5)) * _sigmoid(w_raw)
        iclr = _sigmoid(a0_ref[d:d + 1, :] + _dot(lora_b, a2_ref[d]))
        kd = k * (1.0 + (iclr - 1.0) * ka_ref[...])
        kd_o[0] = kd
        b_o[0] = iclr * kk
        kds.append(kd)

    r_o[0] = r
    v_o[0] = v
    kk_o[0] = kk
    gate_o[0] = gate
    bonus_o[0] = _dot_exact_rhs(r * (kds[0] + kds[1]) * rk_ref[...], head_sum) * v


def _rwkv_prep(p3, mu_prev, mu_next, w0, w2, a0, a2, g2, k_k, k_a, r_k, tile):
    b, t, cols = p3.shape
    c = RWKV_WIDTH
    nt = t // tile
    per8 = tile // SUBLANES_F32
    zeros = jnp.zeros((2, DECAY_LORA, c), F32)
    w2p = jnp.concatenate([w2, zeros], axis=1).astype(BF16)
    a2p = jnp.concatenate([zeros, a2], axis=1).astype(BF16)
    tok = lambda bi, i: (bi, i, 0)
    prev8 = lambda bi, i: (bi, jnp.maximum(i * per8 - 1, 0), 0)
    next8 = lambda bi, i: (bi, jnp.minimum((i + 1) * per8, t // SUBLANES_F32 - 1), 0)
    c2 = lambda bi, i: (0, 0)
    c3 = lambda bi, i: (0, 0, 0)
    out = jax.ShapeDtypeStruct((b, t, c), F32)
    return pl.pallas_call(
        _prep_body,
        grid=(b, nt),
        in_specs=[pl.BlockSpec((1, tile, cols), tok), pl.BlockSpec((1, SUBLANES_F32, cols), prev8),
                  pl.BlockSpec((1, SUBLANES_F32, cols), next8),
                  pl.BlockSpec((1, cols), c2), pl.BlockSpec((1, cols), c2),
                  pl.BlockSpec((2, c), c2), pl.BlockSpec((2, LANES, c), c3),
                  pl.BlockSpec((2, c), c2), pl.BlockSpec((2, LANES, c), c3),
                  pl.BlockSpec((GATE_LORA, c), c2),
                  pl.BlockSpec((1, c), c2), pl.BlockSpec((1, c), c2), pl.BlockSpec((1, c), c2)],
        out_specs=[pl.BlockSpec((1, tile, c), tok)] * 11,
        out_shape=[out] * 11,
        compiler_params=pltpu.CompilerParams(dimension_semantics=("parallel", "parallel"),
                                             vmem_limit_bytes=VMEM_LIMIT),
        name="rwkv_prep",
    )(p3, p3, p3, mu_prev.reshape(1, cols), mu_next.reshape(1, cols), w0, w2p, a0, a2p, g2.astype(BF16),
      k_k.reshape(1, c), k_a.reshape(1, c), r_k.reshape(1, c))


def _scan_body(rf, vf, kkf, lwf, kdf, bf, rb, vb, kkb, lwb, kdb, bb, yf_o, yb_o, h_scr):
    j = pl.program_id(1)

    @pl.when(j == 0)
    def _():
        h_scr[...] = jnp.zeros_like(h_scr)

    cs = rf.shape[1]
    n_pairs = rf.shape[2] // PAIR

    ri = lax.broadcasted_iota(jnp.int32, (2 * cs, 2 * cs), 0)
    ci = lax.broadcasted_iota(jnp.int32, (2 * cs, 2 * cs), 1)
    cs_shift = cs.bit_length() - 1
    assert 1 << cs_shift == cs
    same_head = (ri >> cs_shift) == (ci >> cs_shift)
    t_row, t_col = ri & (cs - 1), ci & (cs - 1)
    eye = (ri == ci).astype(F32)
    tr = lax.broadcasted_iota(jnp.int32, (cs, cs), 0)
    tc = lax.broadcasted_iota(jnp.int32, (cs, cs), 1)
    lane_even = lax.broadcasted_iota(jnp.int32, (cs, PAIR), 1) < HEAD_DIM

    def stack(x):
        return jnp.concatenate([jnp.where(lane_even, x, 0.0), jnp.where(lane_even, 0.0, x)], axis=0)

    def unstack(x):
        return x[:cs] + x[cs:]

    dirs = ((0, rf, vf, kkf, lwf, kdf, bf, yf_o), (1, rb, vb, kkb, lwb, kdb, bb, yb_o))
    chains = []
    for d, r_ref, v_ref, kk_ref, lw_ref, kd_ref, b_ref, y_o in dirs:
        fwd = d == 0
        before = (t_col < t_row) if fwd else (t_col > t_row)
        strict = same_head & before
        incl = same_head & (before | (t_col == t_row))
        cum = ((tc <= tr) if fwd else (tc >= tr)).astype(BF16)

        lw = lw_ref[0]
        c_in = sum(_dot(cum, piece) for piece in _split_bf16(lw, 3))
        c_ex = c_in - lw
        c_tot = jnp.sum(lw, axis=0, keepdims=True)
        e_neg = jnp.exp(-c_in)
        e_end = jnp.exp(c_tot - c_in)
        gam = jnp.exp(c_tot)
        kd = kd_ref[0]
        bv = b_ref[0]
        a_t = -kk_ref[0] * jnp.exp(c_ex)
        r_t = r_ref[0] * jnp.exp(c_in)
        b_t = bv * e_neg
        k_t = kd * e_neg
        b_h = bv * e_end
        k_h = kd * e_end
        vv = v_ref[0]
        for pr in range(n_pairs):
            sl = slice(pr * PAIR, (pr + 1) * PAIR)
            r_st = stack(r_t[:, sl])
            chains.append(dict(
                d=d, pr=pr, sl=sl, y_o=y_o, strict=strict, incl=incl, r_st=r_st, gam=gam[:, sl],
                a_sb=stack(a_t[:, sl]).astype(BF16), r_sb=r_st.astype(BF16), v_sb=stack(vv[:, sl]).astype(BF16),
                bt=b_t[:, sl].astype(BF16), kt=k_t[:, sl].astype(BF16),
                b_sb=stack(b_h[:, sl]).astype(BF16), k_sb=stack(k_h[:, sl]).astype(BF16)))

    h2 = 2 * cs
    for c in chains:
        lhs = jnp.concatenate([c["a_sb"], c["r_sb"]], axis=0)
        rhs = jnp.concatenate([c["bt"], c["bt"], c["kt"], c["kt"]], axis=0)
        gram = _dot_nt(lhs, rhs)
        a_ab = jnp.where(c["strict"], gram[:h2, :h2], 0.0)
        c["a_ak"] = jnp.where(c["strict"], gram[:h2, h2:], 0.0).astype(BF16)
        c["a_rb"] = jnp.where(c["incl"], gram[h2:, :h2], 0.0).astype(BF16)
        c["a_rk"] = jnp.where(c["incl"], gram[h2:, h2:], 0.0).astype(BF16)
        c["xb"] = a_ab.astype(BF16)
        c["tinv"] = eye + a_ab

    for _ in range(cs_shift - 1):
        for c in chains:
            c["xb"] = _dot(c["xb"], c["xb"]).astype(BF16)
        for c in chains:
            c["tinv"] = c["tinv"] + _dot(c["tinv"].astype(BF16), c["xb"])

    for c in chains:
        c["w1"] = _dot(c["a_ak"], c["v_sb"]).astype(BF16)
        c["av"] = _dot(c["a_rk"], c["v_sb"])
        c["kv"] = _dot_tn(c["k_sb"], c["v_sb"])
        c["h"] = _split_bf16(h_scr[c["d"], c["pr"]], 2)
    for c in chains:
        c["pq"] = _dot(c["tinv"].astype(BF16), jnp.concatenate([c["a_sb"], c["w1"]], axis=1)).astype(BF16)
    for c in chains:
        ry = _dot(c["a_rb"], c["pq"])
        c["r_hat"] = unstack(c["r_st"] + ry[:, :PAIR]).astype(BF16)
        c["y_hat"] = unstack(ry[:, PAIR:] + c["av"])
        gd = _dot_tn(c["b_sb"], c["pq"])
        c["g"] = _split_bf16(eye * c["gam"] + gd[:, :PAIR], 2)
        c["dd"] = gd[:, PAIR:] + c["kv"]
    for c in chains:
        h_hi, h_lo = c["h"]
        g_hi, g_lo = c["g"]
        c["y_o"][0, :, c["sl"]] = _dot(c["r_hat"], h_hi) + c["y_hat"]
        h_scr[c["d"], c["pr"]] = _dot(g_hi, h_hi) + _dot(g_lo, h_hi) + _dot(g_hi, h_lo) + c["dd"]


def _rwkv_scan(r, v, kk, lw0, lw1, kd0, kd1, b0, b1):
    b, t, c = r.shape
    nc = t // CHUNK
    fw = lambda bi, j: (bi, j, 0)
    bw = lambda bi, j: (bi, nc - 1 - j, 0)
    blk = (1, CHUNK, c)
    out = jax.ShapeDtypeStruct((b, t, c), F32)
    return pl.pallas_call(
        _scan_body,
        grid=(b, nc),
        in_specs=[pl.BlockSpec(blk, fw)] * 6 + [pl.BlockSpec(blk, bw)] * 6,
        out_specs=[pl.BlockSpec(blk, fw), pl.BlockSpec(blk, bw)],
        out_shape=[out, out],
        scratch_shapes=[pltpu.VMEM((2, c // PAIR, PAIR, PAIR), F32)],
        compiler_params=pltpu.CompilerParams(dimension_semantics=("arbitrary", "arbitrary"),
                                             vmem_limit_bytes=VMEM_LIMIT),
        name="rwkv_scan",
    )(r, v, kk, lw0, kd0, b0, r, v, kk, lw1, kd1, b1)


def _attn_body(q_ref, kvp_ref, kvc_ref, kvn_ref, qg_ref, kg_ref, bias_ref, sink_ref, o_ref):
    n = pl.program_id(1)
    nb = pl.num_programs(1)
    blk = q_ref.shape[1]
    head_sum = _head_sum_matrix(LANES)
    head_mean = lambda z: _dot_exact_rhs(z, head_sum) * (1.0 / HEAD_DIM)

    k_win = jnp.concatenate([kvp_ref[0, :, :KV_COLS], kvc_ref[0, :, :KV_COLS], kvn_ref[0, :, :KV_COLS]], axis=0)
    v_win = jnp.concatenate([kvp_ref[0, :, KV_COLS:], kvc_ref[0, :, KV_COLS:], kvn_ref[0, :, KV_COLS:]], axis=0)
    kn = k_win * lax.rsqrt(head_mean(k_win * k_win) + NORM_EPS) * kg_ref[...]
    v_b = v_win.astype(BF16)

    lane = lax.broadcasted_iota(jnp.int32, kn.shape, 1)
    k_at = []
    for g in range(KV_HEADS):
        own = jnp.where((lane >> HEAD_SHIFT) == g, kn, 0.0)
        other = pltpu.roll(own, HEAD_DIM, 1)
        k_at.append([own if p == g else other for p in range(2)])

    row = lax.broadcasted_iota(jnp.int32, (blk, 3 * blk), 0)
    col = lax.broadcasted_iota(jnp.int32, (blk, 3 * blk), 1)
    rel = col - blk - row
    valid = (jnp.abs(rel) <= WINDOW)
    valid &= (col >= blk) | (n > 0)
    valid &= (col < 2 * blk) | (n < nb - 1)
    out_lane_even = lax.broadcasted_iota(jnp.int32, (blk, LANES), 1) < HEAD_DIM

    group = Q_HEADS // KV_HEADS
    n_slabs = ATTN_WIDTH // LANES
    slabs_per_group = n_slabs // KV_HEADS
    qn = []
    for s in range(n_slabs):
        q = q_ref[0, :, s * LANES:(s + 1) * LANES]
        qn.append((q * lax.rsqrt(head_mean(q * q) + NORM_EPS) * qg_ref[...] * (HEAD_DIM ** -0.5)).astype(BF16))
    scores = []
    for g in range(KV_HEADS):
        kcat = jnp.concatenate([k_at[g][0], k_at[g][1]], axis=0).astype(BF16)
        q_g = jnp.concatenate(qn[g * slabs_per_group:(g + 1) * slabs_per_group], axis=0)
        scores.append(_dot_nt(q_g, kcat))
    heads = range(Q_HEADS)
    sc = []
    for h in heads:
        s, p = h // 2, h % 2
        g, sg = s // slabs_per_group, s % slabs_per_group
        z = scores[g][sg * blk:(sg + 1) * blk, p * 3 * blk:(p + 1) * 3 * blk] + bias_ref[h]
        sc.append(jnp.where(valid, z, -jnp.inf))
    sinks = [sink_ref[h] for h in heads]
    m = [jnp.maximum(jnp.max(sc[h], axis=-1, keepdims=True), sinks[h]) for h in heads]
    e = [jnp.exp(sc[h] - m[h]) for h in heads]
    denom = [jnp.sum(e[h], axis=-1, keepdims=True) + jnp.exp(sinks[h] - m[h]) for h in heads]
    probs = [(e[h] / denom[h]).astype(BF16) for h in heads]
    o_all = _dot(jnp.concatenate(probs, axis=0), v_b)
    for s in range(n_slabs):
        g = (2 * s) // group
        halves = []
        for p in range(2):
            h = 2 * s + p
            o = o_all[h * blk:(h + 1) * blk]
            halves.append(o if p == g else pltpu.roll(o, HEAD_DIM, 1))
        o_ref[0, :, s * LANES:(s + 1) * LANES] = jnp.where(out_lane_even, halves[0], halves[1])


def _t5_bucket_table():
    nb = REL_BUCKETS // 2
    max_exact = nb // 2
    rel = (jnp.arange(3 * BLOCK)[None, :] - BLOCK) - jnp.arange(BLOCK)[:, None]
    ret = jnp.where(rel > 0, nb, 0)
    n = jnp.abs(rel)
    large = max_exact + (jnp.log(jnp.maximum(n, 1).astype(F32) / max_exact)
                         / math.log(REL_MAX_DIST / max_exact) * (nb - max_exact)).astype(jnp.int32)
    large = jnp.minimum(large, nb - 1)
    return ret + jnp.where(n < max_exact, n, large)


def _bias_body(bucket_ref, rb_ref, o_ref):
    h = pl.program_id(0)
    bucket = bucket_ref[...]
    acc = jnp.zeros(bucket.shape, F32)
    for bkt in range(REL_BUCKETS):
        acc = jnp.where(bucket == bkt, rb_ref[bkt, h], acc)
    o_ref[0] = acc


def _bias_table(rel_bias):
    shape = (BLOCK, 3 * BLOCK)
    return pl.pallas_call(
        _bias_body,
        grid=(Q_HEADS,),
        in_specs=[pl.BlockSpec(shape, lambda h: (0, 0)), pl.BlockSpec(memory_space=pltpu.SMEM)],
        out_specs=pl.BlockSpec((1,) + shape, lambda h: (h, 0, 0)),
        out_shape=jax.ShapeDtypeStruct((Q_HEADS,) + shape, F32),
        name="bias_table",
    )(_t5_bucket_table().astype(jnp.int32), rel_bias.astype(F32))


def _attention(q, kv, q_gain, k_gain, rel_bias, sink):
    b, t, _ = q.shape
    nb = t // BLOCK
    bias = _bias_table(rel_bias)
    cur = lambda bi, n: (bi, n, 0)
    prv = lambda bi, n: (bi, jnp.maximum(n - 1, 0), 0)
    nxt = lambda bi, n: (bi, jnp.minimum(n + 1, nb - 1), 0)
    c2 = lambda bi, n: (0, 0)
    c3 = lambda bi, n: (0, 0, 0)
    kvblk = (1, BLOCK, 2 * KV_COLS)
    return pl.pallas_call(
        _attn_body,
        grid=(b, nb),
        in_specs=[pl.BlockSpec((1, BLOCK, ATTN_WIDTH), cur), pl.BlockSpec(kvblk, prv), pl.BlockSpec(kvblk, cur),
                  pl.BlockSpec(kvblk, nxt), pl.BlockSpec((1, LANES), c2), pl.BlockSpec((1, LANES), c2),
                  pl.BlockSpec((Q_HEADS, BLOCK, 3 * BLOCK), c3), pl.BlockSpec(memory_space=pltpu.SMEM)],
        out_specs=pl.BlockSpec((1, BLOCK, ATTN_WIDTH), cur),
        out_shape=jax.ShapeDtypeStruct((b, t, ATTN_WIDTH), F32),
        compiler_params=pltpu.CompilerParams(dimension_semantics=("parallel", "parallel"),
                                             vmem_limit_bytes=VMEM_LIMIT),
        name="band_attn",
    )(q, kv, kv, kv, jnp.tile(q_gain, 2).reshape(1, LANES), jnp.tile(k_gain, 2).reshape(1, LANES),
      bias, sink.astype(F32))


def _outproj_body(x_ref, yf_ref, yb_ref, bonus_ref, gate_ref, attn_ref, lw_ref, lb_ref, wo_ref, gf_ref,
                  x1_ref, h2_ref):
    c = RWKV_WIDTH
    head_sum = _head_sum_matrix(c)
    y = yf_ref[...] + yb_ref[...]
    mu = _dot_exact_rhs(y, head_sum) * (1.0 / HEAD_DIM)
    yc = y - mu
    var = _dot_exact_rhs(yc * yc, head_sum) * (1.0 / HEAD_DIM)
    yn = yc * lax.rsqrt(var + LNX_EPS) * lw_ref[...] + lb_ref[...]
    mix_r = (yn + bonus_ref[...]) * gate_ref[...]
    x1 = (x_ref[...] + _dot(mix_r.astype(BF16), wo_ref[:c, :])
          + _dot(attn_ref[...].astype(BF16), wo_ref[c:, :]))
    x1_ref[...] = x1
    h2 = x1 * lax.rsqrt(jnp.mean(x1 * x1, axis=-1, keepdims=True) + NORM_EPS) * gf_ref[...]
    h2_ref[...] = h2.astype(BF16)


def _outproj(x2, yf, yb, bonus, gate, attn, lnx_w, lnx_b, w_out, g_ffn, tile):
    rows, d = x2.shape
    c = RWKV_WIDTH
    row = lambda i: (i, 0)
    const = lambda i: (0, 0)
    return pl.pallas_call(
        _outproj_body,
        grid=(rows // tile,),
        in_specs=[pl.BlockSpec((tile, d), row)] + [pl.BlockSpec((tile, c), row)] * 5
                 + [pl.BlockSpec((1, c), const), pl.BlockSpec((1, c), const),
                    pl.BlockSpec(w_out.shape, const), pl.BlockSpec((1, d), const)],
        out_specs=[pl.BlockSpec((tile, d), row), pl.BlockSpec((tile, d), row)],
        out_shape=[jax.ShapeDtypeStruct((rows, d), F32), jax.ShapeDtypeStruct((rows, d), BF16)],
        compiler_params=pltpu.CompilerParams(dimension_semantics=("parallel",), vmem_limit_bytes=VMEM_LIMIT),
        name="outproj",
    )(x2, yf, yb, bonus, gate, attn, lnx_w.reshape(1, c), lnx_b.reshape(1, c), w_out.astype(BF16),
      g_ffn.reshape(1, d))


def _ffn_body(x1_ref, h_ref, hp_ref, hn_ref, wu_ref, cw_ref, cb_ref, wd_ref, o_ref, act_scr,
              *, tiles_per_seq, ff_chunk):
    i = pl.program_id(0)
    tile = h_ref.shape[0]
    halo = hp_ref.shape[0]
    d_ff = wd_ref.shape[0]
    first = (i % tiles_per_seq) == 0
    last = (i % tiles_per_seq) == tiles_per_seq - 1
    hp = hp_ref[...]
    hn = hn_ref[...]
    hp = jnp.where(first, jnp.zeros_like(hp), hp)
    hn = jnp.where(last, jnp.zeros_like(hn), hn)
    hcat = jnp.concatenate([hp, h_ref[...], hn], axis=0)
    total = tile + 2 * halo

    def conv(cols):
        u = _dot(hcat, wu_ref[:, cols])
        up = pltpu.roll(u, 1, 0)[halo:halo + tile]
        un = pltpu.roll(u, total - 1, 0)[halo:halo + tile]
        return (up * cw_ref[0:1, cols] + u[halo:halo + tile] * cw_ref[1:2, cols] + un * cw_ref[2:3, cols]
                + cb_ref[:, cols])

    for c in range(d_ff // ff_chunk):
        g = conv(slice(c * ff_chunk, (c + 1) * ff_chunk))
        val = conv(slice(d_ff + c * ff_chunk, d_ff + (c + 1) * ff_chunk))
        act_scr[:, c * ff_chunk:(c + 1) * ff_chunk] = (g * _sigmoid(g) * val).astype(BF16)
    o_ref[...] = x1_ref[...] + _dot(act_scr[...], wd_ref[...])


def _ffn(x1, h2, w_up, conv_w, conv_b, w_down, seq, tile, ff_chunk):
    rows, d = x1.shape
    d_ff = w_down.shape[0]
    halo = SUBLANES_BF16
    per = tile // halo
    row = lambda i: (i, 0)
    prv = lambda i: (jnp.maximum(i * per - 1, 0), 0)
    nxt = lambda i: (jnp.minimum((i + 1) * per, rows // halo - 1), 0)
    const = lambda i: (0, 0)
    resident = dict(pipeline_mode=pl.Buffered(1))
    body = functools.partial(_ffn_body, tiles_per_seq=seq // tile, ff_chunk=ff_chunk)
    return pl.pallas_call(
        body,
        grid=(rows // tile,),
        in_specs=[pl.BlockSpec((tile, d), row), pl.BlockSpec((tile, d), row),
                  pl.BlockSpec((halo, d), prv), pl.BlockSpec((halo, d), nxt),
                  pl.BlockSpec((d, 2 * d_ff), const, **resident),
                  pl.BlockSpec((3, 2 * d_ff), const), pl.BlockSpec((1, 2 * d_ff), const),
                  pl.BlockSpec((d_ff, d), const, **resident)],
        out_specs=pl.BlockSpec((tile, d), row),
        out_shape=jax.ShapeDtypeStruct((rows, d), F32),
        scratch_shapes=[pltpu.VMEM((tile, d_ff), BF16)],
        compiler_params=pltpu.CompilerParams(dimension_semantics=("parallel",), vmem_limit_bytes=VMEM_LIMIT),
        name="convglu",
    )(x1, h2, h2, h2, w_up.astype(BF16), conv_w, conv_b.reshape(1, 2 * d_ff), w_down.astype(BF16))


def _layer(x, g_mix, w_in, mu_prev, mu_next, w0, w2, a0, a2, g2, k_k, k_a, r_k, lnx_w, lnx_b,
           q_gain, k_gain, rel_bias, sink, w_out, g_ffn, w_up, conv_w, conv_b, w_down):
    b, t, d = x.shape
    rows = b * t
    x2 = x.reshape(rows, d)
    p, q, kv = _inproj(x2, g_mix, w_in, tile=min(512, t))
    ops = _rwkv_prep(p.reshape(b, t, RWKV_COLS), mu_prev, mu_next, w0, w2, a0, a2, g2, k_k, k_a,
                     r_k.reshape(-1), tile=min(256, t))
    r, v, kk, lw0, lw1, kd0, kd1, b0, b1, gate, bonus = ops
    yf, yb = _rwkv_scan(r, v, kk, lw0, lw1, kd0, kd1, b0, b1)
    attn = _attention(q.reshape(b, t, ATTN_WIDTH), kv.reshape(b, t, 2 * KV_COLS), q_gain, k_gain, rel_bias, sink)
    flat = lambda z: z.reshape(rows, z.shape[-1])
    x1, h2 = _outproj(x2, flat(yf), flat(yb), flat(bonus), flat(gate), flat(attn), lnx_w, lnx_b, w_out, g_ffn,
                      tile=min(512, t))
    out = _ffn(x1, h2, w_up, conv_w, conv_b, w_down, seq=t, tile=min(512, t), ff_chunk=256)
    return out.reshape(b, t, d)


def kernel(x, g_mix, w_in, mu_prev, mu_next, w0, w2, a0, a2, g2, k_k, k_a, r_k, lnx_w, lnx_b, q_gain, k_gain,
           rel_bias, sink, w_out, g_ffn, w_up, conv_w, conv_b, w_down):
    depth = g_mix.shape[0]
    for l in range(depth):
        x = _layer(x, g_mix[l], w_in[l], mu_prev[l], mu_next[l], w0[l], w2[l], a0[l], a2[l], g2[l], k_k[l], k_a[l],
                   r_k[l], lnx_w[l], lnx_b[l], q_gain[l], k_gain[l], rel_bias, sink[l], w_out[l], g_ffn[l],
                   w_up[l], conv_w[l], conv_b[l], w_down[l])
    return x
```

```python
import functools
import math

import jax
import jax.numpy as jnp
from jax import lax
from jax.experimental import pallas as pl
from jax.experimental.pallas import tpu as pltpu

F32 = jnp.float32
BF16 = jnp.bfloat16

HEAD_DIM = 64
RWKV_WIDTH = 512
ATTN_WIDTH = 512
KV_HEADS = 2
Q_HEADS = 8
DECAY_LORA = 64
ICLR_LORA = 64
GATE_LORA = 128
RWKV_COLS = 3 * RWKV_WIDTH + DECAY_LORA + ICLR_LORA + GATE_LORA
KV_COLS = KV_HEADS * HEAD_DIM
WINDOW = 128
BLOCK = 128
REL_BUCKETS = 32
REL_MAX_DIST = 128
NORM_EPS = 1e-6
LNX_EPS = 64e-5
KK_EPS = 1e-12

LANES = 128
SUBLANES_F32 = 8
SUBLANES_BF16 = 16
VMEM_LIMIT = 48 * 1024 * 1024

CHUNK = 64
PAIR = 2 * HEAD_DIM
assert PAIR == LANES
HEAD_SHIFT = HEAD_DIM.bit_length() - 1
assert 1 << HEAD_SHIFT == HEAD_DIM


def _dot(a, b, precision=None):
    return jnp.dot(a, b, preferred_element_type=F32, precision=precision)


def _dot_nt(a, b, precision=None):
    return lax.dot_general(a, b, (((1,), (1,)), ((), ())), preferred_element_type=F32, precision=precision)


def _dot_tn(a, b, precision=None):
    return lax.dot_general(a, b, (((0,), (0,)), ((), ())), preferred_element_type=F32, precision=precision)


def _sigmoid(x):
    return 1.0 / (1.0 + jnp.exp(-x))


def _split_bf16(x, parts):
    out = []
    for _ in range(parts):
        h = x.astype(BF16)
        out.append(h)
        x = x - h.astype(F32)
    return out


def _dot_exact_rhs(x, m, parts=2):
    acc = None
    for h in _split_bf16(x, parts):
        t = _dot(h, m)
        acc = t if acc is None else acc + t
    return acc


def _head_sum_matrix(width):
    r = lax.broadcasted_iota(jnp.int32, (width, width), 0) >> HEAD_SHIFT
    c = lax.broadcasted_iota(jnp.int32, (width, width), 1) >> HEAD_SHIFT
    return (r == c).astype(BF16)


def _inproj_body(x_ref, g_ref, wr_ref, wq_ref, wkv_ref, p_ref, q_ref, kv_ref):
    x = x_ref[...]
    h = x * lax.rsqrt(jnp.mean(x * x, axis=-1, keepdims=True) + NORM_EPS) * g_ref[...]
    hb = h.astype(BF16)
    p_ref[...] = _dot(hb, wr_ref[...])
    q_ref[...] = _dot(hb, wq_ref[...])
    kv_ref[...] = _dot(hb, wkv_ref[...])


def _inproj(x2, g_mix, w_in, tile):
    rows, d = x2.shape
    wb = w_in.astype(BF16)
    wr, wq, wkv = wb[:, :RWKV_COLS], wb[:, RWKV_COLS:RWKV_COLS + ATTN_WIDTH], wb[:, RWKV_COLS + ATTN_WIDTH:]
    const = lambda i: (0, 0)
    row = lambda i: (i, 0)
    return pl.pallas_call(
        _inproj_body,
        grid=(rows // tile,),
        in_specs=[pl.BlockSpec((tile, d), row), pl.BlockSpec((1, d), const),
                  pl.BlockSpec(wr.shape, const), pl.BlockSpec(wq.shape, const), pl.BlockSpec(wkv.shape, const)],
        out_specs=[pl.BlockSpec((tile, RWKV_COLS), row), pl.BlockSpec((tile, ATTN_WIDTH), row),
                   pl.BlockSpec((tile, 2 * KV_COLS), row)],
        out_shape=[jax.ShapeDtypeStruct((rows, RWKV_COLS), F32), jax.ShapeDtypeStruct((rows, ATTN_WIDTH), F32),
                   jax.ShapeDtypeStruct((rows, 2 * KV_COLS), F32)],
        compiler_params=pltpu.CompilerParams(dimension_semantics=("parallel",), vmem_limit_bytes=VMEM_LIMIT),
        name="inproj",
    )(x2, g_mix.reshape(1, d), wr, wq, wkv)


def _prep_body(p_ref, pp_ref, pn_ref, mup_ref, mun_ref, w0_ref, w2_ref, a0_ref, a2_ref, g2_ref,
               kk_ref, ka_ref, rk_ref,
               r_o, v_o, kk_o, lw0_o, lw1_o, kd0_o, kd1_o, b0_o, b1_o, gate_o, bonus_o):
    i = pl.program_id(1)
    last = pl.num_programs(1) - 1
    p = p_ref[0]
    tt = p.shape[0]
    prev_row = jnp.where(i == 0, 0.0, pp_ref[0, SUBLANES_F32 - 1:SUBLANES_F32, :])
    next_row = jnp.where(i == last, 0.0, pn_ref[0, 0:1, :])
    row = lax.broadcasted_iota(jnp.int32, p.shape, 0)
    prev = jnp.where(row == 0, prev_row, pltpu.roll(p, 1, 0))
    nxt = jnp.where(row == tt - 1, next_row, pltpu.roll(p, tt - 1, 0))
    pf = p + mup_ref[...] * (prev - p) + mun_ref[...] * (nxt - p)

    c = RWKV_WIDTH
    r, k, v = pf[:, :c], pf[:, c:2 * c], pf[:, 2 * c:3 * c]
    lora = pf[:, 3 * c:3 * c + LANES]
    xg = pf[:, 3 * c + LANES:]
    lora_t = jnp.tanh(lora).astype(BF16)
    lora_b = lora.astype(BF16)

    head_sum = _head_sum_matrix(c)
    kx = k * kk_ref[...]
    kk = kx * lax.rsqrt(_dot_exact_rhs(kx * kx, head_sum) + KK_EPS)
    gate = _dot(_sigmoid(xg).astype(BF16), g2_ref[...])

    kds = []
    for d, (lw_o, kd_o, b_o) in enumerate(((lw0_o, kd0_o, b0_o), (lw1_o, kd1_o, b1_o))):
        w_raw = w0_ref[d:d + 1, :] + _dot(lora_t, w2_ref[d])
        lw_o[0] = (-math.exp(-0.5)) * _sigmoid(w_raw)
        iclr = _sigmoid(a0_ref[d:d + 1, :] + _dot(lora_b, a2_ref[d]))
        kd = k * (1.0 + (iclr - 1.0) * ka_ref[...])
        kd_o[0] = kd
        b_o[0] = iclr * kk
        kds.append(kd)

    r_o[0] = r
    v_o[0] = v
    kk_o[0] = kk
    gate_o[0] = gate
    bonus_o[0] = _dot_exact_rhs(r * (kds[0] + kds[1]) * rk_ref[...], head_sum) * v


def _rwkv_prep(p3, mu_prev, mu_next, w0, w2, a0, a2, g2, k_k, k_a, r_k, tile):
    b, t, cols = p3.shape
    c = RWKV_WIDTH
    nt = t // tile
    per8 = tile // SUBLANES_F32
    zeros = jnp.zeros((2, DECAY_LORA, c), F32)
    w2p = jnp.concatenate([w2, zeros], axis=1).astype(BF16)
    a2p = jnp.concatenate([zeros, a2], axis=1).astype(BF16)
    tok = lambda bi, i: (bi, i, 0)
    prev8 = lambda bi, i: (bi, jnp.maximum(i * per8 - 1, 0), 0)
    next8 = lambda bi, i: (bi, jnp.minimum((i + 1) * per8, t // SUBLANES_F32 - 1), 0)
    c2 = lambda bi, i: (0, 0)
    c3 = lambda bi, i: (0, 0, 0)
    out = jax.ShapeDtypeStruct((b, t, c), F32)
    return pl.pallas_call(
        _prep_body,
        grid=(b, nt),
        in_specs=[pl.BlockSpec((1, tile, cols), tok), pl.BlockSpec((1, SUBLANES_F32, cols), prev8),
                  pl.BlockSpec((1, SUBLANES_F32, cols), next8),
                  pl.BlockSpec((1, cols), c2), pl.BlockSpec((1, cols), c2),
                  pl.BlockSpec((2, c), c2), pl.BlockSpec((2, LANES, c), c3),
                  pl.BlockSpec((2, c), c2), pl.BlockSpec((2, LANES, c), c3),
                  pl.BlockSpec((GATE_LORA, c), c2),
                  pl.BlockSpec((1, c), c2), pl.BlockSpec((1, c), c2), pl.BlockSpec((1, c), c2)],
        out_specs=[pl.BlockSpec((1, tile, c), tok)] * 11,
        out_shape=[out] * 11,
        compiler_params=pltpu.CompilerParams(dimension_semantics=("parallel", "parallel"),
                                             vmem_limit_bytes=VMEM_LIMIT),
        name="rwkv_prep",
    )(p3, p3, p3, mu_prev.reshape(1, cols), mu_next.reshape(1, cols), w0, w2p, a0, a2p, g2.astype(BF16),
      k_k.reshape(1, c), k_a.reshape(1, c), r_k.reshape(1, c))


def _scan_body(rf, vf, kkf, lwf, kdf, bf, rb, vb, kkb, lwb, kdb, bb, yf_o, yb_o, h_scr):
    j = pl.program_id(0)

    @pl.when(j == 0)
    def _():
        h_scr[...] = jnp.zeros_like(h_scr)

    n_batch = rf.shape[0]
    cs = rf.shape[1]
    n_pairs = rf.shape[2] // PAIR

    ri = lax.broadcasted_iota(jnp.int32, (2 * cs, 2 * cs), 0)
    ci = lax.broadcasted_iota(jnp.int32, (2 * cs, 2 * cs), 1)
    cs_shift = cs.bit_length() - 1
    assert 1 << cs_shift == cs
    same_head = (ri >> cs_shift) == (ci >> cs_shift)
    t_row, t_col = ri & (cs - 1), ci & (cs - 1)
    eye = (ri == ci).astype(F32)
    tr = lax.broadcasted_iota(jnp.int32, (cs, cs), 0)
    tc = lax.broadcasted_iota(jnp.int32, (cs, cs), 1)
    lane_even = lax.broadcasted_iota(jnp.int32, (cs, PAIR), 1) < HEAD_DIM

    def stack(x):
        return jnp.concatenate([jnp.where(lane_even, x, 0.0), jnp.where(lane_even, 0.0, x)], axis=0)

    def unstack(x):
        return x[:cs] + x[cs:]

    dirs = ((0, rf, vf, kkf, lwf, kdf, bf, yf_o), (1, rb, vb, kkb, lwb, kdb, bb, yb_o))
    chains = []
    for bi, (d, r_ref, v_ref, kk_ref, lw_ref, kd_ref, b_ref, y_o) in (
            (bi, dr) for bi in range(n_batch) for dr in dirs):
        fwd = d == 0
        before = (t_col < t_row) if fwd else (t_col > t_row)
        strict = same_head & before
        incl = same_head & (before | (t_col == t_row))
        cum = ((tc <= tr) if fwd else (tc >= tr)).astype(BF16)

        lw = lw_ref[bi]
        c_in = sum(_dot(cum, piece) for piece in _split_bf16(lw, 3))
        c_ex = c_in - lw
        c_tot = jnp.sum(lw, axis=0, keepdims=True)
        e_neg = jnp.exp(-c_in)
        e_end = jnp.exp(c_tot - c_in)
        gam = jnp.exp(c_tot)
        kd = kd_ref[bi]
        bv = b_ref[bi]
        a_t = -kk_ref[bi] * jnp.exp(c_ex)
        r_t = r_ref[bi] * jnp.exp(c_in)
        b_t = bv * e_neg
        k_t = kd * e_neg
        b_h = bv * e_end
        k_h = kd * e_end
        vv = v_ref[bi]
        for pr in range(n_pairs):
            sl = slice(pr * PAIR, (pr + 1) * PAIR)
            r_st = stack(r_t[:, sl])
            chains.append(dict(
                bi=bi, d=d, pr=pr, sl=sl, y_o=y_o, strict=strict, incl=incl, r_st=r_st, gam=gam[:, sl],
                a_sb=stack(a_t[:, sl]).astype(BF16), r_sb=r_st.astype(BF16), v_sb=stack(vv[:, sl]).astype(BF16),
                bt=b_t[:, sl].astype(BF16), kt=k_t[:, sl].astype(BF16),
                b_sb=stack(b_h[:, sl]).astype(BF16), k_sb=stack(k_h[:, sl]).astype(BF16)))

    h2 = 2 * cs
    for c in chains:
        lhs = jnp.concatenate([c["a_sb"], c["r_sb"]], axis=0)
        rhs = jnp.concatenate([c["bt"], c["bt"], c["kt"], c["kt"]], axis=0)
        gram = _dot_nt(lhs, rhs)
        a_ab = jnp.where(c["strict"], gram[:h2, :h2], 0.0)
        c["a_ak"] = jnp.where(c["strict"], gram[:h2, h2:], 0.0).astype(BF16)
        c["a_rb"] = jnp.where(c["incl"], gram[h2:, :h2], 0.0).astype(BF16)
        c["a_rk"] = jnp.where(c["incl"], gram[h2:, h2:], 0.0).astype(BF16)
        c["xb"] = a_ab.astype(BF16)
        c["tinv"] = eye + a_ab

    for _ in range(cs_shift - 1):
        for c in chains:
            c["xb"] = _dot(c["xb"], c["xb"]).astype(BF16)
        for c in chains:
            c["tinv"] = c["tinv"] + _dot(c["tinv"].astype(BF16), c["xb"])

    zero_sb = jnp.zeros((h2, PAIR), BF16)
    for c in chains:
        c["w1"] = _dot(c["a_ak"], c["v_sb"]).astype(BF16)
        c["av"] = _dot(c["a_rk"], c["v_sb"])
        c["h0"] = h_scr[c["bi"], c["d"], c["pr"]]
        c["h_b"] = c["h0"].astype(BF16)
        c["gam_col"] = jnp.sum(eye * c["gam"], axis=1, keepdims=True)
    for c in chains:
        c["pq"] = _dot(c["tinv"].astype(BF16), jnp.concatenate([c["a_sb"], c["w1"]], axis=1)).astype(BF16)
    for c in chains:
        ry = _dot(c["a_rb"], c["pq"])
        c["r_hat"] = unstack(c["r_st"] + ry[:, :PAIR]).astype(BF16)
        c["y_hat"] = unstack(ry[:, PAIR:] + c["av"])
        lhs = jnp.concatenate([c["b_sb"], c["k_sb"]], axis=0)
        rhs = jnp.concatenate([c["pq"], jnp.concatenate([zero_sb, c["v_sb"]], axis=1)], axis=0)
        gd = _dot_tn(lhs, rhs)
        c["btp"] = gd[:, :PAIR].astype(BF16)
        c["dd"] = gd[:, PAIR:]
    for c in chains:
        c["y_o"][c["bi"], :, c["sl"]] = _dot(c["r_hat"], c["h_b"]) + c["y_hat"]
        h_scr[c["bi"], c["d"], c["pr"]] = c["gam_col"] * c["h0"] + _dot(c["btp"], c["h_b"]) + c["dd"]


def _rwkv_scan(r, v, kk, lw0, lw1, kd0, kd1, b0, b1):
    b, t, c = r.shape
    nc = t // CHUNK
    fw = lambda j: (0, j, 0)
    bw = lambda j: (0, nc - 1 - j, 0)
    blk = (b, CHUNK, c)
    out = jax.ShapeDtypeStruct((b, t, c), F32)
    return pl.pallas_call(
        _scan_body,
        grid=(nc,),
        in_specs=[pl.BlockSpec(blk, fw)] * 6 + [pl.BlockSpec(blk, bw)] * 6,
        out_specs=[pl.BlockSpec(blk, fw), pl.BlockSpec(blk, bw)],
        out_shape=[out, out],
        scratch_shapes=[pltpu.VMEM((b, 2, c // PAIR, PAIR, PAIR), F32)],
        compiler_params=pltpu.CompilerParams(dimension_semantics=("arbitrary",),
                                             vmem_limit_bytes=VMEM_LIMIT),
        name="rwkv_scan",
    )(r, v, kk, lw0, kd0, b0, r, v, kk, lw1, kd1, b1)


def _attn_body(q_ref, kvp_ref, kvc_ref, kvn_ref, qg_ref, kg_ref, bias_ref, sink_ref, o_ref):
    n = pl.program_id(1)
    nb = pl.num_programs(1)
    blk = q_ref.shape[1]
    head_sum = _head_sum_matrix(LANES)
    head_mean = lambda z: _dot_exact_rhs(z, head_sum) * (1.0 / HEAD_DIM)

    k_win = jnp.concatenate([kvp_ref[0, :, :KV_COLS], kvc_ref[0, :, :KV_COLS], kvn_ref[0, :, :KV_COLS]], axis=0)
    v_win = jnp.concatenate([kvp_ref[0, :, KV_COLS:], kvc_ref[0, :, KV_COLS:], kvn_ref[0, :, KV_COLS:]], axis=0)
    kn = k_win * lax.rsqrt(head_mean(k_win * k_win) + NORM_EPS) * kg_ref[...]
    v_b = v_win.astype(BF16)

    lane = lax.broadcasted_iota(jnp.int32, kn.shape, 1)
    k_at = []
    for g in range(KV_HEADS):
        own = jnp.where((lane >> HEAD_SHIFT) == g, kn, 0.0)
        other = pltpu.roll(own, HEAD_DIM, 1)
        k_at.append([own if p == g else other for p in range(2)])

    row = lax.broadcasted_iota(jnp.int32, (blk, 3 * blk), 0)
    col = lax.broadcasted_iota(jnp.int32, (blk, 3 * blk), 1)
    rel = col - blk - row
    valid = (jnp.abs(rel) <= WINDOW)
    valid &= (col >= blk) | (n > 0)
    valid &= (col < 2 * blk) | (n < nb - 1)
    out_lane_even = lax.broadcasted_iota(jnp.int32, (blk, LANES), 1) < HEAD_DIM

    group = Q_HEADS // KV_HEADS
    n_slabs = ATTN_WIDTH // LANES
    slabs_per_group = n_slabs // KV_HEADS
    qn = []
    for s in range(n_slabs):
        q = q_ref[0, :, s * LANES:(s + 1) * LANES]
        qn.append((q * lax.rsqrt(head_mean(q * q) + NORM_EPS) * qg_ref[...] * (HEAD_DIM ** -0.5)).astype(BF16))
    scores = []
    for g in range(KV_HEADS):
        kcat = jnp.concatenate([k_at[g][0], k_at[g][1]], axis=0).astype(BF16)
        q_g = jnp.concatenate(qn[g * slabs_per_group:(g + 1) * slabs_per_group], axis=0)
        scores.append(_dot_nt(q_g, kcat))
    heads = range(Q_HEADS)
    sc = []
    for h in heads:
        s, p = h // 2, h % 2
        g, sg = s // slabs_per_group, s % slabs_per_group
        z = scores[g][sg * blk:(sg + 1) * blk, p * 3 * blk:(p + 1) * 3 * blk] + bias_ref[h]
        sc.append(jnp.where(valid, z, -jnp.inf))
    sinks = [sink_ref[h] for h in heads]
    m = [jnp.maximum(jnp.max(sc[h], axis=-1, keepdims=True), sinks[h]) for h in heads]
    e = [jnp.exp(sc[h] - m[h]) for h in heads]
    denom = [jnp.sum(e[h], axis=-1, keepdims=True) + jnp.exp(sinks[h] - m[h]) for h in heads]
    probs = [(e[h] / denom[h]).astype(BF16) for h in heads]
    o_all = _dot(jnp.concatenate(probs, axis=0), v_b)
    for s in range(n_slabs):
        g = (2 * s) // group
        halves = []
        for p in range(2):
            h = 2 * s + p
            o = o_all[h * blk:(h + 1) * blk]
            halves.append(o if p == g else pltpu.roll(o, HEAD_DIM, 1))
        o_ref[0, :, s * LANES:(s + 1) * LANES] = jnp.where(out_lane_even, halves[0], halves[1])


def _t5_bucket_table():
    nb = REL_BUCKETS // 2
    max_exact = nb // 2
    rel = (jnp.arange(3 * BLOCK)[None, :] - BLOCK) - jnp.arange(BLOCK)[:, None]
    ret = jnp.where(rel > 0, nb, 0)
    n = jnp.abs(rel)
    large = max_exact + (jnp.log(jnp.maximum(n, 1).astype(F32) / max_exact)
                         / math.log(REL_MAX_DIST / max_exact) * (nb - max_exact)).astype(jnp.int32)
    large = jnp.minimum(large, nb - 1)
    return ret + jnp.where(n < max_exact, n, large)


def _bias_body(bucket_ref, rb_ref, o_ref):
    h = pl.program_id(0)
    bucket = bucket_ref[...]
    acc = jnp.zeros(bucket.shape, F32)
    for bkt in range(REL_BUCKETS):
        acc = jnp.where(bucket == bkt, rb_ref[bkt, h], acc)
    o_ref[0] = acc


def _bias_table(rel_bias):
    shape = (BLOCK, 3 * BLOCK)
    return pl.pallas_call(
        _bias_body,
        grid=(Q_HEADS,),
        in_specs=[pl.BlockSpec(shape, lambda h: (0, 0)), pl.BlockSpec(memory_space=pltpu.SMEM)],
        out_specs=pl.BlockSpec((1,) + shape, lambda h: (h, 0, 0)),
        out_shape=jax.ShapeDtypeStruct((Q_HEADS,) + shape, F32),
        name="bias_table",
    )(_t5_bucket_table().astype(jnp.int32), rel_bias.astype(F32))


def _attention(q, kv, q_gain, k_gain, rel_bias, sink):
    b, t, _ = q.shape
    nb = t // BLOCK
    bias = _bias_table(rel_bias)
    cur = lambda bi, n: (bi, n, 0)
    prv = lambda bi, n: (bi, jnp.maximum(n - 1, 0), 0)
    nxt = lambda bi, n: (bi, jnp.minimum(n + 1, nb - 1), 0)
    c2 = lambda bi, n: (0, 0)
    c3 = lambda bi, n: (0, 0, 0)
    kvblk = (1, BLOCK, 2 * KV_COLS)
    return pl.pallas_call(
        _attn_body,
        grid=(b, nb),
        in_specs=[pl.BlockSpec((1, BLOCK, ATTN_WIDTH), cur), pl.BlockSpec(kvblk, prv), pl.BlockSpec(kvblk, cur),
                  pl.BlockSpec(kvblk, nxt), pl.BlockSpec((1, LANES), c2), pl.BlockSpec((1, LANES), c2),
                  pl.BlockSpec((Q_HEADS, BLOCK, 3 * BLOCK), c3), pl.BlockSpec(memory_space=pltpu.SMEM)],
        out_specs=pl.BlockSpec((1, BLOCK, ATTN_WIDTH), cur),
        out_shape=jax.ShapeDtypeStruct((b, t, ATTN_WIDTH), F32),
        compiler_params=pltpu.CompilerParams(dimension_semantics=("parallel", "parallel"),
                                             vmem_limit_bytes=VMEM_LIMIT),
        name="band_attn",
    )(q, kv, kv, kv, jnp.tile(q_gain, 2).reshape(1, LANES), jnp.tile(k_gain, 2).reshape(1, LANES),
      bias, sink.astype(F32))


def _outproj_body(x_ref, yf_ref, yb_ref, bonus_ref, gate_ref, attn_ref, lw_ref, lb_ref, wo_ref, gf_ref,
                  x1_ref, h2_ref):
    c = RWKV_WIDTH
    head_sum = _head_sum_matrix(c)
    y = yf_ref[...] + yb_ref[...]
    mu = _dot_exact_rhs(y, head_sum) * (1.0 / HEAD_DIM)
    yc = y - mu
    var = _dot_exact_rhs(yc * yc, head_sum) * (1.0 / HEAD_DIM)
    yn = yc * lax.rsqrt(var + LNX_EPS) * lw_ref[...] + lb_ref[...]
    mix_r = (yn + bonus_ref[...]) * gate_ref[...]
    x1 = (x_ref[...] + _dot(mix_r.astype(BF16), wo_ref[:c, :])
          + _dot(attn_ref[...].astype(BF16), wo_ref[c:, :]))
    x1_ref[...] = x1
    h2 = x1 * lax.rsqrt(jnp.mean(x1 * x1, axis=-1, keepdims=True) + NORM_EPS) * gf_ref[...]
    h2_ref[...] = h2.astype(BF16)


def _outproj(x2, yf, yb, bonus, gate, attn, lnx_w, lnx_b, w_out, g_ffn, tile):
    rows, d = x2.shape
    c = RWKV_WIDTH
    row = lambda i: (i, 0)
    const = lambda i: (0, 0)
    return pl.pallas_call(
        _outproj_body,
        grid=(rows // tile,),
        in_specs=[pl.BlockSpec((tile, d), row)] + [pl.BlockSpec((tile, c), row)] * 5
                 + [pl.BlockSpec((1, c), const), pl.BlockSpec((1, c), const),
                    pl.BlockSpec(w_out.shape, const), pl.BlockSpec((1, d), const)],
        out_specs=[pl.BlockSpec((tile, d), row), pl.BlockSpec((tile, d), row)],
        out_shape=[jax.ShapeDtypeStruct((rows, d), F32), jax.ShapeDtypeStruct((rows, d), BF16)],
        compiler_params=pltpu.CompilerParams(dimension_semantics=("parallel",), vmem_limit_bytes=VMEM_LIMIT),
        name="outproj",
    )(x2, yf, yb, bonus, gate, attn, lnx_w.reshape(1, c), lnx_b.reshape(1, c), w_out.astype(BF16),
      g_ffn.reshape(1, d))


def _ffn_body(x1_ref, h_ref, hp_ref, hn_ref, wu_ref, cw_ref, cb_ref, wd_ref, o_ref, act_scr,
              *, tiles_per_seq, ff_chunk):
    i = pl.program_id(0)
    tile = h_ref.shape[0]
    halo = hp_ref.shape[0]
    d_ff = wd_ref.shape[0]
    first = (i % tiles_per_seq) == 0
    last = (i % tiles_per_seq) == tiles_per_seq - 1
    hp = hp_ref[...]
    hn = hn_ref[...]
    hp = jnp.where(first, jnp.zeros_like(hp), hp)
    hn = jnp.where(last, jnp.zeros_like(hn), hn)
    hcat = jnp.concatenate([hp, h_ref[...], hn], axis=0)
    total = tile + 2 * halo

    def conv(cols):
        u = _dot(hcat, wu_ref[:, cols])
        up = pltpu.roll(u, 1, 0)[halo:halo + tile]
        un = pltpu.roll(u, total - 1, 0)[halo:halo + tile]
        return (up * cw_ref[0:1, cols] + u[halo:halo + tile] * cw_ref[1:2, cols] + un * cw_ref[2:3, cols]
                + cb_ref[:, cols])

    for c in range(d_ff // ff_chunk):
        g = conv(slice(c * ff_chunk, (c + 1) * ff_chunk))
        val = conv(slice(d_ff + c * ff_chunk, d_ff + (c + 1) * ff_chunk))
        act_scr[:, c * ff_chunk:(c + 1) * ff_chunk] = (g * _sigmoid(g) * val).astype(BF16)
    o_ref[...] = x1_ref[...] + _dot(act_scr[...], wd_ref[...])


def _ffn(x1, h2, w_up, conv_w, conv_b, w_down, seq, tile, ff_chunk):
    rows, d = x1.shape
    d_ff = w_down.shape[0]
    halo = SUBLANES_BF16
    per = tile // halo
    row = lambda i: (i, 0)
    prv = lambda i: (jnp.maximum(i * per - 1, 0), 0)
    nxt = lambda i: (jnp.minimum((i + 1) * per, rows // halo - 1), 0)
    const = lambda i: (0, 0)
    resident = dict(pipeline_mode=pl.Buffered(1))
    body = functools.partial(_ffn_body, tiles_per_seq=seq // tile, ff_chunk=ff_chunk)
    return pl.pallas_call(
        body,
        grid=(rows // tile,),
        in_specs=[pl.BlockSpec((tile, d), row), pl.BlockSpec((tile, d), row),
                  pl.BlockSpec((halo, d), prv), pl.BlockSpec((halo, d), nxt),
                  pl.BlockSpec((d, 2 * d_ff), const, **resident),
                  pl.BlockSpec((3, 2 * d_ff), const), pl.BlockSpec((1, 2 * d_ff), const),
                  pl.BlockSpec((d_ff, d), const, **resident)],
        out_specs=pl.BlockSpec((tile, d), row),
        out_shape=jax.ShapeDtypeStruct((rows, d), F32),
        scratch_shapes=[pltpu.VMEM((tile, d_ff), BF16)],
        compiler_params=pltpu.CompilerParams(dimension_semantics=("parallel",), vmem_limit_bytes=VMEM_LIMIT),
        name="convglu",
    )(x1, h2, h2, h2, w_up.astype(BF16), conv_w, conv_b.reshape(1, 2 * d_ff), w_down.astype(BF16))


def _layer(x, g_mix, w_in, mu_prev, mu_next, w0, w2, a0, a2, g2, k_k, k_a, r_k, lnx_w, lnx_b,
           q_gain, k_gain, rel_bias, sink, w_out, g_ffn, w_up, conv_w, conv_b, w_down):
    b, t, d = x.shape
    rows = b * t
    x2 = x.reshape(rows, d)
    p, q, kv = _inproj(x2, g_mix, w_in, tile=min(512, t))
    ops = _rwkv_prep(p.reshape(b, t, RWKV_COLS), mu_prev, mu_next, w0, w2, a0, a2, g2, k_k, k_a,
                     r_k.reshape(-1), tile=min(256, t))
    r, v, kk, lw0, lw1, kd0, kd1, b0, b1, gate, bonus = ops
    yf, yb = _rwkv_scan(r, v, kk, lw0, lw1, kd0, kd1, b0, b1)
    attn = _attention(q.reshape(b, t, ATTN_WIDTH), kv.reshape(b, t, 2 * KV_COLS), q_gain, k_gain, rel_bias, sink)
    flat = lambda z: z.reshape(rows, z.shape[-1])
    x1, h2 = _outproj(x2, flat(yf), flat(yb), flat(bonus), flat(gate), flat(attn), lnx_w, lnx_b, w_out, g_ffn,
                      tile=min(512, t))
    out = _ffn(x1, h2, w_up, conv_w, conv_b, w_down, seq=t, tile=min(512, t), ff_chunk=256)
    return out.reshape(b, t, d)


def kernel(x, g_mix, w_in, mu_prev, mu_next, w0, w2, a0, a2, g2, k_k, k_a, r_k, lnx_w, lnx_b, q_gain, k_gain,
           rel_bias, sink, w_out, g_ffn, w_up, conv_w, conv_b, w_down):
    depth = g_mix.shape[0]
    for l in range(depth):
        x = _layer(x, g_mix[l], w_in[l], mu_prev[l], mu_next[l], w0[l], w2[l], a0[l], a2[l], g2[l], k_k[l], k_a[l],
                   r_k[l], lnx_w[l], lnx_b[l], q_gain[l], k_gain[l], rel_bias, sink[l], w_out[l], g_ffn[l],
                   w_up[l], conv_w[l], conv_b[l], w_down[l])
    return x
```

```python
import functools
import math

import jax
import jax.numpy as jnp
from jax import lax
from jax.experimental import pallas as pl
from jax.experimental.pallas import tpu as pltpu

F32 = jnp.float32
BF16 = jnp.bfloat16

HEAD_DIM = 64
RWKV_WIDTH = 512
ATTN_WIDTH = 512
KV_HEADS = 2
Q_HEADS = 8
DECAY_LORA = 64
ICLR_LORA = 64
GATE_LORA = 128
RWKV_COLS = 3 * RWKV_WIDTH + DECAY_LORA + ICLR_LORA + GATE_LORA
KV_COLS = KV_HEADS * HEAD_DIM
WINDOW = 128
BLOCK = 128
REL_BUCKETS = 32
REL_MAX_DIST = 128
NORM_EPS = 1e-6
LNX_EPS = 64e-5
KK_EPS = 1e-12

LANES = 128
SUBLANES_F32 = 8
SUBLANES_BF16 = 16
VMEM_LIMIT = 48 * 1024 * 1024

CHUNK = 64
PAIR = 2 * HEAD_DIM
assert PAIR == LANES
HEAD_SHIFT = HEAD_DIM.bit_length() - 1
assert 1 << HEAD_SHIFT == HEAD_DIM


def _dot(a, b, precision=None):
    return jnp.dot(a, b, preferred_element_type=F32, precision=precision)


def _dot_nt(a, b, precision=None):
    return lax.dot_general(a, b, (((1,), (1,)), ((), ())), preferred_element_type=F32, precision=precision)


def _dot_tn(a, b, precision=None):
    return lax.dot_general(a, b, (((0,), (0,)), ((), ())), preferred_element_type=F32, precision=precision)


def _sigmoid(x):
    return 1.0 / (1.0 + jnp.exp(-x))


def _split_bf16(x, parts):
    out = []
    for _ in range(parts):
        h = x.astype(BF16)
        out.append(h)
        x = x - h.astype(F32)
    return out


def _dot_exact_rhs(x, m, parts=2):
    acc = None
    for h in _split_bf16(x, parts):
        t = _dot(h, m)
        acc = t if acc is None else acc + t
    return acc


def _head_sum_matrix(width):
    r = lax.broadcasted_iota(jnp.int32, (width, width), 0) >> HEAD_SHIFT
    c = lax.broadcasted_iota(jnp.int32, (width, width), 1) >> HEAD_SHIFT
    return (r == c).astype(BF16)


def _inproj_body(x_ref, g_ref, wr_ref, wq_ref, wkv_ref, p_ref, q_ref, kv_ref):
    x = x_ref[...]
    h = x * lax.rsqrt(jnp.mean(x * x, axis=-1, keepdims=True) + NORM_EPS) * g_ref[...]
    hb = h.astype(BF16)
    p_ref[...] = _dot(hb, wr_ref[...])
    q_ref[...] = _dot(hb, wq_ref[...])
    kv_ref[...] = _dot(hb, wkv_ref[...])


def _inproj(x2, g_mix, w_in, tile):
    rows, d = x2.shape
    wb = w_in.astype(BF16)
    wr, wq, wkv = wb[:, :RWKV_COLS], wb[:, RWKV_COLS:RWKV_COLS + ATTN_WIDTH], wb[:, RWKV_COLS + ATTN_WIDTH:]
    const = lambda i: (0, 0)
    row = lambda i: (i, 0)
    return pl.pallas_call(
        _inproj_body,
        grid=(rows // tile,),
        in_specs=[pl.BlockSpec((tile, d), row), pl.BlockSpec((1, d), const),
                  pl.BlockSpec(wr.shape, const), pl.BlockSpec(wq.shape, const), pl.BlockSpec(wkv.shape, const)],
        out_specs=[pl.BlockSpec((tile, RWKV_COLS), row), pl.BlockSpec((tile, ATTN_WIDTH), row),
                   pl.BlockSpec((tile, 2 * KV_COLS), row)],
        out_shape=[jax.ShapeDtypeStruct((rows, RWKV_COLS), F32), jax.ShapeDtypeStruct((rows, ATTN_WIDTH), F32),
                   jax.ShapeDtypeStruct((rows, 2 * KV_COLS), F32)],
        compiler_params=pltpu.CompilerParams(dimension_semantics=("parallel",), vmem_limit_bytes=VMEM_LIMIT),
        name="inproj",
    )(x2, g_mix.reshape(1, d), wr, wq, wkv)


def _prep_body(p_ref, pp_ref, pn_ref, mup_ref, mun_ref, w0_ref, w2_ref, a0_ref, a2_ref, g2_ref,
               kk_ref, ka_ref, rk_ref,
               r_o, v_o, kk_o, lw0_o, lw1_o, kd0_o, kd1_o, b0_o, b1_o, gate_o, bonus_o):
    i = pl.program_id(1)
    last = pl.num_programs(1) - 1
    p = p_ref[0]
    tt = p.shape[0]
    prev_row = jnp.where(i == 0, 0.0, pp_ref[0, SUBLANES_F32 - 1:SUBLANES_F32, :])
    next_row = jnp.where(i == last, 0.0, pn_ref[0, 0:1, :])
    row = lax.broadcasted_iota(jnp.int32, p.shape, 0)
    prev = jnp.where(row == 0, prev_row, pltpu.roll(p, 1, 0))
    nxt = jnp.where(row == tt - 1, next_row, pltpu.roll(p, tt - 1, 0))
    pf = p + mup_ref[...] * (prev - p) + mun_ref[...] * (nxt - p)

    c = RWKV_WIDTH
    r, k, v = pf[:, :c], pf[:, c:2 * c], pf[:, 2 * c:3 * c]
    lora = pf[:, 3 * c:3 * c + LANES]
    xg = pf[:, 3 * c + LANES:]
    lora_t = jnp.tanh(lora).astype(BF16)
    lora_b = lora.astype(BF16)

    head_sum = _head_sum_matrix(c)
    kx = k * kk_ref[...]
    kk = kx * lax.rsqrt(_dot_exact_rhs(kx * kx, head_sum) + KK_EPS)
    gate = _dot(_sigmoid(xg).astype(BF16), g2_ref[...])

    kds = []
    for d, (lw_o, kd_o, b_o) in enumerate(((lw0_o, kd0_o, b0_o), (lw1_o, kd1_o, b1_o))):
        w_raw = w0_ref[d:d + 1, :] + _dot(lora_t, w2_ref[d])
        lw_o[0] = (-math.exp(-0.5)) * _sigmoid(w_raw)
        iclr = _sigmoid(a0_ref[d:d + 1, :] + _dot(lora_b, a2_ref[d]))
        kd = k * (1.0 + (iclr - 1.0) * ka_ref[...])
        kd_o[0] = kd
        b_o[0] = iclr * kk
        kds.append(kd)

    r_o[0] = r
    v_o[0] = v
    kk_o[0] = kk
    gate_o[0] = gate
    bonus_o[0] = _dot_exact_rhs(r * (kds[0] + kds[1]) * rk_ref[...], head_sum) * v


def _rwkv_prep(p3, mu_prev, mu_next, w0, w2, a0, a2, g2, k_k, k_a, r_k, tile):
    b, t, cols = p3.shape
    c = RWKV_WIDTH
    nt = t // tile
    per8 = tile // SUBLANES_F32
    zeros = jnp.zeros((2, DECAY_LORA, c), F32)
    w2p = jnp.concatenate([w2, zeros], axis=1).astype(BF16)
    a2p = jnp.concatenate([zeros, a2], axis=1).astype(BF16)
    tok = lambda bi, i: (bi, i, 0)
    prev8 = lambda bi, i: (bi, jnp.maximum(i * per8 - 1, 0), 0)
    next8 = lambda bi, i: (bi, jnp.minimum((i + 1) * per8, t // SUBLANES_F32 - 1), 0)
    c2 = lambda bi, i: (0, 0)
    c3 = lambda bi, i: (0, 0, 0)
    out = jax.ShapeDtypeStruct((b, t, c), F32)
    return pl.pallas_call(
        _prep_body,
        grid=(b, nt),
        in_specs=[pl.BlockSpec((1, tile, cols), tok), pl.BlockSpec((1, SUBLANES_F32, cols), prev8),
                  pl.BlockSpec((1, SUBLANES_F32, cols), next8),
                  pl.BlockSpec((1, cols), c2), pl.BlockSpec((1, cols), c2),
                  pl.BlockSpec((2, c), c2), pl.BlockSpec((2, LANES, c), c3),
                  pl.BlockSpec((2, c), c2), pl.BlockSpec((2, LANES, c), c3),
                  pl.BlockSpec((GATE_LORA, c), c2),
                  pl.BlockSpec((1, c), c2), pl.BlockSpec((1, c), c2), pl.BlockSpec((1, c), c2)],
        out_specs=[pl.BlockSpec((1, tile, c), tok)] * 11,
        out_shape=[out] * 11,
        compiler_params=pltpu.CompilerParams(dimension_semantics=("parallel", "parallel"),
                                             vmem_limit_bytes=VMEM_LIMIT),
        name="rwkv_prep",
    )(p3, p3, p3, mu_prev.reshape(1, cols), mu_next.reshape(1, cols), w0, w2p, a0, a2p, g2.astype(BF16),
      k_k.reshape(1, c), k_a.reshape(1, c), r_k.reshape(1, c))


def _scan_body(rf, vf, kkf, lwf, kdf, bf, rb, vb, kkb, lwb, kdb, bb, yf_o, yb_o, h_scr):
    j = pl.program_id(0)

    @pl.when(j == 0)
    def _():
        h_scr[...] = jnp.zeros_like(h_scr)

    n_batch = rf.shape[0]
    cs = rf.shape[1]
    n_pairs = rf.shape[2] // PAIR

    cs_shift = cs.bit_length() - 1
    assert 1 << cs_shift == cs
    t_row = lax.broadcasted_iota(jnp.int32, (cs, 2 * cs), 0)
    t_col = lax.broadcasted_iota(jnp.int32, (cs, 2 * cs), 1) & (cs - 1)
    eye_cat = (t_row == t_col).astype(F32)
    ri = lax.broadcasted_iota(jnp.int32, (PAIR, PAIR), 0)
    ci = lax.broadcasted_iota(jnp.int32, (PAIR, PAIR), 1)
    eye = (ri == ci).astype(F32)
    same_head = (ri >> HEAD_SHIFT) == (ci >> HEAD_SHIFT)
    tr = lax.broadcasted_iota(jnp.int32, (cs, cs), 0)
    tc = lax.broadcasted_iota(jnp.int32, (cs, cs), 1)
    lane_even = lax.broadcasted_iota(jnp.int32, (cs, PAIR), 1) < HEAD_DIM
    col_first = lax.broadcasted_iota(jnp.int32, (cs, 2 * cs), 1) < cs

    def stack(x):
        zero = jnp.zeros_like(x)
        return jnp.concatenate([jnp.where(lane_even, x, zero), jnp.where(lane_even, zero, x)], axis=0)

    def block_diag(m):
        zero = jnp.zeros_like(m)
        return jnp.concatenate([jnp.where(col_first, m, zero), jnp.where(col_first, zero, m)], axis=0)

    dirs = ((0, rf, vf, kkf, lwf, kdf, bf, yf_o), (1, rb, vb, kkb, lwb, kdb, bb, yb_o))
    chains = []
    for bi, (d, r_ref, v_ref, kk_ref, lw_ref, kd_ref, b_ref, y_o) in (
            (bi, dr) for bi in range(n_batch) for dr in dirs):
        fwd = d == 0
        strict = (t_col < t_row) if fwd else (t_col > t_row)
        incl = strict | (t_col == t_row)
        cum = ((tc <= tr) if fwd else (tc >= tr)).astype(BF16)

        lw = lw_ref[bi]
        c_in = sum(_dot(cum, piece) for piece in _split_bf16(lw, 3))
        c_ex = c_in - lw
        c_tot = jnp.sum(lw, axis=0, keepdims=True)
        e_neg = jnp.exp(-c_in)
        e_end = jnp.exp(c_tot - c_in)
        gam = jnp.exp(c_tot)
        kd = kd_ref[bi]
        bv = b_ref[bi]
        a_t = -kk_ref[bi] * jnp.exp(c_ex)
        r_t = r_ref[bi] * jnp.exp(c_in)
        b_t = bv * e_neg
        k_t = kd * e_neg
        b_h = bv * e_end
        k_h = kd * e_end
        vv = v_ref[bi]
        for pr in range(n_pairs):
            sl = slice(pr * PAIR, (pr + 1) * PAIR)
            a_b, v_b = a_t[:, sl].astype(BF16), vv[:, sl].astype(BF16)
            chains.append(dict(
                bi=bi, d=d, pr=pr, sl=sl, y_o=y_o, strict=strict, incl=incl, r_t=r_t[:, sl], gam=gam[:, sl],
                a_b=a_b, r_b=r_t[:, sl].astype(BF16), a_sb=stack(a_b), v_b=v_b, v_sb=stack(v_b),
                bt_sb=stack(b_t[:, sl].astype(BF16)), kt_sb=stack(k_t[:, sl].astype(BF16)),
                bk_h=jnp.concatenate([b_h[:, sl].astype(BF16), k_h[:, sl].astype(BF16)], axis=0)))

    h2 = 2 * cs
    for c in chains:
        lhs = jnp.concatenate([c["a_b"], c["r_b"]], axis=0)
        rhs = jnp.concatenate([c["bt_sb"], c["kt_sb"]], axis=0)
        gram = _dot_nt(lhs, rhs)
        a_ab = jnp.where(c["strict"], gram[:cs, :h2], 0.0)
        c["a_xk"] = jnp.concatenate([jnp.where(c["strict"], gram[:cs, h2:], 0.0),
                                     jnp.where(c["incl"], gram[cs:, h2:], 0.0)], axis=0).astype(BF16)
        c["a_rb"] = jnp.where(c["incl"], gram[cs:, :h2], 0.0).astype(BF16)
        c["x_cat"] = a_ab.astype(BF16)
        c["s_cat"] = eye_cat + a_ab

    for c in chains:
        c["x_cat"] = _dot(c["x_cat"], block_diag(c["x_cat"])).astype(BF16)
    for _ in range(cs_shift - 2):
        for c in chains:
            sx = _dot(jnp.concatenate([c["s_cat"].astype(BF16), c["x_cat"]], axis=0), block_diag(c["x_cat"]))
            c["s_cat"] = c["s_cat"] + sx[:cs]
            c["x_cat"] = sx[cs:].astype(BF16)
    for c in chains:
        c["s_cat"] = (c["s_cat"] + _dot(c["s_cat"].astype(BF16), block_diag(c["x_cat"]))).astype(BF16)

    zero_b = jnp.zeros((cs, PAIR), BF16)
    for c in chains:
        av = _dot(c["a_xk"], c["v_sb"])
        c["w1_sb"] = stack(av[:cs].astype(BF16))
        c["av"] = av[cs:]
        c["h0"] = h_scr[c["bi"], c["d"], c["pr"]]
        c["h_b"] = c["h0"].astype(BF16)
        c["gam_col"] = jnp.sum(eye * c["gam"], axis=1, keepdims=True)
    for c in chains:
        c["pq"] = _dot(c["s_cat"], jnp.concatenate([c["a_sb"], c["w1_sb"]], axis=1)).astype(BF16)
    for c in chains:
        pq = c["pq"]
        ry = _dot(c["a_rb"], jnp.concatenate([stack(pq[:, :PAIR]), stack(pq[:, PAIR:])], axis=1))
        c["r_hat"] = (c["r_t"] + ry[:, :PAIR]).astype(BF16)
        c["y_hat"] = ry[:, PAIR:] + c["av"]
        rhs = jnp.concatenate([pq, jnp.concatenate([zero_b, c["v_b"]], axis=1)], axis=0)
        gd = _dot_tn(c["bk_h"], rhs)
        c["btp"] = jnp.where(same_head, gd[:, :PAIR], 0.0).astype(BF16)
        c["dd"] = jnp.where(same_head, gd[:, PAIR:], 0.0)
    for c in chains:
        yh = _dot(jnp.concatenate([c["r_hat"], c["btp"]], axis=0), c["h_b"])
        c["y_o"][c["bi"], :, c["sl"]] = yh[:cs] + c["y_hat"]
        h_scr[c["bi"], c["d"], c["pr"]] = c["gam_col"] * c["h0"] + yh[cs:] + c["dd"]


def _rwkv_scan(r, v, kk, lw0, lw1, kd0, kd1, b0, b1):
    b, t, c = r.shape
    nc = t // CHUNK
    fw = lambda j: (0, j, 0)
    bw = lambda j: (0, nc - 1 - j, 0)
    blk = (b, CHUNK, c)
    out = jax.ShapeDtypeStruct((b, t, c), F32)
    return pl.pallas_call(
        _scan_body,
        grid=(nc,),
        in_specs=[pl.BlockSpec(blk, fw)] * 6 + [pl.BlockSpec(blk, bw)] * 6,
        out_specs=[pl.BlockSpec(blk, fw), pl.BlockSpec(blk, bw)],
        out_shape=[out, out],
        scratch_shapes=[pltpu.VMEM((b, 2, c // PAIR, PAIR, PAIR), F32)],
        compiler_params=pltpu.CompilerParams(dimension_semantics=("arbitrary",),
                                             vmem_limit_bytes=VMEM_LIMIT),
        name="rwkv_scan",
    )(r, v, kk, lw0, kd0, b0, r, v, kk, lw1, kd1, b1)


def _attn_body(q_ref, kvp_ref, kvc_ref, kvn_ref, qg_ref, kg_ref, bias_ref, sink_ref, o_ref):
    n = pl.program_id(1)
    nb = pl.num_programs(1)
    blk = q_ref.shape[1]
    head_sum = _head_sum_matrix(LANES)
    head_mean = lambda z: _dot_exact_rhs(z, head_sum) * (1.0 / HEAD_DIM)

    k_win = jnp.concatenate([kvp_ref[0, :, :KV_COLS], kvc_ref[0, :, :KV_COLS], kvn_ref[0, :, :KV_COLS]], axis=0)
    v_win = jnp.concatenate([kvp_ref[0, :, KV_COLS:], kvc_ref[0, :, KV_COLS:], kvn_ref[0, :, KV_COLS:]], axis=0)
    kn = k_win * lax.rsqrt(head_mean(k_win * k_win) + NORM_EPS) * kg_ref[...]
    v_b = v_win.astype(BF16)

    lane = lax.broadcasted_iota(jnp.int32, kn.shape, 1)
    k_at = []
    for g in range(KV_HEADS):
        own = jnp.where((lane >> HEAD_SHIFT) == g, kn, 0.0)
        other = pltpu.roll(own, HEAD_DIM, 1)
        k_at.append([own if p == g else other for p in range(2)])

    row = lax.broadcasted_iota(jnp.int32, (blk, 3 * blk), 0)
    col = lax.broadcasted_iota(jnp.int32, (blk, 3 * blk), 1)
    rel = col - blk - row
    valid = (jnp.abs(rel) <= WINDOW)
    valid &= (col >= blk) | (n > 0)
    valid &= (col < 2 * blk) | (n < nb - 1)
    out_lane_even = lax.broadcasted_iota(jnp.int32, (blk, LANES), 1) < HEAD_DIM

    group = Q_HEADS // KV_HEADS
    n_slabs = ATTN_WIDTH // LANES
    slabs_per_group = n_slabs // KV_HEADS
    qn = []
    for s in range(n_slabs):
        q = q_ref[0, :, s * LANES:(s + 1) * LANES]
        qn.append((q * lax.rsqrt(head_mean(q * q) + NORM_EPS) * qg_ref[...] * (HEAD_DIM ** -0.5)).astype(BF16))
    scores = []
    for g in range(KV_HEADS):
        kcat = jnp.concatenate([k_at[g][0], k_at[g][1]], axis=0).astype(BF16)
        q_g = jnp.concatenate(qn[g * slabs_per_group:(g + 1) * slabs_per_group], axis=0)
        scores.append(_dot_nt(q_g, kcat))
    heads = range(Q_HEADS)
    sc = []
    for h in heads:
        s, p = h // 2, h % 2
        g, sg = s // slabs_per_group, s % slabs_per_group
        z = scores[g][sg * blk:(sg + 1) * blk, p * 3 * blk:(p + 1) * 3 * blk] + bias_ref[h]
        sc.append(jnp.where(valid, z, -jnp.inf))
    sinks = [sink_ref[h] for h in heads]
    m = [jnp.maximum(jnp.max(sc[h], axis=-1, keepdims=True), sinks[h]) for h in heads]
    e = [jnp.exp(sc[h] - m[h]) for h in heads]
    denom = [jnp.sum(e[h], axis=-1, keepdims=True) + jnp.exp(sinks[h] - m[h]) for h in heads]
    probs = [(e[h] / denom[h]).astype(BF16) for h in heads]
    o_all = _dot(jnp.concatenate(probs, axis=0), v_b)
    for s in range(n_slabs):
        g = (2 * s) // group
        halves = []
        for p in range(2):
            h = 2 * s + p
            o = o_all[h * blk:(h + 1) * blk]
            halves.append(o if p == g else pltpu.roll(o, HEAD_DIM, 1))
        o_ref[0, :, s * LANES:(s + 1) * LANES] = jnp.where(out_lane_even, halves[0], halves[1])


def _t5_bucket(rel):
    nb = REL_BUCKETS // 2
    max_exact = nb // 2
    ret = jnp.where(rel > 0, nb, 0)
    n = jnp.abs(rel)
    large = max_exact + (jnp.log(jnp.maximum(n, 1).astype(F32) / max_exact)
                         / math.log(REL_MAX_DIST / max_exact) * (nb - max_exact)).astype(jnp.int32)
    large = jnp.minimum(large, nb - 1)
    return ret + jnp.where(n < max_exact, n, large)


BIAS_SPAN = 4 * BLOCK


def _bias_body(tab_ref, o_ref):
    blk = o_ref.shape[1]
    x = jnp.broadcast_to(tab_ref[0], (blk, BIAS_SPAN))
    shifted = pltpu.roll(x, BIAS_SPAN - (blk - 1), 1, stride=1, stride_axis=0)
    o_ref[0] = shifted[:, :3 * blk]


def _bias_table(rel_bias):
    rel = jnp.arange(BIAS_SPAN) - (2 * BLOCK - 1)
    tab = jnp.transpose(rel_bias[_t5_bucket(rel)].astype(F32))
    shape = (BLOCK, 3 * BLOCK)
    return pl.pallas_call(
        _bias_body,
        grid=(Q_HEADS,),
        in_specs=[pl.BlockSpec((1, 1, BIAS_SPAN), lambda h: (h, 0, 0))],
        out_specs=pl.BlockSpec((1,) + shape, lambda h: (h, 0, 0)),
        out_shape=jax.ShapeDtypeStruct((Q_HEADS,) + shape, F32),
        name="bias_table",
    )(tab.reshape(Q_HEADS, 1, BIAS_SPAN))


def _attention(q, kv, q_gain, k_gain, rel_bias, sink):
    b, t, _ = q.shape
    nb = t // BLOCK
    bias = _bias_table(rel_bias)
    cur = lambda bi, n: (bi, n, 0)
    prv = lambda bi, n: (bi, jnp.maximum(n - 1, 0), 0)
    nxt = lambda bi, n: (bi, jnp.minimum(n + 1, nb - 1), 0)
    c2 = lambda bi, n: (0, 0)
    c3 = lambda bi, n: (0, 0, 0)
    kvblk = (1, BLOCK, 2 * KV_COLS)
    return pl.pallas_call(
        _attn_body,
        grid=(b, nb),
        in_specs=[pl.BlockSpec((1, BLOCK, ATTN_WIDTH), cur), pl.BlockSpec(kvblk, prv), pl.BlockSpec(kvblk, cur),
                  pl.BlockSpec(kvblk, nxt), pl.BlockSpec((1, LANES), c2), pl.BlockSpec((1, LANES), c2),
                  pl.BlockSpec((Q_HEADS, BLOCK, 3 * BLOCK), c3), pl.BlockSpec(memory_space=pltpu.SMEM)],
        out_specs=pl.BlockSpec((1, BLOCK, ATTN_WIDTH), cur),
        out_shape=jax.ShapeDtypeStruct((b, t, ATTN_WIDTH), F32),
        compiler_params=pltpu.CompilerParams(dimension_semantics=("parallel", "parallel"),
                                             vmem_limit_bytes=VMEM_LIMIT),
        name="band_attn",
    )(q, kv, kv, kv, jnp.tile(q_gain, 2).reshape(1, LANES), jnp.tile(k_gain, 2).reshape(1, LANES),
      bias, sink.astype(F32))


def _outproj_body(x_ref, yf_ref, yb_ref, bonus_ref, gate_ref, attn_ref, lw_ref, lb_ref, wo_ref, gf_ref,
                  x1_ref, h2_ref):
    c = RWKV_WIDTH
    head_sum = _head_sum_matrix(c)
    y = yf_ref[...] + yb_ref[...]
    mu = _dot_exact_rhs(y, head_sum) * (1.0 / HEAD_DIM)
    yc = y - mu
    var = _dot_exact_rhs(yc * yc, head_sum) * (1.0 / HEAD_DIM)
    yn = yc * lax.rsqrt(var + LNX_EPS) * lw_ref[...] + lb_ref[...]
    mix_r = (yn + bonus_ref[...]) * gate_ref[...]
    x1 = (x_ref[...] + _dot(mix_r.astype(BF16), wo_ref[:c, :])
          + _dot(attn_ref[...].astype(BF16), wo_ref[c:, :]))
    x1_ref[...] = x1
    h2 = x1 * lax.rsqrt(jnp.mean(x1 * x1, axis=-1, keepdims=True) + NORM_EPS) * gf_ref[...]
    h2_ref[...] = h2.astype(BF16)


def _outproj(x2, yf, yb, bonus, gate, attn, lnx_w, lnx_b, w_out, g_ffn, tile):
    rows, d = x2.shape
    c = RWKV_WIDTH
    row = lambda i: (i, 0)
    const = lambda i: (0, 0)
    return pl.pallas_call(
        _outproj_body,
        grid=(rows // tile,),
        in_specs=[pl.BlockSpec((tile, d), row)] + [pl.BlockSpec((tile, c), row)] * 5
                 + [pl.BlockSpec((1, c), const), pl.BlockSpec((1, c), const),
                    pl.BlockSpec(w_out.shape, const), pl.BlockSpec((1, d), const)],
        out_specs=[pl.BlockSpec((tile, d), row), pl.BlockSpec((tile, d), row)],
        out_shape=[jax.ShapeDtypeStruct((rows, d), F32), jax.ShapeDtypeStruct((rows, d), BF16)],
        compiler_params=pltpu.CompilerParams(dimension_semantics=("parallel",), vmem_limit_bytes=VMEM_LIMIT),
        name="outproj",
    )(x2, yf, yb, bonus, gate, attn, lnx_w.reshape(1, c), lnx_b.reshape(1, c), w_out.astype(BF16),
      g_ffn.reshape(1, d))


def _ffn_body(x1_ref, h_ref, hp_ref, hn_ref, wu_ref, cw_ref, cb_ref, wd_ref, o_ref, act_scr,
              *, tiles_per_seq, ff_chunk):
    i = pl.program_id(0)
    tile = h_ref.shape[0]
    halo = hp_ref.shape[0]
    d_ff = wd_ref.shape[0]
    first = (i % tiles_per_seq) == 0
    last = (i % tiles_per_seq) == tiles_per_seq - 1
    hp = hp_ref[...]
    hn = hn_ref[...]
    hp = jnp.where(first, jnp.zeros_like(hp), hp)
    hn = jnp.where(last, jnp.zeros_like(hn), hn)
    hcat = jnp.concatenate([hp, h_ref[...], hn], axis=0)
    total = tile + 2 * halo

    def conv(cols):
        u = _dot(hcat, wu_ref[:, cols])
        up = pltpu.roll(u, 1, 0)[halo:halo + tile]
        un = pltpu.roll(u, total - 1, 0)[halo:halo + tile]
        return (up * cw_ref[0:1, cols] + u[halo:halo + tile] * cw_ref[1:2, cols] + un * cw_ref[2:3, cols]
                + cb_ref[:, cols])

    for c in range(d_ff // ff_chunk):
        g = conv(slice(c * ff_chunk, (c + 1) * ff_chunk))
        val = conv(slice(d_ff + c * ff_chunk, d_ff + (c + 1) * ff_chunk))
        act_scr[:, c * ff_chunk:(c + 1) * ff_chunk] = (g * _sigmoid(g) * val).astype(BF16)
    o_ref[...] = x1_ref[...] + _dot(act_scr[...], wd_ref[...])


def _ffn(x1, h2, w_up, conv_w, conv_b, w_down, seq, tile, ff_chunk):
    rows, d = x1.shape
    d_ff = w_down.shape[0]
    halo = SUBLANES_BF16
    per = tile // halo
    row = lambda i: (i, 0)
    prv = lambda i: (jnp.maximum(i * per - 1, 0), 0)
    nxt = lambda i: (jnp.minimum((i + 1) * per, rows // halo - 1), 0)
    const = lambda i: (0, 0)
    resident = dict(pipeline_mode=pl.Buffered(1))
    body = functools.partial(_ffn_body, tiles_per_seq=seq // tile, ff_chunk=ff_chunk)
    return pl.pallas_call(
        body,
        grid=(rows // tile,),
        in_specs=[pl.BlockSpec((tile, d), row), pl.BlockSpec((tile, d), row),
                  pl.BlockSpec((halo, d), prv), pl.BlockSpec((halo, d), nxt),
                  pl.BlockSpec((d, 2 * d_ff), const, **resident),
                  pl.BlockSpec((3, 2 * d_ff), const), pl.BlockSpec((1, 2 * d_ff), const),
                  pl.BlockSpec((d_ff, d), const, **resident)],
        out_specs=pl.BlockSpec((tile, d), row),
        out_shape=jax.ShapeDtypeStruct((rows, d), F32),
        scratch_shapes=[pltpu.VMEM((tile, d_ff), BF16)],
        compiler_params=pltpu.CompilerParams(dimension_semantics=("parallel",), vmem_limit_bytes=VMEM_LIMIT),
        name="convglu",
    )(x1, h2, h2, h2, w_up.astype(BF16), conv_w, conv_b.reshape(1, 2 * d_ff), w_down.astype(BF16))


def _layer(x, g_mix, w_in, mu_prev, mu_next, w0, w2, a0, a2, g2, k_k, k_a, r_k, lnx_w, lnx_b,
           q_gain, k_gain, rel_bias, sink, w_out, g_ffn, w_up, conv_w, conv_b, w_down):
    b, t, d = x.shape
    rows = b * t
    x2 = x.reshape(rows, d)
    p, q, kv = _inproj(x2, g_mix, w_in, tile=min(512, t))
    ops = _rwkv_prep(p.reshape(b, t, RWKV_COLS), mu_prev, mu_next, w0, w2, a0, a2, g2, k_k, k_a,
                     r_k.reshape(-1), tile=min(256, t))
    r, v, kk, lw0, lw1, kd0, kd1, b0, b1, gate, bonus = ops
    yf, yb = _rwkv_scan(r, v, kk, lw0, lw1, kd0, kd1, b0, b1)
    attn = _attention(q.reshape(b, t, ATTN_WIDTH), kv.reshape(b, t, 2 * KV_COLS), q_gain, k_gain, rel_bias, sink)
    flat = lambda z: z.reshape(rows, z.shape[-1])
    x1, h2 = _outproj(x2, flat(yf), flat(yb), flat(bonus), flat(gate), flat(attn), lnx_w, lnx_b, w_out, g_ffn,
                      tile=min(512, t))
    out = _ffn(x1, h2, w_up, conv_w, conv_b, w_down, seq=t, tile=min(512, t), ff_chunk=256)
    return out.reshape(b, t, d)


def kernel(x, g_mix, w_in, mu_prev, mu_next, w0, w2, a0, a2, g2, k_k, k_a, r_k, lnx_w, lnx_b, q_gain, k_gain,
           rel_bias, sink, w_out, g_ffn, w_up, conv_w, conv_b, w_down):
    depth = g_mix.shape[0]
    for l in range(depth):
        x = _layer(x, g_mix[l], w_in[l], mu_prev[l], mu_next[l], w0[l], w2[l], a0[l], a2[l], g2[l], k_k[l], k_a[l],
                   r_k[l], lnx_w[l], lnx_b[l], q_gain[l], k_gain[l], rel_bias, sink[l], w_out[l], g_ffn[l],
                   w_up[l], conv_w[l], conv_b[l], w_down[l])
    return x
```

```python
import functools
import math

import jax
import jax.numpy as jnp
from jax import lax
from jax.experimental import pallas as pl
from jax.experimental.pallas import tpu as pltpu

F32 = jnp.float32
BF16 = jnp.bfloat16

HEAD_DIM = 64
RWKV_WIDTH = 512
ATTN_WIDTH = 512
KV_HEADS = 2
Q_HEADS = 8
DECAY_LORA = 64
ICLR_LORA = 64
GATE_LORA = 128
RWKV_COLS = 3 * RWKV_WIDTH + DECAY_LORA + ICLR_LORA + GATE_LORA
KV_COLS = KV_HEADS * HEAD_DIM
WINDOW = 128
BLOCK = 128
REL_BUCKETS = 32
REL_MAX_DIST = 128
NORM_EPS = 1e-6
LNX_EPS = 64e-5
KK_EPS = 1e-12

LANES = 128
SUBLANES_F32 = 8
SUBLANES_BF16 = 16
VMEM_LIMIT = 48 * 1024 * 1024

CHUNK = 64
PAIR = 2 * HEAD_DIM
assert PAIR == LANES
HEAD_SHIFT = HEAD_DIM.bit_length() - 1
assert 1 << HEAD_SHIFT == HEAD_DIM


def _dot(a, b, precision=None):
    return jnp.dot(a, b, preferred_element_type=F32, precision=precision)


def _dot_nt(a, b, precision=None):
    return lax.dot_general(a, b, (((1,), (1,)), ((), ())), preferred_element_type=F32, precision=precision)


def _dot_tn(a, b, precision=None):
    return lax.dot_general(a, b, (((0,), (0,)), ((), ())), preferred_element_type=F32, precision=precision)


def _sigmoid(x):
    return 1.0 / (1.0 + jnp.exp(-x))


def _split_bf16(x, parts):
    out = []
    for _ in range(parts):
        h = x.astype(BF16)
        out.append(h)
        x = x - h.astype(F32)
    return out


def _dot_exact_rhs(x, m, parts=2):
    acc = None
    for h in _split_bf16(x, parts):
        t = _dot(h, m)
        acc = t if acc is None else acc + t
    return acc


def _head_sum_matrix(width):
    r = lax.broadcasted_iota(jnp.int32, (width, width), 0) >> HEAD_SHIFT
    c = lax.broadcasted_iota(jnp.int32, (width, width), 1) >> HEAD_SHIFT
    return (r == c).astype(BF16)


def _inprep_body(x_ref, xp_ref, xn_ref, g_ref, wr_ref, wq_ref, wkv_ref,
                 mup_ref, mun_ref, w0_ref, w2_ref, a0_ref, a2_ref, g2_ref, kk_ref, ka_ref, rk_ref,
                 q_o, kv_o, r_o, v_o, kk_o, lw0_o, lw1_o, kd0_o, kd1_o, b0_o, b1_o, gate_o, bonus_o,
                 *, tiles_per_seq):
    i = pl.program_id(0)
    tile = x_ref.shape[0]
    halo = xp_ref.shape[0]
    first = (i % tiles_per_seq) == 0
    last = (i % tiles_per_seq) == tiles_per_seq - 1

    def norm(x):
        return (x * lax.rsqrt(jnp.mean(x * x, axis=-1, keepdims=True) + NORM_EPS) * g_ref[...]).astype(BF16)

    h = norm(x_ref[...])
    q_o[...] = _dot(h, wq_ref[...])
    kv_o[...] = _dot(h, wkv_ref[...])
    hp = norm(xp_ref[...])
    hn = norm(xn_ref[...])
    hp = jnp.where(first, jnp.zeros_like(hp), hp)
    hn = jnp.where(last, jnp.zeros_like(hn), hn)
    pcat = _dot(jnp.concatenate([hp, h, hn], axis=0), wr_ref[...])
    total = tile + 2 * halo
    p = pcat[halo:halo + tile]
    prev = pltpu.roll(pcat, 1, 0)[halo:halo + tile]
    nxt = pltpu.roll(pcat, total - 1, 0)[halo:halo + tile]
    pf = p + mup_ref[...] * (prev - p) + mun_ref[...] * (nxt - p)

    c = RWKV_WIDTH
    r, k, v = pf[:, :c], pf[:, c:2 * c], pf[:, 2 * c:3 * c]
    lora = pf[:, 3 * c:3 * c + LANES]
    xg = pf[:, 3 * c + LANES:]
    lora_t = jnp.tanh(lora).astype(BF16)
    lora_b = lora.astype(BF16)

    head_sum = _head_sum_matrix(c)
    kx = k * kk_ref[...]
    kk = kx * lax.rsqrt(_dot_exact_rhs(kx * kx, head_sum) + KK_EPS)
    gate = _dot(_sigmoid(xg).astype(BF16), g2_ref[...])

    kds = []
    for d, (lw_o, kd_o, b_o) in enumerate(((lw0_o, kd0_o, b0_o), (lw1_o, kd1_o, b1_o))):
        w_raw = w0_ref[d:d + 1, :] + _dot(lora_t, w2_ref[d])
        lw_o[...] = (-math.exp(-0.5)) * _sigmoid(w_raw)
        iclr = _sigmoid(a0_ref[d:d + 1, :] + _dot(lora_b, a2_ref[d]))
        kd = k * (1.0 + (iclr - 1.0) * ka_ref[...])
        kd_o[...] = kd
        b_o[...] = iclr * kk
        kds.append(kd)

    r_o[...] = r
    v_o[...] = v
    kk_o[...] = kk
    gate_o[...] = gate
    bonus_o[...] = _dot_exact_rhs(r * (kds[0] + kds[1]) * rk_ref[...], head_sum) * v


def _inproj_prep(x2, g_mix, w_in, mu_prev, mu_next, w0, w2, a0, a2, g2, k_k, k_a, r_k, seq, tile):
    rows, d = x2.shape
    c = RWKV_WIDTH
    cols = RWKV_COLS
    halo = SUBLANES_BF16
    per = tile // halo
    wb = w_in.astype(BF16)
    wr, wq, wkv = wb[:, :cols], wb[:, cols:cols + ATTN_WIDTH], wb[:, cols + ATTN_WIDTH:]
    zeros = jnp.zeros((2, DECAY_LORA, c), F32)
    w2p = jnp.concatenate([w2, zeros], axis=1).astype(BF16)
    a2p = jnp.concatenate([zeros, a2], axis=1).astype(BF16)
    row = lambda i: (i, 0)
    prv = lambda i: (jnp.maximum(i * per - 1, 0), 0)
    nxt = lambda i: (jnp.minimum((i + 1) * per, rows // halo - 1), 0)
    c2 = lambda i: (0, 0)
    c3 = lambda i: (0, 0, 0)
    resident = dict(pipeline_mode=pl.Buffered(1))
    out = jax.ShapeDtypeStruct((rows, c), F32)
    body = functools.partial(_inprep_body, tiles_per_seq=seq // tile)
    return pl.pallas_call(
        body,
        grid=(rows // tile,),
        in_specs=[pl.BlockSpec((tile, d), row), pl.BlockSpec((halo, d), prv), pl.BlockSpec((halo, d), nxt),
                  pl.BlockSpec((1, d), c2),
                  pl.BlockSpec(wr.shape, c2, **resident), pl.BlockSpec(wq.shape, c2, **resident),
                  pl.BlockSpec(wkv.shape, c2, **resident),
                  pl.BlockSpec((1, cols), c2), pl.BlockSpec((1, cols), c2),
                  pl.BlockSpec((2, c), c2), pl.BlockSpec((2, LANES, c), c3),
                  pl.BlockSpec((2, c), c2), pl.BlockSpec((2, LANES, c), c3),
                  pl.BlockSpec((GATE_LORA, c), c2),
                  pl.BlockSpec((1, c), c2), pl.BlockSpec((1, c), c2), pl.BlockSpec((1, c), c2)],
        out_specs=[pl.BlockSpec((tile, ATTN_WIDTH), row), pl.BlockSpec((tile, 2 * KV_COLS), row)]
                  + [pl.BlockSpec((tile, c), row)] * 11,
        out_shape=[jax.ShapeDtypeStruct((rows, ATTN_WIDTH), F32), jax.ShapeDtypeStruct((rows, 2 * KV_COLS), F32)]
                  + [out] * 11,
        compiler_params=pltpu.CompilerParams(dimension_semantics=("parallel",), vmem_limit_bytes=VMEM_LIMIT),
        name="inproj_prep",
    )(x2, x2, x2, g_mix.reshape(1, d), wr, wq, wkv, mu_prev.reshape(1, cols), mu_next.reshape(1, cols),
      w0, w2p, a0, a2p, g2.astype(BF16), k_k.reshape(1, c), k_a.reshape(1, c), r_k.reshape(1, c))


def _scan_body(rf, vf, kkf, lwf, kdf, bf, rb, vb, kkb, lwb, kdb, bb, yf_o, yb_o, h_scr):
    j = pl.program_id(0)

    @pl.when(j == 0)
    def _():
        h_scr[...] = jnp.zeros_like(h_scr)

    n_batch = rf.shape[0]
    cs = rf.shape[1]
    n_pairs = rf.shape[2] // PAIR

    cs_shift = cs.bit_length() - 1
    assert 1 << cs_shift == cs
    t_row = lax.broadcasted_iota(jnp.int32, (cs, 2 * cs), 0)
    t_col = lax.broadcasted_iota(jnp.int32, (cs, 2 * cs), 1) & (cs - 1)
    eye_cat = (t_row == t_col).astype(F32)
    ri = lax.broadcasted_iota(jnp.int32, (PAIR, PAIR), 0)
    ci = lax.broadcasted_iota(jnp.int32, (PAIR, PAIR), 1)
    eye = (ri == ci).astype(F32)
    same_head = (ri >> HEAD_SHIFT) == (ci >> HEAD_SHIFT)
    tr = lax.broadcasted_iota(jnp.int32, (cs, cs), 0)
    tc = lax.broadcasted_iota(jnp.int32, (cs, cs), 1)
    lane_even = lax.broadcasted_iota(jnp.int32, (cs, PAIR), 1) < HEAD_DIM
    col_first = lax.broadcasted_iota(jnp.int32, (cs, 2 * cs), 1) < cs

    def stack(x):
        zero = jnp.zeros_like(x)
        return jnp.concatenate([jnp.where(lane_even, x, zero), jnp.where(lane_even, zero, x)], axis=0)

    def block_diag(m):
        zero = jnp.zeros_like(m)
        return jnp.concatenate([jnp.where(col_first, m, zero), jnp.where(col_first, zero, m)], axis=0)

    dirs = ((0, rf, vf, kkf, lwf, kdf, bf, yf_o), (1, rb, vb, kkb, lwb, kdb, bb, yb_o))
    chains = []
    for bi, (d, r_ref, v_ref, kk_ref, lw_ref, kd_ref, b_ref, y_o) in (
            (bi, dr) for bi in range(n_batch) for dr in dirs):
        fwd = d == 0
        strict = (t_col < t_row) if fwd else (t_col > t_row)
        incl = strict | (t_col == t_row)
        cum = ((tc <= tr) if fwd else (tc >= tr)).astype(BF16)

        lw = lw_ref[bi]
        c_in = sum(_dot(cum, piece) for piece in _split_bf16(lw, 3))
        c_ex = c_in - lw
        c_tot = jnp.sum(lw, axis=0, keepdims=True)
        e_neg = jnp.exp(-c_in)
        e_end = jnp.exp(c_tot - c_in)
        gam = jnp.exp(c_tot)
        kd = kd_ref[bi]
        bv = b_ref[bi]
        a_t = -kk_ref[bi] * jnp.exp(c_ex)
        r_t = r_ref[bi] * jnp.exp(c_in)
        b_t = bv * e_neg
        k_t = kd * e_neg
        b_h = bv * e_end
        k_h = kd * e_end
        vv = v_ref[bi]
        for pr in range(n_pairs):
            sl = slice(pr * PAIR, (pr + 1) * PAIR)
            a_b, v_b = a_t[:, sl].astype(BF16), vv[:, sl].astype(BF16)
            chains.append(dict(
                bi=bi, d=d, pr=pr, sl=sl, y_o=y_o, strict=strict, incl=incl, r_t=r_t[:, sl], gam=gam[:, sl],
                a_b=a_b, r_b=r_t[:, sl].astype(BF16), a_sb=stack(a_b), v_b=v_b, v_sb=stack(v_b),
                bt_sb=stack(b_t[:, sl].astype(BF16)), kt_sb=stack(k_t[:, sl].astype(BF16)),
                bk_h=jnp.concatenate([b_h[:, sl].astype(BF16), k_h[:, sl].astype(BF16)], axis=0)))

    h2 = 2 * cs
    for c in chains:
        lhs = jnp.concatenate([c["a_b"], c["r_b"]], axis=0)
        rhs = jnp.concatenate([c["bt_sb"], c["kt_sb"]], axis=0)
        gram = _dot_nt(lhs, rhs)
        a_ab = jnp.where(c["strict"], gram[:cs, :h2], 0.0)
        c["a_xk"] = jnp.concatenate([jnp.where(c["strict"], gram[:cs, h2:], 0.0),
                                     jnp.where(c["incl"], gram[cs:, h2:], 0.0)], axis=0).astype(BF16)
        c["a_rb"] = jnp.where(c["incl"], gram[cs:, :h2], 0.0).astype(BF16)
        c["x_cat"] = a_ab.astype(BF16)
        c["s_cat"] = eye_cat + a_ab

    for c in chains:
        c["x_cat"] = _dot(c["x_cat"], block_diag(c["x_cat"])).astype(BF16)
    for _ in range(cs_shift - 2):
        for c in chains:
            sx = _dot(jnp.concatenate([c["s_cat"].astype(BF16), c["x_cat"]], axis=0), block_diag(c["x_cat"]))
            c["s_cat"] = c["s_cat"] + sx[:cs]
            c["x_cat"] = sx[cs:].astype(BF16)
    for c in chains:
        c["s_cat"] = (c["s_cat"] + _dot(c["s_cat"].astype(BF16), block_diag(c["x_cat"]))).astype(BF16)

    zero_b = jnp.zeros((cs, PAIR), BF16)
    for c in chains:
        av = _dot(c["a_xk"], c["v_sb"])
        c["w1_sb"] = stack(av[:cs].astype(BF16))
        c["av"] = av[cs:]
        c["h0"] = h_scr[c["bi"], c["d"], c["pr"]]
        c["h_b"] = c["h0"].astype(BF16)
        c["gam_col"] = jnp.sum(eye * c["gam"], axis=1, keepdims=True)
    for c in chains:
        c["pq"] = _dot(c["s_cat"], jnp.concatenate([c["a_sb"], c["w1_sb"]], axis=1)).astype(BF16)
    for c in chains:
        pq = c["pq"]
        ry = _dot(c["a_rb"], jnp.concatenate([stack(pq[:, :PAIR]), stack(pq[:, PAIR:])], axis=1))
        c["r_hat"] = (c["r_t"] + ry[:, :PAIR]).astype(BF16)
        c["y_hat"] = ry[:, PAIR:] + c["av"]
        rhs = jnp.concatenate([pq, jnp.concatenate([zero_b, c["v_b"]], axis=1)], axis=0)
        gd = _dot_tn(c["bk_h"], rhs)
        c["btp"] = jnp.where(same_head, gd[:, :PAIR], 0.0).astype(BF16)
        c["dd"] = jnp.where(same_head, gd[:, PAIR:], 0.0)
    for c in chains:
        yh = _dot(jnp.concatenate([c["r_hat"], c["btp"]], axis=0), c["h_b"])
        c["y_o"][c["bi"], :, c["sl"]] = yh[:cs] + c["y_hat"]
        h_scr[c["bi"], c["d"], c["pr"]] = c["gam_col"] * c["h0"] + yh[cs:] + c["dd"]


def _rwkv_scan(r, v, kk, lw0, lw1, kd0, kd1, b0, b1):
    b, t, c = r.shape
    nc = t // CHUNK
    fw = lambda j: (0, j, 0)
    bw = lambda j: (0, nc - 1 - j, 0)
    blk = (b, CHUNK, c)
    out = jax.ShapeDtypeStruct((b, t, c), F32)
    return pl.pallas_call(
        _scan_body,
        grid=(nc,),
        in_specs=[pl.BlockSpec(blk, fw)] * 6 + [pl.BlockSpec(blk, bw)] * 6,
        out_specs=[pl.BlockSpec(blk, fw), pl.BlockSpec(blk, bw)],
        out_shape=[out, out],
        scratch_shapes=[pltpu.VMEM((b, 2, c // PAIR, PAIR, PAIR), F32)],
        compiler_params=pltpu.CompilerParams(dimension_semantics=("arbitrary",),
                                             vmem_limit_bytes=VMEM_LIMIT),
        name="rwkv_scan",
    )(r, v, kk, lw0, kd0, b0, r, v, kk, lw1, kd1, b1)


def _attn_body(q_ref, kvp_ref, kvc_ref, kvn_ref, qg_ref, kg_ref, bias_ref, sink_ref, o_ref):
    n = pl.program_id(1)
    nb = pl.num_programs(1)
    blk = q_ref.shape[1]
    head_sum = _head_sum_matrix(LANES)
    head_mean = lambda z: _dot_exact_rhs(z, head_sum) * (1.0 / HEAD_DIM)

    k_win = jnp.concatenate([kvp_ref[0, :, :KV_COLS], kvc_ref[0, :, :KV_COLS], kvn_ref[0, :, :KV_COLS]], axis=0)
    v_win = jnp.concatenate([kvp_ref[0, :, KV_COLS:], kvc_ref[0, :, KV_COLS:], kvn_ref[0, :, KV_COLS:]], axis=0)
    kn = k_win * lax.rsqrt(head_mean(k_win * k_win) + NORM_EPS) * kg_ref[...]
    v_b = v_win.astype(BF16)

    lane = lax.broadcasted_iota(jnp.int32, kn.shape, 1)
    k_at = []
    for g in range(KV_HEADS):
        own = jnp.where((lane >> HEAD_SHIFT) == g, kn, 0.0)
        other = pltpu.roll(own, HEAD_DIM, 1)
        k_at.append([own if p == g else other for p in range(2)])

    row = lax.broadcasted_iota(jnp.int32, (blk, 3 * blk), 0)
    col = lax.broadcasted_iota(jnp.int32, (blk, 3 * blk), 1)
    rel = col - blk - row
    valid = (jnp.abs(rel) <= WINDOW)
    valid &= (col >= blk) | (n > 0)
    valid &= (col < 2 * blk) | (n < nb - 1)
    out_lane_even = lax.broadcasted_iota(jnp.int32, (blk, LANES), 1) < HEAD_DIM

    group = Q_HEADS // KV_HEADS
    n_slabs = ATTN_WIDTH // LANES
    slabs_per_group = n_slabs // KV_HEADS
    qn = []
    for s in range(n_slabs):
        q = q_ref[0, :, s * LANES:(s + 1) * LANES]
        qn.append((q * lax.rsqrt(head_mean(q * q) + NORM_EPS) * qg_ref[...] * (HEAD_DIM ** -0.5)).astype(BF16))
    scores = []
    for g in range(KV_HEADS):
        kcat = jnp.concatenate([k_at[g][0], k_at[g][1]], axis=0).astype(BF16)
        q_g = jnp.concatenate(qn[g * slabs_per_group:(g + 1) * slabs_per_group], axis=0)
        scores.append(_dot_nt(q_g, kcat))
    heads = range(Q_HEADS)
    sc = []
    for h in heads:
        s, p = h // 2, h % 2
        g, sg = s // slabs_per_group, s % slabs_per_group
        z = scores[g][sg * blk:(sg + 1) * blk, p * 3 * blk:(p + 1) * 3 * blk] + bias_ref[h]
        sc.append(jnp.where(valid, z, -jnp.inf))
    sinks = [sink_ref[h] for h in heads]
    m = [jnp.maximum(jnp.max(sc[h], axis=-1, keepdims=True), sinks[h]) for h in heads]
    e = [jnp.exp(sc[h] - m[h]) for h in heads]
    denom = [jnp.sum(e[h], axis=-1, keepdims=True) + jnp.exp(sinks[h] - m[h]) for h in heads]
    probs = [(e[h] / denom[h]).astype(BF16) for h in heads]
    o_all = _dot(jnp.concatenate(probs, axis=0), v_b)
    for s in range(n_slabs):
        g = (2 * s) // group
        halves = []
        for p in range(2):
            h = 2 * s + p
            o = o_all[h * blk:(h + 1) * blk]
            halves.append(o if p == g else pltpu.roll(o, HEAD_DIM, 1))
        o_ref[0, :, s * LANES:(s + 1) * LANES] = jnp.where(out_lane_even, halves[0], halves[1])


def _t5_bucket(rel):
    nb = REL_BUCKETS // 2
    max_exact = nb // 2
    ret = jnp.where(rel > 0, nb, 0)
    n = jnp.abs(rel)
    large = max_exact + (jnp.log(jnp.maximum(n, 1).astype(F32) / max_exact)
                         / math.log(REL_MAX_DIST / max_exact) * (nb - max_exact)).astype(jnp.int32)
    large = jnp.minimum(large, nb - 1)
    return ret + jnp.where(n < max_exact, n, large)


BIAS_SPAN = 4 * BLOCK


def _bias_body(tab_ref, o_ref):
    blk = o_ref.shape[1]
    x = jnp.broadcast_to(tab_ref[0], (blk, BIAS_SPAN))
    shifted = pltpu.roll(x, BIAS_SPAN - (blk - 1), 1, stride=1, stride_axis=0)
    o_ref[0] = shifted[:, :3 * blk]


def _bias_table(rel_bias):
    rel = jnp.arange(BIAS_SPAN) - (2 * BLOCK - 1)
    tab = jnp.transpose(rel_bias[_t5_bucket(rel)].astype(F32))
    shape = (BLOCK, 3 * BLOCK)
    return pl.pallas_call(
        _bias_body,
        grid=(Q_HEADS,),
        in_specs=[pl.BlockSpec((1, 1, BIAS_SPAN), lambda h: (h, 0, 0))],
        out_specs=pl.BlockSpec((1,) + shape, lambda h: (h, 0, 0)),
        out_shape=jax.ShapeDtypeStruct((Q_HEADS,) + shape, F32),
        name="bias_table",
    )(tab.reshape(Q_HEADS, 1, BIAS_SPAN))


def _attention(q, kv, q_gain, k_gain, rel_bias, sink):
    b, t, _ = q.shape
    nb = t // BLOCK
    bias = _bias_table(rel_bias)
    cur = lambda bi, n: (bi, n, 0)
    prv = lambda bi, n: (bi, jnp.maximum(n - 1, 0), 0)
    nxt = lambda bi, n: (bi, jnp.minimum(n + 1, nb - 1), 0)
    c2 = lambda bi, n: (0, 0)
    c3 = lambda bi, n: (0, 0, 0)
    kvblk = (1, BLOCK, 2 * KV_COLS)
    return pl.pallas_call(
        _attn_body,
        grid=(b, nb),
        in_specs=[pl.BlockSpec((1, BLOCK, ATTN_WIDTH), cur), pl.BlockSpec(kvblk, prv), pl.BlockSpec(kvblk, cur),
                  pl.BlockSpec(kvblk, nxt), pl.BlockSpec((1, LANES), c2), pl.BlockSpec((1, LANES), c2),
                  pl.BlockSpec((Q_HEADS, BLOCK, 3 * BLOCK), c3), pl.BlockSpec(memory_space=pltpu.SMEM)],
        out_specs=pl.BlockSpec((1, BLOCK, ATTN_WIDTH), cur),
        out_shape=jax.ShapeDtypeStruct((b, t, ATTN_WIDTH), F32),
        compiler_params=pltpu.CompilerParams(dimension_semantics=("parallel", "parallel"),
                                             vmem_limit_bytes=VMEM_LIMIT),
        name="band_attn",
    )(q, kv, kv, kv, jnp.tile(q_gain, 2).reshape(1, LANES), jnp.tile(k_gain, 2).reshape(1, LANES),
      bias, sink.astype(F32))


def _outproj_body(x_ref, yf_ref, yb_ref, bonus_ref, gate_ref, attn_ref, lw_ref, lb_ref, wo_ref, gf_ref,
                  x1_ref, h2_ref):
    c = RWKV_WIDTH
    head_sum = _head_sum_matrix(c)
    y = yf_ref[...] + yb_ref[...]
    mu = _dot_exact_rhs(y, head_sum) * (1.0 / HEAD_DIM)
    yc = y - mu
    var = _dot_exact_rhs(yc * yc, head_sum) * (1.0 / HEAD_DIM)
    yn = yc * lax.rsqrt(var + LNX_EPS) * lw_ref[...] + lb_ref[...]
    mix_r = (yn + bonus_ref[...]) * gate_ref[...]
    x1 = (x_ref[...] + _dot(mix_r.astype(BF16), wo_ref[:c, :])
          + _dot(attn_ref[...].astype(BF16), wo_ref[c:, :]))
    x1_ref[...] = x1
    h2 = x1 * lax.rsqrt(jnp.mean(x1 * x1, axis=-1, keepdims=True) + NORM_EPS) * gf_ref[...]
    h2_ref[...] = h2.astype(BF16)


def _outproj(x2, yf, yb, bonus, gate, attn, lnx_w, lnx_b, w_out, g_ffn, tile):
    rows, d = x2.shape
    c = RWKV_WIDTH
    row = lambda i: (i, 0)
    const = lambda i: (0, 0)
    return pl.pallas_call(
        _outproj_body,
        grid=(rows // tile,),
        in_specs=[pl.BlockSpec((tile, d), row)] + [pl.BlockSpec((tile, c), row)] * 5
                 + [pl.BlockSpec((1, c), const), pl.BlockSpec((1, c), const),
                    pl.BlockSpec(w_out.shape, const), pl.BlockSpec((1, d), const)],
        out_specs=[pl.BlockSpec((tile, d), row), pl.BlockSpec((tile, d), row)],
        out_shape=[jax.ShapeDtypeStruct((rows, d), F32), jax.ShapeDtypeStruct((rows, d), BF16)],
        compiler_params=pltpu.CompilerParams(dimension_semantics=("parallel",), vmem_limit_bytes=VMEM_LIMIT),
        name="outproj",
    )(x2, yf, yb, bonus, gate, attn, lnx_w.reshape(1, c), lnx_b.reshape(1, c), w_out.astype(BF16),
      g_ffn.reshape(1, d))


def _ffn_body(x1_ref, h_ref, hp_ref, hn_ref, wu_ref, cw_ref, cb_ref, wd_ref, o_ref, act_scr,
              *, tiles_per_seq, ff_chunk):
    i = pl.program_id(0)
    tile = h_ref.shape[0]
    halo = hp_ref.shape[0]
    d_ff = wd_ref.shape[0]
    first = (i % tiles_per_seq) == 0
    last = (i % tiles_per_seq) == tiles_per_seq - 1
    hp = hp_ref[...]
    hn = hn_ref[...]
    hp = jnp.where(first, jnp.zeros_like(hp), hp)
    hn = jnp.where(last, jnp.zeros_like(hn), hn)
    hcat = jnp.concatenate([hp, h_ref[...], hn], axis=0)
    total = tile + 2 * halo

    def conv(cols):
        u = _dot(hcat, wu_ref[:, cols])
        up = pltpu.roll(u, 1, 0)[halo:halo + tile]
        un = pltpu.roll(u, total - 1, 0)[halo:halo + tile]
        return (up * cw_ref[0:1, cols] + u[halo:halo + tile] * cw_ref[1:2, cols] + un * cw_ref[2:3, cols]
                + cb_ref[:, cols])

    for c in range(d_ff // ff_chunk):
        g = conv(slice(c * ff_chunk, (c + 1) * ff_chunk))
        val = conv(slice(d_ff + c * ff_chunk, d_ff + (c + 1) * ff_chunk))
        act_scr[:, c * ff_chunk:(c + 1) * ff_chunk] = (g * _sigmoid(g) * val).astype(BF16)
    o_ref[...] = x1_ref[...] + _dot(act_scr[...], wd_ref[...])


def _ffn(x1, h2, w_up, conv_w, conv_b, w_down, seq, tile, ff_chunk):
    rows, d = x1.shape
    d_ff = w_down.shape[0]
    halo = SUBLANES_BF16
    per = tile // halo
    row = lambda i: (i, 0)
    prv = lambda i: (jnp.maximum(i * per - 1, 0), 0)
    nxt = lambda i: (jnp.minimum((i + 1) * per, rows // halo - 1), 0)
    const = lambda i: (0, 0)
    resident = dict(pipeline_mode=pl.Buffered(1))
    body = functools.partial(_ffn_body, tiles_per_seq=seq // tile, ff_chunk=ff_chunk)
    return pl.pallas_call(
        body,
        grid=(rows // tile,),
        in_specs=[pl.BlockSpec((tile, d), row), pl.BlockSpec((tile, d), row),
                  pl.BlockSpec((halo, d), prv), pl.BlockSpec((halo, d), nxt),
                  pl.BlockSpec((d, 2 * d_ff), const, **resident),
                  pl.BlockSpec((3, 2 * d_ff), const), pl.BlockSpec((1, 2 * d_ff), const),
                  pl.BlockSpec((d_ff, d), const, **resident)],
        out_specs=pl.BlockSpec((tile, d), row),
        out_shape=jax.ShapeDtypeStruct((rows, d), F32),
        scratch_shapes=[pltpu.VMEM((tile, d_ff), BF16)],
        compiler_params=pltpu.CompilerParams(dimension_semantics=("parallel",), vmem_limit_bytes=VMEM_LIMIT),
        name="convglu",
    )(x1, h2, h2, h2, w_up.astype(BF16), conv_w, conv_b.reshape(1, 2 * d_ff), w_down.astype(BF16))


def _layer(x, g_mix, w_in, mu_prev, mu_next, w0, w2, a0, a2, g2, k_k, k_a, r_k, lnx_w, lnx_b,
           q_gain, k_gain, rel_bias, sink, w_out, g_ffn, w_up, conv_w, conv_b, w_down):
    b, t, d = x.shape
    rows = b * t
    x2 = x.reshape(rows, d)
    q, kv, *ops = _inproj_prep(x2, g_mix, w_in, mu_prev, mu_next, w0, w2, a0, a2, g2, k_k, k_a, r_k.reshape(-1),
                               seq=t, tile=min(256, t))
    r, v, kk, lw0, lw1, kd0, kd1, b0, b1 = (z.reshape(b, t, RWKV_WIDTH) for z in ops[:9])
    gate, bonus = ops[9:]
    yf, yb = _rwkv_scan(r, v, kk, lw0, lw1, kd0, kd1, b0, b1)
    attn = _attention(q.reshape(b, t, ATTN_WIDTH), kv.reshape(b, t, 2 * KV_COLS), q_gain, k_gain, rel_bias, sink)
    flat = lambda z: z.reshape(rows, z.shape[-1])
    x1, h2 = _outproj(x2, flat(yf), flat(yb), bonus, gate, flat(attn), lnx_w, lnx_b, w_out, g_ffn,
                      tile=min(512, t))
    out = _ffn(x1, h2, w_up, conv_w, conv_b, w_down, seq=t, tile=min(512, t), ff_chunk=256)
    return out.reshape(b, t, d)


def kernel(x, g_mix, w_in, mu_prev, mu_next, w0, w2, a0, a2, g2, k_k, k_a, r_k, lnx_w, lnx_b, q_gain, k_gain,
           rel_bias, sink, w_out, g_ffn, w_up, conv_w, conv_b, w_down):
    depth = g_mix.shape[0]
    for l in range(depth):
        x = _layer(x, g_mix[l], w_in[l], mu_prev[l], mu_next[l], w0[l], w2[l], a0[l], a2[l], g2[l], k_k[l], k_a[l],
                   r_k[l], lnx_w[l], lnx_b[l], q_gain[l], k_gain[l], rel_bias, sink[l], w_out[l], g_ffn[l],
                   w_up[l], conv_w[l], conv_b[l], w_down[l])
    return x
```

```python
import functools
import math

import jax
import jax.numpy as jnp
from jax import lax
from jax.experimental import pallas as pl
from jax.experimental.pallas import tpu as pltpu

F32 = jnp.float32
BF16 = jnp.bfloat16

HEAD_DIM = 64
RWKV_WIDTH = 512
ATTN_WIDTH = 512
KV_HEADS = 2
Q_HEADS = 8
DECAY_LORA = 64
ICLR_LORA = 64
GATE_LORA = 128
RWKV_COLS = 3 * RWKV_WIDTH + DECAY_LORA + ICLR_LORA + GATE_LORA
KV_COLS = KV_HEADS * HEAD_DIM
WINDOW = 128
BLOCK = 128
REL_BUCKETS = 32
REL_MAX_DIST = 128
NORM_EPS = 1e-6
LNX_EPS = 64e-5
KK_EPS = 1e-12

LANES = 128
MXU_DIM = 256
SUBLANES_F32 = 8
SUBLANES_BF16 = 16
VMEM_LIMIT = 48 * 1024 * 1024

CHUNK = 64
PAIR = 2 * HEAD_DIM
assert PAIR == LANES
HEAD_SHIFT = HEAD_DIM.bit_length() - 1
assert 1 << HEAD_SHIFT == HEAD_DIM


def _dot(a, b, precision=None):
    return jnp.dot(a, b, preferred_element_type=F32, precision=precision)


def _dot_nt(a, b, precision=None):
    return lax.dot_general(a, b, (((1,), (1,)), ((), ())), preferred_element_type=F32, precision=precision)


def _dot_tn(a, b, precision=None):
    return lax.dot_general(a, b, (((0,), (0,)), ((), ())), preferred_element_type=F32, precision=precision)


def _sigmoid(x):
    return 1.0 / (1.0 + jnp.exp(-x))


def _split_bf16(x, parts):
    out = []
    for _ in range(parts):
        h = x.astype(BF16)
        out.append(h)
        x = x - h.astype(F32)
    return out


def _head_sum(x, parts):
    width = x.shape[-1]
    blk = min(width, MXU_DIM)
    r = lax.broadcasted_iota(jnp.int32, (blk, blk), 0) >> HEAD_SHIFT
    c = lax.broadcasted_iota(jnp.int32, (blk, blk), 1) >> HEAD_SHIFT
    m = (r == c).astype(BF16)
    pieces = _split_bf16(x, parts)
    cols = []
    for c0 in range(0, width, blk):
        acc = None
        for h in pieces:
            t = _dot(h[:, c0:c0 + blk], m)
            acc = t if acc is None else acc + t
        cols.append(acc)
    return cols[0] if len(cols) == 1 else jnp.concatenate(cols, axis=1)


def _inprep_body(x_ref, xp_ref, xn_ref, g_ref, wr_ref, wq_ref, wkv_ref,
                 mup_ref, mun_ref, w0_ref, w2_ref, a0_ref, a2_ref, g2_ref, kk_ref, ka_ref, rk_ref,
                 q_o, kv_o, r_o, v_o, kk_o, lw0_o, lw1_o, kd0_o, kd1_o, b0_o, b1_o, gate_o, bonus_o,
                 *, tiles_per_seq):
    i = pl.program_id(0)
    tile = x_ref.shape[0]
    halo = xp_ref.shape[0]
    first = (i % tiles_per_seq) == 0
    last = (i % tiles_per_seq) == tiles_per_seq - 1

    def norm(x):
        return (x * lax.rsqrt(jnp.mean(x * x, axis=-1, keepdims=True) + NORM_EPS) * g_ref[...]).astype(BF16)

    h = norm(x_ref[...])
    q_o[...] = _dot(h, wq_ref[...])
    kv_o[...] = _dot(h, wkv_ref[...])
    hp = norm(xp_ref[...])
    hn = norm(xn_ref[...])
    hp = jnp.where(first, jnp.zeros_like(hp), hp)
    hn = jnp.where(last, jnp.zeros_like(hn), hn)
    pcat = _dot(jnp.concatenate([hp, h, hn], axis=0), wr_ref[...])
    total = tile + 2 * halo
    p = pcat[halo:halo + tile]
    prev = pltpu.roll(pcat, 1, 0)[halo:halo + tile]
    nxt = pltpu.roll(pcat, total - 1, 0)[halo:halo + tile]
    pf = p + mup_ref[...] * (prev - p) + mun_ref[...] * (nxt - p)

    c = RWKV_WIDTH
    r, k, v = pf[:, :c], pf[:, c:2 * c], pf[:, 2 * c:3 * c]
    lora = pf[:, 3 * c:3 * c + LANES]
    xg = pf[:, 3 * c + LANES:]
    lora_t = jnp.tanh(lora).astype(BF16)
    lora_b = lora.astype(BF16)

    kx = k * kk_ref[...]
    kk = kx * lax.rsqrt(_head_sum(kx * kx, 1) + KK_EPS)
    gate = _dot(_sigmoid(xg).astype(BF16), g2_ref[...])

    kds = []
    for d, (lw_o, kd_o, b_o) in enumerate(((lw0_o, kd0_o, b0_o), (lw1_o, kd1_o, b1_o))):
        w_raw = w0_ref[d:d + 1, :] + _dot(lora_t, w2_ref[d])
        lw_o[...] = (-math.exp(-0.5)) * _sigmoid(w_raw)
        iclr = _sigmoid(a0_ref[d:d + 1, :] + _dot(lora_b, a2_ref[d]))
        kd = k * (1.0 + (iclr - 1.0) * ka_ref[...])
        kd_o[...] = kd
        b_o[...] = iclr * kk
        kds.append(kd)

    r_o[...] = r
    v_o[...] = v
    kk_o[...] = kk
    gate_o[...] = gate
    bonus_o[...] = _head_sum(r * (kds[0] + kds[1]) * rk_ref[...], 1) * v


def _inproj_prep(x2, g_mix, w_in, mu_prev, mu_next, w0, w2, a0, a2, g2, k_k, k_a, r_k, seq, tile):
    rows, d = x2.shape
    c = RWKV_WIDTH
    cols = RWKV_COLS
    halo = SUBLANES_BF16
    per = tile // halo
    wb = w_in.astype(BF16)
    wr, wq, wkv = wb[:, :cols], wb[:, cols:cols + ATTN_WIDTH], wb[:, cols + ATTN_WIDTH:]
    zeros = jnp.zeros((2, DECAY_LORA, c), F32)
    w2p = jnp.concatenate([w2, zeros], axis=1).astype(BF16)
    a2p = jnp.concatenate([zeros, a2], axis=1).astype(BF16)
    row = lambda i: (i, 0)
    prv = lambda i: (jnp.maximum(i * per - 1, 0), 0)
    nxt = lambda i: (jnp.minimum((i + 1) * per, rows // halo - 1), 0)
    c2 = lambda i: (0, 0)
    c3 = lambda i: (0, 0, 0)
    resident = dict(pipeline_mode=pl.Buffered(1))
    out = jax.ShapeDtypeStruct((rows, c), F32)
    body = functools.partial(_inprep_body, tiles_per_seq=seq // tile)
    return pl.pallas_call(
        body,
        grid=(rows // tile,),
        in_specs=[pl.BlockSpec((tile, d), row), pl.BlockSpec((halo, d), prv), pl.BlockSpec((halo, d), nxt),
                  pl.BlockSpec((1, d), c2),
                  pl.BlockSpec(wr.shape, c2, **resident), pl.BlockSpec(wq.shape, c2, **resident),
                  pl.BlockSpec(wkv.shape, c2, **resident),
                  pl.BlockSpec((1, cols), c2), pl.BlockSpec((1, cols), c2),
                  pl.BlockSpec((2, c), c2), pl.BlockSpec((2, LANES, c), c3),
                  pl.BlockSpec((2, c), c2), pl.BlockSpec((2, LANES, c), c3),
                  pl.BlockSpec((GATE_LORA, c), c2),
                  pl.BlockSpec((1, c), c2), pl.BlockSpec((1, c), c2), pl.BlockSpec((1, c), c2)],
        out_specs=[pl.BlockSpec((tile, ATTN_WIDTH), row), pl.BlockSpec((tile, 2 * KV_COLS), row)]
                  + [pl.BlockSpec((tile, c), row)] * 11,
        out_shape=[jax.ShapeDtypeStruct((rows, ATTN_WIDTH), F32), jax.ShapeDtypeStruct((rows, 2 * KV_COLS), F32)]
                  + [out] * 11,
        compiler_params=pltpu.CompilerParams(dimension_semantics=("parallel",), vmem_limit_bytes=VMEM_LIMIT),
        name="inproj_prep",
    )(x2, x2, x2, g_mix.reshape(1, d), wr, wq, wkv, mu_prev.reshape(1, cols), mu_next.reshape(1, cols),
      w0, w2p, a0, a2p, g2.astype(BF16), k_k.reshape(1, c), k_a.reshape(1, c), r_k.reshape(1, c))


def _scan_body(rf, vf, kkf, lwf, kdf, bf, rb, vb, kkb, lwb, kdb, bb, yf_o, yb_o, h_scr):
    j = pl.program_id(0)

    @pl.when(j == 0)
    def _():
        h_scr[...] = jnp.zeros_like(h_scr)

    n_batch = rf.shape[0]
    cs = rf.shape[1]
    n_pairs = rf.shape[2] // PAIR

    cs_shift = cs.bit_length() - 1
    assert 1 << cs_shift == cs
    t_row = lax.broadcasted_iota(jnp.int32, (cs, 2 * cs), 0)
    t_col = lax.broadcasted_iota(jnp.int32, (cs, 2 * cs), 1) & (cs - 1)
    eye_cat = (t_row == t_col).astype(F32)
    ri = lax.broadcasted_iota(jnp.int32, (PAIR, PAIR), 0)
    ci = lax.broadcasted_iota(jnp.int32, (PAIR, PAIR), 1)
    eye = (ri == ci).astype(F32)
    same_head = (ri >> HEAD_SHIFT) == (ci >> HEAD_SHIFT)
    tr = lax.broadcasted_iota(jnp.int32, (cs, cs), 0)
    tc = lax.broadcasted_iota(jnp.int32, (cs, cs), 1)
    lane_even = lax.broadcasted_iota(jnp.int32, (cs, PAIR), 1) < HEAD_DIM
    col_first = lax.broadcasted_iota(jnp.int32, (cs, 2 * cs), 1) < cs

    def stack(x):
        zero = jnp.zeros_like(x)
        return jnp.concatenate([jnp.where(lane_even, x, zero), jnp.where(lane_even, zero, x)], axis=0)

    def block_diag(m):
        zero = jnp.zeros_like(m)
        return jnp.concatenate([jnp.where(col_first, m, zero), jnp.where(col_first, zero, m)], axis=0)

    dirs = ((0, rf, vf, kkf, lwf, kdf, bf, yf_o), (1, rb, vb, kkb, lwb, kdb, bb, yb_o))
    chains = []
    for bi, (d, r_ref, v_ref, kk_ref, lw_ref, kd_ref, b_ref, y_o) in (
            (bi, dr) for bi in range(n_batch) for dr in dirs):
        fwd = d == 0
        strict = (t_col < t_row) if fwd else (t_col > t_row)
        incl = strict | (t_col == t_row)
        cum = ((tc <= tr) if fwd else (tc >= tr)).astype(BF16)

        lw = lw_ref[bi]
        c_in = sum(_dot(cum, piece) for piece in _split_bf16(lw, 3))
        c_ex = c_in - lw
        c_tot = jnp.sum(lw, axis=0, keepdims=True)
        e_neg = jnp.exp(-c_in)
        e_end = jnp.exp(c_tot - c_in)
        gam = jnp.exp(c_tot)
        kd = kd_ref[bi]
        bv = b_ref[bi]
        a_t = -kk_ref[bi] * jnp.exp(c_ex)
        r_t = r_ref[bi] * jnp.exp(c_in)
        b_t = bv * e_neg
        k_t = kd * e_neg
        b_h = bv * e_end
        k_h = kd * e_end
        vv = v_ref[bi]
        for pr in range(n_pairs):
            sl = slice(pr * PAIR, (pr + 1) * PAIR)
            a_b, v_b = a_t[:, sl].astype(BF16), vv[:, sl].astype(BF16)
            chains.append(dict(
                bi=bi, d=d, pr=pr, sl=sl, y_o=y_o, strict=strict, incl=incl, r_t=r_t[:, sl], gam=gam[:, sl],
                a_b=a_b, r_b=r_t[:, sl].astype(BF16), a_sb=stack(a_b), v_b=v_b, v_sb=stack(v_b),
                bt_sb=stack(b_t[:, sl].astype(BF16)), kt_sb=stack(k_t[:, sl].astype(BF16)),
                bk_h=jnp.concatenate([b_h[:, sl].astype(BF16), k_h[:, sl].astype(BF16)], axis=0)))

    h2 = 2 * cs
    for c in chains:
        lhs = jnp.concatenate([c["a_b"], c["r_b"]], axis=0)
        rhs = jnp.concatenate([c["bt_sb"], c["kt_sb"]], axis=0)
        gram = _dot_nt(lhs, rhs)
        a_ab = jnp.where(c["strict"], gram[:cs, :h2], 0.0)
        c["a_xk"] = jnp.concatenate([jnp.where(c["strict"], gram[:cs, h2:], 0.0),
                                     jnp.where(c["incl"], gram[cs:, h2:], 0.0)], axis=0).astype(BF16)
        c["a_rb"] = jnp.where(c["incl"], gram[cs:, :h2], 0.0).astype(BF16)
        c["x_cat"] = a_ab.astype(BF16)
        c["s_cat"] = eye_cat + a_ab

    for c in chains:
        c["x_cat"] = _dot(c["x_cat"], block_diag(c["x_cat"])).astype(BF16)
    for _ in range(cs_shift - 2):
        for c in chains:
            sx = _dot(jnp.concatenate([c["s_cat"].astype(BF16), c["x_cat"]], axis=0), block_diag(c["x_cat"]))
            c["s_cat"] = c["s_cat"] + sx[:cs]
            c["x_cat"] = sx[cs:].astype(BF16)
    for c in chains:
        c["s_cat"] = (c["s_cat"] + _dot(c["s_cat"].astype(BF16), block_diag(c["x_cat"]))).astype(BF16)

    zero_b = jnp.zeros((cs, PAIR), BF16)
    for c in chains:
        av = _dot(c["a_xk"], c["v_sb"])
        c["w1_sb"] = stack(av[:cs].astype(BF16))
        c["av"] = av[cs:]
        c["h0"] = h_scr[c["bi"], c["d"], c["pr"]]
        c["h_b"] = c["h0"].astype(BF16)
        c["gam_col"] = jnp.sum(eye * c["gam"], axis=1, keepdims=True)
    for c in chains:
        c["pq"] = _dot(c["s_cat"], jnp.concatenate([c["a_sb"], c["w1_sb"]], axis=1)).astype(BF16)
    for c in chains:
        pq = c["pq"]
        ry = _dot(c["a_rb"], jnp.concatenate([stack(pq[:, :PAIR]), stack(pq[:, PAIR:])], axis=1))
        c["r_hat"] = (c["r_t"] + ry[:, :PAIR]).astype(BF16)
        c["y_hat"] = ry[:, PAIR:] + c["av"]
        rhs = jnp.concatenate([pq, jnp.concatenate([zero_b, c["v_b"]], axis=1)], axis=0)
        gd = _dot_tn(c["bk_h"], rhs)
        c["btp"] = jnp.where(same_head, gd[:, :PAIR], 0.0).astype(BF16)
        c["dd"] = jnp.where(same_head, gd[:, PAIR:], 0.0)
    for c in chains:
        yh = _dot(jnp.concatenate([c["r_hat"], c["btp"]], axis=0), c["h_b"])
        c["y_o"][c["bi"], :, c["sl"]] = yh[:cs] + c["y_hat"]
        h_scr[c["bi"], c["d"], c["pr"]] = c["gam_col"] * c["h0"] + yh[cs:] + c["dd"]


def _rwkv_scan(r, v, kk, lw0, lw1, kd0, kd1, b0, b1):
    b, t, c = r.shape
    nc = t // CHUNK
    fw = lambda j: (0, j, 0)
    bw = lambda j: (0, nc - 1 - j, 0)
    blk = (b, CHUNK, c)
    out = jax.ShapeDtypeStruct((b, t, c), F32)
    return pl.pallas_call(
        _scan_body,
        grid=(nc,),
        in_specs=[pl.BlockSpec(blk, fw)] * 6 + [pl.BlockSpec(blk, bw)] * 6,
        out_specs=[pl.BlockSpec(blk, fw), pl.BlockSpec(blk, bw)],
        out_shape=[out, out],
        scratch_shapes=[pltpu.VMEM((b, 2, c // PAIR, PAIR, PAIR), F32)],
        compiler_params=pltpu.CompilerParams(dimension_semantics=("arbitrary",),
                                             vmem_limit_bytes=VMEM_LIMIT),
        name="rwkv_scan",
    )(r, v, kk, lw0, kd0, b0, r, v, kk, lw1, kd1, b1)


def _attn_body(q_ref, kvp_ref, kvc_ref, kvn_ref, qg_ref, kg_ref, bias_ref, sink_ref, o_ref):
    n = pl.program_id(1)
    nb = pl.num_programs(1)
    blk = q_ref.shape[1]
    head_mean = lambda z: _head_sum(z, 1) * (1.0 / HEAD_DIM)

    k_win = jnp.concatenate([kvp_ref[0, :, :KV_COLS], kvc_ref[0, :, :KV_COLS], kvn_ref[0, :, :KV_COLS]], axis=0)
    v_win = jnp.concatenate([kvp_ref[0, :, KV_COLS:], kvc_ref[0, :, KV_COLS:], kvn_ref[0, :, KV_COLS:]], axis=0)
    kn = k_win * lax.rsqrt(head_mean(k_win * k_win) + NORM_EPS) * kg_ref[...]
    v_b = v_win.astype(BF16)

    lane = lax.broadcasted_iota(jnp.int32, kn.shape, 1)
    k_at = []
    for g in range(KV_HEADS):
        own = jnp.where((lane >> HEAD_SHIFT) == g, kn, 0.0)
        other = pltpu.roll(own, HEAD_DIM, 1)
        k_at.append([own if p == g else other for p in range(2)])

    row = lax.broadcasted_iota(jnp.int32, (blk, 3 * blk), 0)
    col = lax.broadcasted_iota(jnp.int32, (blk, 3 * blk), 1)
    rel = col - blk - row
    valid = (jnp.abs(rel) <= WINDOW)
    valid &= (col >= blk) | (n > 0)
    valid &= (col < 2 * blk) | (n < nb - 1)
    out_lane_even = lax.broadcasted_iota(jnp.int32, (blk, LANES), 1) < HEAD_DIM

    group = Q_HEADS // KV_HEADS
    n_slabs = ATTN_WIDTH // LANES
    slabs_per_group = n_slabs // KV_HEADS
    qn = []
    for s in range(n_slabs):
        q = q_ref[0, :, s * LANES:(s + 1) * LANES]
        qn.append((q * lax.rsqrt(head_mean(q * q) + NORM_EPS) * qg_ref[...] * (HEAD_DIM ** -0.5)).astype(BF16))
    scores = []
    for g in range(KV_HEADS):
        kcat = jnp.concatenate([k_at[g][0], k_at[g][1]], axis=0).astype(BF16)
        q_g = jnp.concatenate(qn[g * slabs_per_group:(g + 1) * slabs_per_group], axis=0)
        scores.append(_dot_nt(q_g, kcat))
    heads = range(Q_HEADS)
    sc = []
    for h in heads:
        s, p = h // 2, h % 2
        g, sg = s // slabs_per_group, s % slabs_per_group
        z = scores[g][sg * blk:(sg + 1) * blk, p * 3 * blk:(p + 1) * 3 * blk] + bias_ref[h]
        sc.append(jnp.where(valid, z, -jnp.inf))
    sinks = [sink_ref[h] for h in heads]
    m = [jnp.maximum(jnp.max(sc[h], axis=-1, keepdims=True), sinks[h]) for h in heads]
    e = [jnp.exp(sc[h] - m[h]) for h in heads]
    denom = [jnp.sum(e[h], axis=-1, keepdims=True) + jnp.exp(sinks[h] - m[h]) for h in heads]
    probs = [(e[h] / denom[h]).astype(BF16) for h in heads]
    o_all = _dot(jnp.concatenate(probs, axis=0), v_b)
    for s in range(n_slabs):
        g = (2 * s) // group
        halves = []
        for p in range(2):
            h = 2 * s + p
            o = o_all[h * blk:(h + 1) * blk]
            halves.append(o if p == g else pltpu.roll(o, HEAD_DIM, 1))
        o_ref[0, :, s * LANES:(s + 1) * LANES] = jnp.where(out_lane_even, halves[0], halves[1])


def _t5_bucket(rel):
    nb = REL_BUCKETS // 2
    max_exact = nb // 2
    ret = jnp.where(rel > 0, nb, 0)
    n = jnp.abs(rel)
    large = max_exact + (jnp.log(jnp.maximum(n, 1).astype(F32) / max_exact)
                         / math.log(REL_MAX_DIST / max_exact) * (nb - max_exact)).astype(jnp.int32)
    large = jnp.minimum(large, nb - 1)
    return ret + jnp.where(n < max_exact, n, large)


BIAS_SPAN = 4 * BLOCK


def _bias_body(tab_ref, o_ref):
    blk = o_ref.shape[1]
    x = jnp.broadcast_to(tab_ref[0], (blk, BIAS_SPAN))
    shifted = pltpu.roll(x, BIAS_SPAN - (blk - 1), 1, stride=1, stride_axis=0)
    o_ref[0] = shifted[:, :3 * blk]


def _bias_table(rel_bias):
    rel = jnp.arange(BIAS_SPAN) - (2 * BLOCK - 1)
    tab = jnp.transpose(rel_bias[_t5_bucket(rel)].astype(F32))
    shape = (BLOCK, 3 * BLOCK)
    return pl.pallas_call(
        _bias_body,
        grid=(Q_HEADS,),
        in_specs=[pl.BlockSpec((1, 1, BIAS_SPAN), lambda h: (h, 0, 0))],
        out_specs=pl.BlockSpec((1,) + shape, lambda h: (h, 0, 0)),
        out_shape=jax.ShapeDtypeStruct((Q_HEADS,) + shape, F32),
        name="bias_table",
    )(tab.reshape(Q_HEADS, 1, BIAS_SPAN))


def _attention(q, kv, q_gain, k_gain, rel_bias, sink):
    b, t, _ = q.shape
    nb = t // BLOCK
    bias = _bias_table(rel_bias)
    cur = lambda bi, n: (bi, n, 0)
    prv = lambda bi, n: (bi, jnp.maximum(n - 1, 0), 0)
    nxt = lambda bi, n: (bi, jnp.minimum(n + 1, nb - 1), 0)
    c2 = lambda bi, n: (0, 0)
    c3 = lambda bi, n: (0, 0, 0)
    kvblk = (1, BLOCK, 2 * KV_COLS)
    return pl.pallas_call(
        _attn_body,
        grid=(b, nb),
        in_specs=[pl.BlockSpec((1, BLOCK, ATTN_WIDTH), cur), pl.BlockSpec(kvblk, prv), pl.BlockSpec(kvblk, cur),
                  pl.BlockSpec(kvblk, nxt), pl.BlockSpec((1, LANES), c2), pl.BlockSpec((1, LANES), c2),
                  pl.BlockSpec((Q_HEADS, BLOCK, 3 * BLOCK), c3), pl.BlockSpec(memory_space=pltpu.SMEM)],
        out_specs=pl.BlockSpec((1, BLOCK, ATTN_WIDTH), cur),
        out_shape=jax.ShapeDtypeStruct((b, t, ATTN_WIDTH), F32),
        compiler_params=pltpu.CompilerParams(dimension_semantics=("parallel", "parallel"),
                                             vmem_limit_bytes=VMEM_LIMIT),
        name="band_attn",
    )(q, kv, kv, kv, jnp.tile(q_gain, 2).reshape(1, LANES), jnp.tile(k_gain, 2).reshape(1, LANES),
      bias, sink.astype(F32))


def _outffn_body(*refs, tiles_per_seq, ff_chunk):
    main, prev, nxt = refs[0:6], refs[6:12], refs[12:18]
    lw_ref, lb_ref, wo_ref, gf_ref, wu_ref, cw_ref, cb_ref, wd_ref, o_ref, act_scr = refs[18:]
    i = pl.program_id(0)
    tile = main[0].shape[0]
    halo = prev[0].shape[0]
    d_ff = wd_ref.shape[0]
    c = RWKV_WIDTH
    first = (i % tiles_per_seq) == 0
    last = (i % tiles_per_seq) == tiles_per_seq - 1

    def mixed(x_ref, yf_ref, yb_ref, bonus_ref, gate_ref, attn_ref):
        y = yf_ref[...] + yb_ref[...]
        mu = _head_sum(y, 2) * (1.0 / HEAD_DIM)
        yc = y - mu
        var = _head_sum(yc * yc, 1) * (1.0 / HEAD_DIM)
        yn = yc * lax.rsqrt(var + LNX_EPS) * lw_ref[...] + lb_ref[...]
        mix_r = (yn + bonus_ref[...]) * gate_ref[...]
        return (x_ref[...] + _dot(mix_r.astype(BF16), wo_ref[:c, :])
                + _dot(attn_ref[...].astype(BF16), wo_ref[c:, :]))

    x1 = mixed(*main)
    x1cat = jnp.concatenate([mixed(*prev), x1, mixed(*nxt)], axis=0)
    total = tile + 2 * halo
    hcat = x1cat * lax.rsqrt(jnp.mean(x1cat * x1cat, axis=-1, keepdims=True) + NORM_EPS) * gf_ref[...]
    row = lax.broadcasted_iota(jnp.int32, (total, 1), 0)
    outside = (first & (row < halo)) | (last & (row >= halo + tile))
    hcat = jnp.where(outside, 0.0, hcat).astype(BF16)

    def conv(cols):
        u = _dot(hcat, wu_ref[:, cols])
        up = pltpu.roll(u, 1, 0)[halo:halo + tile]
        un = pltpu.roll(u, total - 1, 0)[halo:halo + tile]
        return (up * cw_ref[0:1, cols] + u[halo:halo + tile] * cw_ref[1:2, cols] + un * cw_ref[2:3, cols]
                + cb_ref[:, cols])

    for c in range(d_ff // ff_chunk):
        g = conv(slice(c * ff_chunk, (c + 1) * ff_chunk))
        val = conv(slice(d_ff + c * ff_chunk, d_ff + (c + 1) * ff_chunk))
        act_scr[:, c * ff_chunk:(c + 1) * ff_chunk] = (g * _sigmoid(g) * val).astype(BF16)
    o_ref[...] = x1 + _dot(act_scr[...], wd_ref[...])


def _out_ffn(x2, yf, yb, bonus, gate, attn, lnx_w, lnx_b, w_out, g_ffn, w_up, conv_w, conv_b, w_down,
             seq, tile, ff_chunk):
    rows, d = x2.shape
    c = RWKV_WIDTH
    d_ff = w_down.shape[0]
    halo = SUBLANES_F32
    per = tile // halo
    row = lambda i: (i, 0)
    prv = lambda i: (jnp.maximum(i * per - 1, 0), 0)
    nxt = lambda i: (jnp.minimum((i + 1) * per, rows // halo - 1), 0)
    const = lambda i: (0, 0)
    resident = dict(pipeline_mode=pl.Buffered(1))
    streams = (x2, yf, yb, bonus, gate, attn)

    def stream_specs(nrows, index_map):
        return [pl.BlockSpec((nrows, z.shape[1]), index_map) for z in streams]

    body = functools.partial(_outffn_body, tiles_per_seq=seq // tile, ff_chunk=ff_chunk)
    return pl.pallas_call(
        body,
        grid=(rows // tile,),
        in_specs=stream_specs(tile, row) + stream_specs(halo, prv) + stream_specs(halo, nxt)
                 + [pl.BlockSpec((1, c), const), pl.BlockSpec((1, c), const),
                    pl.BlockSpec(w_out.shape, const, **resident), pl.BlockSpec((1, d), const),
                    pl.BlockSpec((d, 2 * d_ff), const, **resident),
                    pl.BlockSpec((3, 2 * d_ff), const), pl.BlockSpec((1, 2 * d_ff), const),
                    pl.BlockSpec((d_ff, d), const, **resident)],
        out_specs=pl.BlockSpec((tile, d), row),
        out_shape=jax.ShapeDtypeStruct((rows, d), F32),
        scratch_shapes=[pltpu.VMEM((tile, d_ff), BF16)],
        compiler_params=pltpu.CompilerParams(dimension_semantics=("parallel",), vmem_limit_bytes=VMEM_LIMIT),
        name="out_ffn",
    )(*streams, *streams, *streams, lnx_w.reshape(1, c), lnx_b.reshape(1, c), w_out.astype(BF16),
      g_ffn.reshape(1, d), w_up.astype(BF16), conv_w, conv_b.reshape(1, 2 * d_ff), w_down.astype(BF16))


def _layer(x, g_mix, w_in, mu_prev, mu_next, w0, w2, a0, a2, g2, k_k, k_a, r_k, lnx_w, lnx_b,
           q_gain, k_gain, rel_bias, sink, w_out, g_ffn, w_up, conv_w, conv_b, w_down):
    b, t, d = x.shape
    rows = b * t
    x2 = x.reshape(rows, d)
    q, kv, *ops = _inproj_prep(x2, g_mix, w_in, mu_prev, mu_next, w0, w2, a0, a2, g2, k_k, k_a, r_k.reshape(-1),
                               seq=t, tile=min(256, t))
    r, v, kk, lw0, lw1, kd0, kd1, b0, b1 = (z.reshape(b, t, RWKV_WIDTH) for z in ops[:9])
    gate, bonus = ops[9:]
    yf, yb = _rwkv_scan(r, v, kk, lw0, lw1, kd0, kd1, b0, b1)
    attn = _attention(q.reshape(b, t, ATTN_WIDTH), kv.reshape(b, t, 2 * KV_COLS), q_gain, k_gain, rel_bias, sink)
    flat = lambda z: z.reshape(rows, z.shape[-1])
    out = _out_ffn(x2, flat(yf), flat(yb), bonus, gate, flat(attn), lnx_w, lnx_b, w_out, g_ffn,
                   w_up, conv_w, conv_b, w_down, seq=t, tile=min(512, t), ff_chunk=256)
    return out.reshape(b, t, d)


def kernel(x, g_mix, w_in, mu_prev, mu_next, w0, w2, a0, a2, g2, k_k, k_a, r_k, lnx_w, lnx_b, q_gain, k_gain,
           rel_bias, sink, w_out, g_ffn, w_up, conv_w, conv_b, w_down):
    depth = g_mix.shape[0]
    for l in range(depth):
        x = _layer(x, g_mix[l], w_in[l], mu_prev[l], mu_next[l], w0[l], w2[l], a0[l], a2[l], g2[l], k_k[l], k_a[l],
                   r_k[l], lnx_w[l], lnx_b[l], q_gain[l], k_gain[l], rel_bias, sink[l], w_out[l], g_ffn[l],
                   w_up[l], conv_w[l], conv_b[l], w_down[l])
    return x
```

```python
import functools
import math

import jax
import jax.numpy as jnp
from jax import lax
from jax.experimental import pallas as pl
from jax.experimental.pallas import tpu as pltpu

F32 = jnp.float32
BF16 = jnp.bfloat16

HEAD_DIM = 64
RWKV_WIDTH = 512
ATTN_WIDTH = 512
KV_HEADS = 2
Q_HEADS = 8
DECAY_LORA = 64
ICLR_LORA = 64
GATE_LORA = 128
RWKV_COLS = 3 * RWKV_WIDTH + DECAY_LORA + ICLR_LORA + GATE_LORA
KV_COLS = KV_HEADS * HEAD_DIM
WINDOW = 128
BLOCK = 128
REL_BUCKETS = 32
REL_MAX_DIST = 128
NORM_EPS = 1e-6
LNX_EPS = 64e-5
KK_EPS = 1e-12

LANES = 128
MXU_DIM = 256
SUBLANES_F32 = 8
SUBLANES_BF16 = 16
VMEM_LIMIT = 48 * 1024 * 1024

CHUNK = 64
PAIR = 2 * HEAD_DIM
assert PAIR == LANES
HEAD_SHIFT = HEAD_DIM.bit_length() - 1
assert 1 << HEAD_SHIFT == HEAD_DIM


def _dot(a, b, precision=None):
    return jnp.dot(a, b, preferred_element_type=F32, precision=precision)


def _dot_nt(a, b, precision=None):
    return lax.dot_general(a, b, (((1,), (1,)), ((), ())), preferred_element_type=F32, precision=precision)


def _dot_tn(a, b, precision=None):
    return lax.dot_general(a, b, (((0,), (0,)), ((), ())), preferred_element_type=F32, precision=precision)


def _sigmoid(x):
    return 1.0 / (1.0 + jnp.exp(-x))


def _split_bf16(x, parts):
    out = []
    for _ in range(parts):
        h = x.astype(BF16)
        out.append(h)
        x = x - h.astype(F32)
    return out


def _head_sum(x, parts):
    width = x.shape[-1]
    blk = min(width, MXU_DIM)
    r = lax.broadcasted_iota(jnp.int32, (blk, blk), 0) >> HEAD_SHIFT
    c = lax.broadcasted_iota(jnp.int32, (blk, blk), 1) >> HEAD_SHIFT
    m = (r == c).astype(BF16)
    pieces = _split_bf16(x, parts)
    cols = []
    for c0 in range(0, width, blk):
        acc = None
        for h in pieces:
            t = _dot(h[:, c0:c0 + blk], m)
            acc = t if acc is None else acc + t
        cols.append(acc)
    return cols[0] if len(cols) == 1 else jnp.concatenate(cols, axis=1)


def _inprep_body(x_ref, xp_ref, xn_ref, g_ref, wr_ref, wq_ref, wkv_ref,
                 mup_ref, mun_ref, w0_ref, w2_ref, a0_ref, a2_ref, g2_ref, kk_ref, ka_ref, rk_ref,
                 q_o, kv_o, r_o, v_o, kk_o, lw0_o, lw1_o, kd0_o, kd1_o, b0_o, b1_o, gate_o, bonus_o,
                 *, tiles_per_seq):
    i = pl.program_id(0)
    tile = x_ref.shape[0]
    halo = xp_ref.shape[0]
    first = (i % tiles_per_seq) == 0
    last = (i % tiles_per_seq) == tiles_per_seq - 1

    def norm(x):
        return (x * lax.rsqrt(jnp.mean(x * x, axis=-1, keepdims=True) + NORM_EPS) * g_ref[...]).astype(BF16)

    h = norm(x_ref[...])
    q_o[...] = _dot(h, wq_ref[...])
    kv_o[...] = _dot(h, wkv_ref[...])
    hp = norm(xp_ref[...])
    hn = norm(xn_ref[...])
    hp = jnp.where(first, jnp.zeros_like(hp), hp)
    hn = jnp.where(last, jnp.zeros_like(hn), hn)
    pcat = _dot(jnp.concatenate([hp, h, hn], axis=0), wr_ref[...])
    total = tile + 2 * halo
    p = pcat[halo:halo + tile]
    prev = pltpu.roll(pcat, 1, 0)[halo:halo + tile]
    nxt = pltpu.roll(pcat, total - 1, 0)[halo:halo + tile]
    pf = p + mup_ref[...] * (prev - p) + mun_ref[...] * (nxt - p)

    c = RWKV_WIDTH
    r, k, v = pf[:, :c], pf[:, c:2 * c], pf[:, 2 * c:3 * c]
    lora = pf[:, 3 * c:3 * c + LANES]
    xg = pf[:, 3 * c + LANES:]
    lora_t = jnp.tanh(lora).astype(BF16)
    lora_b = lora.astype(BF16)

    kx = k * kk_ref[...]
    kk = kx * lax.rsqrt(_head_sum(kx * kx, 1) + KK_EPS)
    gate = _dot(_sigmoid(xg).astype(BF16), g2_ref[...])

    kds = []
    for d, (lw_o, kd_o, b_o) in enumerate(((lw0_o, kd0_o, b0_o), (lw1_o, kd1_o, b1_o))):
        w_raw = w0_ref[d:d + 1, :] + _dot(lora_t, w2_ref[d])
        lw_o[...] = (-math.exp(-0.5)) * _sigmoid(w_raw)
        iclr = _sigmoid(a0_ref[d:d + 1, :] + _dot(lora_b, a2_ref[d]))
        kd = k * (1.0 + (iclr - 1.0) * ka_ref[...])
        kd_o[...] = kd
        b_o[...] = iclr * kk
        kds.append(kd)

    r_o[...] = r
    v_o[...] = v
    kk_o[...] = kk
    gate_o[...] = gate
    bonus_o[...] = _head_sum(r * (kds[0] + kds[1]) * rk_ref[...], 1) * v


def _inproj_prep(x2, g_mix, w_in, mu_prev, mu_next, w0, w2, a0, a2, g2, k_k, k_a, r_k, seq, tile):
    rows, d = x2.shape
    c = RWKV_WIDTH
    cols = RWKV_COLS
    halo = SUBLANES_BF16
    per = tile // halo
    wb = w_in.astype(BF16)
    wr, wq, wkv = wb[:, :cols], wb[:, cols:cols + ATTN_WIDTH], wb[:, cols + ATTN_WIDTH:]
    zeros = jnp.zeros((2, DECAY_LORA, c), F32)
    w2p = jnp.concatenate([w2, zeros], axis=1).astype(BF16)
    a2p = jnp.concatenate([zeros, a2], axis=1).astype(BF16)
    row = lambda i: (i, 0)
    prv = lambda i: (jnp.maximum(i * per - 1, 0), 0)
    nxt = lambda i: (jnp.minimum((i + 1) * per, rows // halo - 1), 0)
    c2 = lambda i: (0, 0)
    c3 = lambda i: (0, 0, 0)
    resident = dict(pipeline_mode=pl.Buffered(1))
    out = jax.ShapeDtypeStruct((rows, c), F32)
    body = functools.partial(_inprep_body, tiles_per_seq=seq // tile)
    return pl.pallas_call(
        body,
        grid=(rows // tile,),
        in_specs=[pl.BlockSpec((tile, d), row), pl.BlockSpec((halo, d), prv), pl.BlockSpec((halo, d), nxt),
                  pl.BlockSpec((1, d), c2),
                  pl.BlockSpec(wr.shape, c2, **resident), pl.BlockSpec(wq.shape, c2, **resident),
                  pl.BlockSpec(wkv.shape, c2, **resident),
                  pl.BlockSpec((1, cols), c2), pl.BlockSpec((1, cols), c2),
                  pl.BlockSpec((2, c), c2), pl.BlockSpec((2, LANES, c), c3),
                  pl.BlockSpec((2, c), c2), pl.BlockSpec((2, LANES, c), c3),
                  pl.BlockSpec((GATE_LORA, c), c2),
                  pl.BlockSpec((1, c), c2), pl.BlockSpec((1, c), c2), pl.BlockSpec((1, c), c2)],
        out_specs=[pl.BlockSpec((tile, ATTN_WIDTH), row), pl.BlockSpec((tile, 2 * KV_COLS), row)]
                  + [pl.BlockSpec((tile, c), row)] * 11,
        out_shape=[jax.ShapeDtypeStruct((rows, ATTN_WIDTH), F32), jax.ShapeDtypeStruct((rows, 2 * KV_COLS), F32)]
                  + [out] * 11,
        compiler_params=pltpu.CompilerParams(dimension_semantics=("parallel",), vmem_limit_bytes=VMEM_LIMIT),
        name="inproj_prep",
    )(x2, x2, x2, g_mix.reshape(1, d), wr, wq, wkv, mu_prev.reshape(1, cols), mu_next.reshape(1, cols),
      w0, w2p, a0, a2p, g2.astype(BF16), k_k.reshape(1, c), k_a.reshape(1, c), r_k.reshape(1, c))


def _scan_step(j, rf, vf, kkf, lwf, kdf, bf, rb, vb, kkb, lwb, kdb, bb, yf_o, yb_o, h_scr, filler=iter(())):
    fill = lambda: next(filler, None)

    @pl.when(j == 0)
    def _():
        h_scr[...] = jnp.zeros_like(h_scr)

    n_batch = rf.shape[0]
    cs = rf.shape[1]
    n_pairs = rf.shape[2] // PAIR

    cs_shift = cs.bit_length() - 1
    assert 1 << cs_shift == cs
    t_row = lax.broadcasted_iota(jnp.int32, (cs, 2 * cs), 0)
    t_col = lax.broadcasted_iota(jnp.int32, (cs, 2 * cs), 1) & (cs - 1)
    eye_cat = (t_row == t_col).astype(F32)
    ri = lax.broadcasted_iota(jnp.int32, (PAIR, PAIR), 0)
    ci = lax.broadcasted_iota(jnp.int32, (PAIR, PAIR), 1)
    eye = (ri == ci).astype(F32)
    same_head = (ri >> HEAD_SHIFT) == (ci >> HEAD_SHIFT)
    tr = lax.broadcasted_iota(jnp.int32, (cs, cs), 0)
    tc = lax.broadcasted_iota(jnp.int32, (cs, cs), 1)
    lane_even = lax.broadcasted_iota(jnp.int32, (cs, PAIR), 1) < HEAD_DIM
    col_first = lax.broadcasted_iota(jnp.int32, (cs, 2 * cs), 1) < cs

    def stack(x):
        zero = jnp.zeros_like(x)
        return jnp.concatenate([jnp.where(lane_even, x, zero), jnp.where(lane_even, zero, x)], axis=0)

    def block_diag(m):
        zero = jnp.zeros_like(m)
        return jnp.concatenate([jnp.where(col_first, m, zero), jnp.where(col_first, zero, m)], axis=0)

    dirs = ((0, rf, vf, kkf, lwf, kdf, bf, yf_o), (1, rb, vb, kkb, lwb, kdb, bb, yb_o))
    chains = []
    for bi, (d, r_ref, v_ref, kk_ref, lw_ref, kd_ref, b_ref, y_o) in (
            (bi, dr) for bi in range(n_batch) for dr in dirs):
        fwd = d == 0
        strict = (t_col < t_row) if fwd else (t_col > t_row)
        incl = strict | (t_col == t_row)
        cum = ((tc <= tr) if fwd else (tc >= tr)).astype(BF16)

        lw = lw_ref[bi]
        c_in = sum(_dot(cum, piece) for piece in _split_bf16(lw, 3))
        c_ex = c_in - lw
        c_tot = jnp.sum(lw, axis=0, keepdims=True)
        e_neg = jnp.exp(-c_in)
        e_end = jnp.exp(c_tot - c_in)
        gam = jnp.exp(c_tot)
        kd = kd_ref[bi]
        bv = b_ref[bi]
        a_t = -kk_ref[bi] * jnp.exp(c_ex)
        r_t = r_ref[bi] * jnp.exp(c_in)
        b_t = bv * e_neg
        k_t = kd * e_neg
        b_h = bv * e_end
        k_h = kd * e_end
        vv = v_ref[bi]
        for pr in range(n_pairs):
            sl = slice(pr * PAIR, (pr + 1) * PAIR)
            a_b, v_b = a_t[:, sl].astype(BF16), vv[:, sl].astype(BF16)
            chains.append(dict(
                bi=bi, d=d, pr=pr, sl=sl, y_o=y_o, strict=strict, incl=incl, r_t=r_t[:, sl], gam=gam[:, sl],
                a_b=a_b, r_b=r_t[:, sl].astype(BF16), a_sb=stack(a_b), v_b=v_b, v_sb=stack(v_b),
                bt_sb=stack(b_t[:, sl].astype(BF16)), kt_sb=stack(k_t[:, sl].astype(BF16)),
                bk_h=jnp.concatenate([b_h[:, sl].astype(BF16), k_h[:, sl].astype(BF16)], axis=0)))

    h2 = 2 * cs
    for c in chains:
        lhs = jnp.concatenate([c["a_b"], c["r_b"]], axis=0)
        rhs = jnp.concatenate([c["bt_sb"], c["kt_sb"]], axis=0)
        gram = _dot_nt(lhs, rhs)
        a_ab = jnp.where(c["strict"], gram[:cs, :h2], 0.0)
        c["a_xk"] = jnp.concatenate([jnp.where(c["strict"], gram[:cs, h2:], 0.0),
                                     jnp.where(c["incl"], gram[cs:, h2:], 0.0)], axis=0).astype(BF16)
        c["a_rb"] = jnp.where(c["incl"], gram[cs:, :h2], 0.0).astype(BF16)
        c["x_cat"] = a_ab.astype(BF16)
        c["s_cat"] = eye_cat + a_ab
    fill()

    for c in chains:
        c["x_cat"] = _dot(c["x_cat"], block_diag(c["x_cat"])).astype(BF16)
    fill()
    for _ in range(cs_shift - 2):
        for c in chains:
            sx = _dot(jnp.concatenate([c["s_cat"].astype(BF16), c["x_cat"]], axis=0), block_diag(c["x_cat"]))
            c["s_cat"] = c["s_cat"] + sx[:cs]
            c["x_cat"] = sx[cs:].astype(BF16)
        fill()
    for c in chains:
        c["s_cat"] = (c["s_cat"] + _dot(c["s_cat"].astype(BF16), block_diag(c["x_cat"]))).astype(BF16)
    fill()

    zero_b = jnp.zeros((cs, PAIR), BF16)
    for c in chains:
        av = _dot(c["a_xk"], c["v_sb"])
        c["w1_sb"] = stack(av[:cs].astype(BF16))
        c["av"] = av[cs:]
        c["h0"] = h_scr[c["bi"], c["d"], c["pr"]]
        c["h_b"] = c["h0"].astype(BF16)
        c["gam_col"] = jnp.sum(eye * c["gam"], axis=1, keepdims=True)
    for c in chains:
        c["pq"] = _dot(c["s_cat"], jnp.concatenate([c["a_sb"], c["w1_sb"]], axis=1)).astype(BF16)
    for c in chains:
        pq = c["pq"]
        ry = _dot(c["a_rb"], jnp.concatenate([stack(pq[:, :PAIR]), stack(pq[:, PAIR:])], axis=1))
        c["r_hat"] = (c["r_t"] + ry[:, :PAIR]).astype(BF16)
        c["y_hat"] = ry[:, PAIR:] + c["av"]
        rhs = jnp.concatenate([pq, jnp.concatenate([zero_b, c["v_b"]], axis=1)], axis=0)
        gd = _dot_tn(c["bk_h"], rhs)
        c["btp"] = jnp.where(same_head, gd[:, :PAIR], 0.0).astype(BF16)
        c["dd"] = jnp.where(same_head, gd[:, PAIR:], 0.0)
    for c in chains:
        yh = _dot(jnp.concatenate([c["r_hat"], c["btp"]], axis=0), c["h_b"])
        c["y_o"][c["bi"], :, c["sl"]] = yh[:cs] + c["y_hat"]
        h_scr[c["bi"], c["d"], c["pr"]] = c["gam_col"] * c["h0"] + yh[cs:] + c["dd"]
    for _ in filler:
        pass


def _attn_stages(n, nb, q_ref, kvp_ref, kvc_ref, kvn_ref, qg_ref, kg_ref, bias_ref, sink_ref, o_ref):
    blk = q_ref.shape[1]
    head_mean = lambda z: _head_sum(z, 1) * (1.0 / HEAD_DIM)

    k_win = jnp.concatenate([kvp_ref[0, :, :KV_COLS], kvc_ref[0, :, :KV_COLS], kvn_ref[0, :, :KV_COLS]], axis=0)
    v_win = jnp.concatenate([kvp_ref[0, :, KV_COLS:], kvc_ref[0, :, KV_COLS:], kvn_ref[0, :, KV_COLS:]], axis=0)
    kn = k_win * lax.rsqrt(head_mean(k_win * k_win) + NORM_EPS) * kg_ref[...]
    v_b = v_win.astype(BF16)

    lane = lax.broadcasted_iota(jnp.int32, kn.shape, 1)
    k_at = []
    for g in range(KV_HEADS):
        own = jnp.where((lane >> HEAD_SHIFT) == g, kn, 0.0)
        other = pltpu.roll(own, HEAD_DIM, 1)
        k_at.append([own if p == g else other for p in range(2)])

    row = lax.broadcasted_iota(jnp.int32, (blk, 3 * blk), 0)
    col = lax.broadcasted_iota(jnp.int32, (blk, 3 * blk), 1)
    rel = col - blk - row
    valid = (jnp.abs(rel) <= WINDOW)
    valid &= (col >= blk) | (n > 0)
    valid &= (col < 2 * blk) | (n < nb - 1)
    out_lane_even = lax.broadcasted_iota(jnp.int32, (blk, LANES), 1) < HEAD_DIM

    group = Q_HEADS // KV_HEADS
    n_slabs = ATTN_WIDTH // LANES
    slabs_per_group = n_slabs // KV_HEADS
    qn = []
    for s in range(n_slabs):
        q = q_ref[0, :, s * LANES:(s + 1) * LANES]
        qn.append((q * lax.rsqrt(head_mean(q * q) + NORM_EPS) * qg_ref[...] * (HEAD_DIM ** -0.5)).astype(BF16))
    scores = []
    for g in range(KV_HEADS):
        kcat = jnp.concatenate([k_at[g][0], k_at[g][1]], axis=0).astype(BF16)
        q_g = jnp.concatenate(qn[g * slabs_per_group:(g + 1) * slabs_per_group], axis=0)
        scores.append(_dot_nt(q_g, kcat))
    yield
    probs = []
    for s in range(n_slabs):
        heads = (2 * s, 2 * s + 1)
        g, sg = s // slabs_per_group, s % slabs_per_group
        sc = [jnp.where(valid, scores[g][sg * blk:(sg + 1) * blk, p * 3 * blk:(p + 1) * 3 * blk] + bias_ref[h],
                        -jnp.inf) for p, h in enumerate(heads)]
        sinks = [sink_ref[h] for h in heads]
        m = [jnp.maximum(jnp.max(z, axis=-1, keepdims=True), sk) for z, sk in zip(sc, sinks)]
        e = [jnp.exp(z - mm) for z, mm in zip(sc, m)]
        denom = [jnp.sum(ee, axis=-1, keepdims=True) + jnp.exp(sk - mm) for ee, sk, mm in zip(e, sinks, m)]
        probs += [(ee / dd).astype(BF16) for ee, dd in zip(e, denom)]
        yield
    o_all = _dot(jnp.concatenate(probs, axis=0), v_b)
    for s in range(n_slabs):
        g = (2 * s) // group
        halves = []
        for p in range(2):
            h = 2 * s + p
            o = o_all[h * blk:(h + 1) * blk]
            halves.append(o if p == g else pltpu.roll(o, HEAD_DIM, 1))
        o_ref[0, :, s * LANES:(s + 1) * LANES] = jnp.where(out_lane_even, halves[0], halves[1])


def _t5_bucket(rel):
    nb = REL_BUCKETS // 2
    max_exact = nb // 2
    ret = jnp.where(rel > 0, nb, 0)
    n = jnp.abs(rel)
    large = max_exact + (jnp.log(jnp.maximum(n, 1).astype(F32) / max_exact)
                         / math.log(REL_MAX_DIST / max_exact) * (nb - max_exact)).astype(jnp.int32)
    large = jnp.minimum(large, nb - 1)
    return ret + jnp.where(n < max_exact, n, large)


BIAS_SPAN = 4 * BLOCK


def _bias_body(tab_ref, o_ref):
    blk = o_ref.shape[1]
    x = jnp.broadcast_to(tab_ref[0], (blk, BIAS_SPAN))
    shifted = pltpu.roll(x, BIAS_SPAN - (blk - 1), 1, stride=1, stride_axis=0)
    o_ref[0] = shifted[:, :3 * blk]


def _bias_table(rel_bias):
    rel = jnp.arange(BIAS_SPAN) - (2 * BLOCK - 1)
    tab = jnp.transpose(rel_bias[_t5_bucket(rel)].astype(F32))
    shape = (BLOCK, 3 * BLOCK)
    return pl.pallas_call(
        _bias_body,
        grid=(Q_HEADS,),
        in_specs=[pl.BlockSpec((1, 1, BIAS_SPAN), lambda h: (h, 0, 0))],
        out_specs=pl.BlockSpec((1,) + shape, lambda h: (h, 0, 0)),
        out_shape=jax.ShapeDtypeStruct((Q_HEADS,) + shape, F32),
        name="bias_table",
    )(tab.reshape(Q_HEADS, 1, BIAS_SPAN))


def _mixers_body(*refs, n_blocks):
    scan_refs, attn_refs = refs[:12], refs[12:20]
    yf_o, yb_o, attn_o, h_scr = refs[20:]
    j = pl.program_id(0)
    _scan_step(j, *scan_refs, yf_o, yb_o, h_scr, filler=_attn_stages(j % n_blocks, n_blocks, *attn_refs, attn_o))


def _mixers(r, v, kk, lw0, lw1, kd0, kd1, b0, b1, q, kv, q_gain, k_gain, rel_bias, sink):
    b, t, c = r.shape
    nc = t // CHUNK
    nb = t // BLOCK
    assert b * nb == nc
    fw = lambda j: (0, j, 0)
    bw = lambda j: (0, nc - 1 - j, 0)
    blk = (b, CHUNK, c)
    cur = lambda j: (j // nb, j % nb, 0)
    prv = lambda j: (j // nb, jnp.maximum(j % nb - 1, 0), 0)
    nxt = lambda j: (j // nb, jnp.minimum(j % nb + 1, nb - 1), 0)
    c2 = lambda j: (0, 0)
    c3 = lambda j: (0, 0, 0)
    kvblk = (1, BLOCK, 2 * KV_COLS)
    return pl.pallas_call(
        functools.partial(_mixers_body, n_blocks=nb),
        grid=(nc,),
        in_specs=[pl.BlockSpec(blk, fw)] * 6 + [pl.BlockSpec(blk, bw)] * 6
                 + [pl.BlockSpec((1, BLOCK, ATTN_WIDTH), cur), pl.BlockSpec(kvblk, prv), pl.BlockSpec(kvblk, cur),
                    pl.BlockSpec(kvblk, nxt), pl.BlockSpec((1, LANES), c2), pl.BlockSpec((1, LANES), c2),
                    pl.BlockSpec((Q_HEADS, BLOCK, 3 * BLOCK), c3), pl.BlockSpec(memory_space=pltpu.SMEM)],
        out_specs=[pl.BlockSpec(blk, fw), pl.BlockSpec(blk, bw), pl.BlockSpec((1, BLOCK, ATTN_WIDTH), cur)],
        out_shape=[jax.ShapeDtypeStruct((b, t, c), F32)] * 2 + [jax.ShapeDtypeStruct((b, t, ATTN_WIDTH), F32)],
        scratch_shapes=[pltpu.VMEM((b, 2, c // PAIR, PAIR, PAIR), F32)],
        compiler_params=pltpu.CompilerParams(dimension_semantics=("arbitrary",), vmem_limit_bytes=VMEM_LIMIT),
        name="mixers",
    )(r, v, kk, lw0, kd0, b0, r, v, kk, lw1, kd1, b1,
      q, kv, kv, kv, jnp.tile(q_gain, 2).reshape(1, LANES), jnp.tile(k_gain, 2).reshape(1, LANES),
      _bias_table(rel_bias), sink.astype(F32))


def _outffn_body(*refs, tiles_per_seq, ff_chunk):
    main, prev, nxt = refs[0:6], refs[6:12], refs[12:18]
    lw_ref, lb_ref, wo_ref, gf_ref, wu_ref, cw_ref, cb_ref, wd_ref, o_ref, act_scr = refs[18:]
    i = pl.program_id(0)
    tile = main[0].shape[0]
    halo = prev[0].shape[0]
    d_ff = wd_ref.shape[0]
    c = RWKV_WIDTH
    first = (i % tiles_per_seq) == 0
    last = (i % tiles_per_seq) == tiles_per_seq - 1

    def mixed(x_ref, yf_ref, yb_ref, bonus_ref, gate_ref, attn_ref):
        y = yf_ref[...] + yb_ref[...]
        mu = _head_sum(y, 2) * (1.0 / HEAD_DIM)
        yc = y - mu
        var = _head_sum(yc * yc, 1) * (1.0 / HEAD_DIM)
        yn = yc * lax.rsqrt(var + LNX_EPS) * lw_ref[...] + lb_ref[...]
        mix_r = (yn + bonus_ref[...]) * gate_ref[...]
        return (x_ref[...] + _dot(mix_r.astype(BF16), wo_ref[:c, :])
                + _dot(attn_ref[...].astype(BF16), wo_ref[c:, :]))

    x1 = mixed(*main)
    x1cat = jnp.concatenate([mixed(*prev), x1, mixed(*nxt)], axis=0)
    total = tile + 2 * halo
    hcat = x1cat * lax.rsqrt(jnp.mean(x1cat * x1cat, axis=-1, keepdims=True) + NORM_EPS) * gf_ref[...]
    row = lax.broadcasted_iota(jnp.int32, (total, 1), 0)
    outside = (first & (row < halo)) | (last & (row >= halo + tile))
    hcat = jnp.where(outside, 0.0, hcat).astype(BF16)

    def conv(cols):
        u = _dot(hcat, wu_ref[:, cols])
        up = pltpu.roll(u, 1, 0)[halo:halo + tile]
        un = pltpu.roll(u, total - 1, 0)[halo:halo + tile]
        return (up * cw_ref[0:1, cols] + u[halo:halo + tile] * cw_ref[1:2, cols] + un * cw_ref[2:3, cols]
                + cb_ref[:, cols])

    for c in range(d_ff // ff_chunk):
        g = conv(slice(c * ff_chunk, (c + 1) * ff_chunk))
        val = conv(slice(d_ff + c * ff_chunk, d_ff + (c + 1) * ff_chunk))
        act_scr[:, c * ff_chunk:(c + 1) * ff_chunk] = (g * _sigmoid(g) * val).astype(BF16)
    o_ref[...] = x1 + _dot(act_scr[...], wd_ref[...])


def _out_ffn(x2, yf, yb, bonus, gate, attn, lnx_w, lnx_b, w_out, g_ffn, w_up, conv_w, conv_b, w_down,
             seq, tile, ff_chunk):
    rows, d = x2.shape
    c = RWKV_WIDTH
    d_ff = w_down.shape[0]
    halo = SUBLANES_F32
    per = tile // halo
    row = lambda i: (i, 0)
    prv = lambda i: (jnp.maximum(i * per - 1, 0), 0)
    nxt = lambda i: (jnp.minimum((i + 1) * per, rows // halo - 1), 0)
    const = lambda i: (0, 0)
    resident = dict(pipeline_mode=pl.Buffered(1))
    streams = (x2, yf, yb, bonus, gate, attn)

    def stream_specs(nrows, index_map):
        return [pl.BlockSpec((nrows, z.shape[1]), index_map) for z in streams]

    body = functools.partial(_outffn_body, tiles_per_seq=seq // tile, ff_chunk=ff_chunk)
    return pl.pallas_call(
        body,
        grid=(rows // tile,),
        in_specs=stream_specs(tile, row) + stream_specs(halo, prv) + stream_specs(halo, nxt)
                 + [pl.BlockSpec((1, c), const), pl.BlockSpec((1, c), const),
                    pl.BlockSpec(w_out.shape, const, **resident), pl.BlockSpec((1, d), const),
                    pl.BlockSpec((d, 2 * d_ff), const, **resident),
                    pl.BlockSpec((3, 2 * d_ff), const), pl.BlockSpec((1, 2 * d_ff), const),
                    pl.BlockSpec((d_ff, d), const, **resident)],
        out_specs=pl.BlockSpec((tile, d), row),
        out_shape=jax.ShapeDtypeStruct((rows, d), F32),
        scratch_shapes=[pltpu.VMEM((tile, d_ff), BF16)],
        compiler_params=pltpu.CompilerParams(dimension_semantics=("parallel",), vmem_limit_bytes=VMEM_LIMIT),
        name="out_ffn",
    )(*streams, *streams, *streams, lnx_w.reshape(1, c), lnx_b.reshape(1, c), w_out.astype(BF16),
      g_ffn.reshape(1, d), w_up.astype(BF16), conv_w, conv_b.reshape(1, 2 * d_ff), w_down.astype(BF16))


def _layer(x, g_mix, w_in, mu_prev, mu_next, w0, w2, a0, a2, g2, k_k, k_a, r_k, lnx_w, lnx_b,
           q_gain, k_gain, rel_bias, sink, w_out, g_ffn, w_up, conv_w, conv_b, w_down):
    b, t, d = x.shape
    rows = b * t
    x2 = x.reshape(rows, d)
    q, kv, *ops = _inproj_prep(x2, g_mix, w_in, mu_prev, mu_next, w0, w2, a0, a2, g2, k_k, k_a, r_k.reshape(-1),
                               seq=t, tile=min(256, t))
    r, v, kk, lw0, lw1, kd0, kd1, b0, b1 = (z.reshape(b, t, RWKV_WIDTH) for z in ops[:9])
    gate, bonus = ops[9:]
    yf, yb, attn = _mixers(r, v, kk, lw0, lw1, kd0, kd1, b0, b1, q.reshape(b, t, ATTN_WIDTH),
                           kv.reshape(b, t, 2 * KV_COLS), q_gain, k_gain, rel_bias, sink)
    flat = lambda z: z.reshape(rows, z.shape[-1])
    out = _out_ffn(x2, flat(yf), flat(yb), bonus, gate, flat(attn), lnx_w, lnx_b, w_out, g_ffn,
                   w_up, conv_w, conv_b, w_down, seq=t, tile=min(512, t), ff_chunk=256)
    return out.reshape(b, t, d)


def kernel(x, g_mix, w_in, mu_prev, mu_next, w0, w2, a0, a2, g2, k_k, k_a, r_k, lnx_w, lnx_b, q_gain, k_gain,
           rel_bias, sink, w_out, g_ffn, w_up, conv_w, conv_b, w_down):
    depth = g_mix.shape[0]
    for l in range(depth):
        x = _layer(x, g_mix[l], w_in[l], mu_prev[l], mu_next[l], w0[l], w2[l], a0[l], a2[l], g2[l], k_k[l], k_a[l],
                   r_k[l], lnx_w[l], lnx_b[l], q_gain[l], k_gain[l], rel_bias, sink[l], w_out[l], g_ffn[l],
                   w_up[l], conv_w[l], conv_b[l], w_down[l])
    return x
```

```python
import functools
import math

import jax
import jax.numpy as jnp
from jax import lax
from jax.experimental import pallas as pl
from jax.experimental.pallas import tpu as pltpu

F32 = jnp.float32
BF16 = jnp.bfloat16

HEAD_DIM = 64
RWKV_WIDTH = 512
ATTN_WIDTH = 512
KV_HEADS = 2
Q_HEADS = 8
DECAY_LORA = 64
ICLR_LORA = 64
GATE_LORA = 128
RWKV_COLS = 3 * RWKV_WIDTH + DECAY_LORA + ICLR_LORA + GATE_LORA
KV_COLS = KV_HEADS * HEAD_DIM
WINDOW = 128
BLOCK = 128
REL_BUCKETS = 32
REL_MAX_DIST = 128
NORM_EPS = 1e-6
LNX_EPS = 64e-5
KK_EPS = 1e-12

LANES = 128
MXU_DIM = 256
SUBLANES_F32 = 8
SUBLANES_BF16 = 16
VMEM_LIMIT = 48 * 1024 * 1024

CHUNK = 64
CHUNKS_PER_STEP = 2
PAIR = 2 * HEAD_DIM
assert PAIR == LANES
HEAD_SHIFT = HEAD_DIM.bit_length() - 1
assert 1 << HEAD_SHIFT == HEAD_DIM


def _dot(a, b, precision=None):
    return jnp.dot(a, b, preferred_element_type=F32, precision=precision)


def _dot_nt(a, b, precision=None):
    return lax.dot_general(a, b, (((1,), (1,)), ((), ())), preferred_element_type=F32, precision=precision)


def _dot_tn(a, b, precision=None):
    return lax.dot_general(a, b, (((0,), (0,)), ((), ())), preferred_element_type=F32, precision=precision)


def _sigmoid(x):
    return 1.0 / (1.0 + jnp.exp(-x))


def _split_bf16(x, parts):
    out = []
    for _ in range(parts):
        h = x.astype(BF16)
        out.append(h)
        x = x - h.astype(F32)
    return out


def _head_sum(x, parts):
    width = x.shape[-1]
    blk = min(width, MXU_DIM)
    r = lax.broadcasted_iota(jnp.int32, (blk, blk), 0) >> HEAD_SHIFT
    c = lax.broadcasted_iota(jnp.int32, (blk, blk), 1) >> HEAD_SHIFT
    m = (r == c).astype(BF16)
    pieces = _split_bf16(x, parts)
    cols = []
    for c0 in range(0, width, blk):
        acc = None
        for h in pieces:
            t = _dot(h[:, c0:c0 + blk], m)
            acc = t if acc is None else acc + t
        cols.append(acc)
    return cols[0] if len(cols) == 1 else jnp.concatenate(cols, axis=1)


def _inprep_body(x_ref, xp_ref, xn_ref, g_ref, wr_ref, wq_ref, wkv_ref,
                 mup_ref, mun_ref, w0_ref, w2_ref, a0_ref, a2_ref, g2_ref, kk_ref, ka_ref, rk_ref,
                 q_o, kv_o, r_o, v_o, kk_o, lw0_o, lw1_o, kd0_o, kd1_o, b0_o, b1_o, gate_o, bonus_o,
                 *, tiles_per_seq):
    i = pl.program_id(0)
    tile = x_ref.shape[0]
    halo = xp_ref.shape[0]
    first = (i % tiles_per_seq) == 0
    last = (i % tiles_per_seq) == tiles_per_seq - 1

    def norm(x):
        return (x * lax.rsqrt(jnp.mean(x * x, axis=-1, keepdims=True) + NORM_EPS) * g_ref[...]).astype(BF16)

    h = norm(x_ref[...])
    q_o[...] = _dot(h, wq_ref[...])
    kv_o[...] = _dot(h, wkv_ref[...])
    hp = norm(xp_ref[...])
    hn = norm(xn_ref[...])
    hp = jnp.where(first, jnp.zeros_like(hp), hp)
    hn = jnp.where(last, jnp.zeros_like(hn), hn)
    pcat = _dot(jnp.concatenate([hp, h, hn], axis=0), wr_ref[...])
    total = tile + 2 * halo
    p = pcat[halo:halo + tile]
    prev = pltpu.roll(pcat, 1, 0)[halo:halo + tile]
    nxt = pltpu.roll(pcat, total - 1, 0)[halo:halo + tile]
    pf = p + mup_ref[...] * (prev - p) + mun_ref[...] * (nxt - p)

    c = RWKV_WIDTH
    r, k, v = pf[:, :c], pf[:, c:2 * c], pf[:, 2 * c:3 * c]
    lora = pf[:, 3 * c:3 * c + LANES]
    xg = pf[:, 3 * c + LANES:]
    lora_t = jnp.tanh(lora).astype(BF16)
    lora_b = lora.astype(BF16)

    kx = k * kk_ref[...]
    kk = kx * lax.rsqrt(_head_sum(kx * kx, 1) + KK_EPS)
    gate = _dot(_sigmoid(xg).astype(BF16), g2_ref[...])

    kds = []
    for d, (lw_o, kd_o, b_o) in enumerate(((lw0_o, kd0_o, b0_o), (lw1_o, kd1_o, b1_o))):
        w_raw = w0_ref[d:d + 1, :] + _dot(lora_t, w2_ref[d])
        lw_o[...] = (-math.exp(-0.5)) * _sigmoid(w_raw)
        iclr = _sigmoid(a0_ref[d:d + 1, :] + _dot(lora_b, a2_ref[d]))
        kd = k * (1.0 + (iclr - 1.0) * ka_ref[...])
        kd_o[...] = kd
        b_o[...] = iclr * kk
        kds.append(kd)

    r_o[...] = r
    v_o[...] = v
    kk_o[...] = kk
    gate_o[...] = gate
    bonus_o[...] = _head_sum(r * (kds[0] + kds[1]) * rk_ref[...], 1) * v


def _inproj_prep(x2, g_mix, w_in, mu_prev, mu_next, w0, w2, a0, a2, g2, k_k, k_a, r_k, seq, tile):
    rows, d = x2.shape
    c = RWKV_WIDTH
    cols = RWKV_COLS
    halo = SUBLANES_BF16
    per = tile // halo
    wb = w_in.astype(BF16)
    wr, wq, wkv = wb[:, :cols], wb[:, cols:cols + ATTN_WIDTH], wb[:, cols + ATTN_WIDTH:]
    zeros = jnp.zeros((2, DECAY_LORA, c), F32)
    w2p = jnp.concatenate([w2, zeros], axis=1).astype(BF16)
    a2p = jnp.concatenate([zeros, a2], axis=1).astype(BF16)
    row = lambda i: (i, 0)
    prv = lambda i: (jnp.maximum(i * per - 1, 0), 0)
    nxt = lambda i: (jnp.minimum((i + 1) * per, rows // halo - 1), 0)
    c2 = lambda i: (0, 0)
    c3 = lambda i: (0, 0, 0)
    resident = dict(pipeline_mode=pl.Buffered(1))
    out = jax.ShapeDtypeStruct((rows, c), F32)
    body = functools.partial(_inprep_body, tiles_per_seq=seq // tile)
    return pl.pallas_call(
        body,
        grid=(rows // tile,),
        in_specs=[pl.BlockSpec((tile, d), row), pl.BlockSpec((halo, d), prv), pl.BlockSpec((halo, d), nxt),
                  pl.BlockSpec((1, d), c2),
                  pl.BlockSpec(wr.shape, c2, **resident), pl.BlockSpec(wq.shape, c2, **resident),
                  pl.BlockSpec(wkv.shape, c2, **resident),
                  pl.BlockSpec((1, cols), c2), pl.BlockSpec((1, cols), c2),
                  pl.BlockSpec((2, c), c2), pl.BlockSpec((2, LANES, c), c3),
                  pl.BlockSpec((2, c), c2), pl.BlockSpec((2, LANES, c), c3),
                  pl.BlockSpec((GATE_LORA, c), c2),
                  pl.BlockSpec((1, c), c2), pl.BlockSpec((1, c), c2), pl.BlockSpec((1, c), c2)],
        out_specs=[pl.BlockSpec((tile, ATTN_WIDTH), row), pl.BlockSpec((tile, 2 * KV_COLS), row)]
                  + [pl.BlockSpec((tile, c), row)] * 11,
        out_shape=[jax.ShapeDtypeStruct((rows, ATTN_WIDTH), F32), jax.ShapeDtypeStruct((rows, 2 * KV_COLS), F32)]
                  + [out] * 11,
        compiler_params=pltpu.CompilerParams(dimension_semantics=("parallel",), vmem_limit_bytes=VMEM_LIMIT),
        name="inproj_prep",
    )(x2, x2, x2, g_mix.reshape(1, d), wr, wq, wkv, mu_prev.reshape(1, cols), mu_next.reshape(1, cols),
      w0, w2p, a0, a2p, g2.astype(BF16), k_k.reshape(1, c), k_a.reshape(1, c), r_k.reshape(1, c))


def _scan_step(j, *refs, filler=iter(())):
    h_scr = refs[-1]

    @pl.when(j == 0)
    def _():
        h_scr[...] = jnp.zeros_like(h_scr)

    n_batch, rows = refs[0].shape[0], refs[0].shape[1]
    members = [(bi, d, sub) for bi in range(n_batch) for d in range(2) for sub in range(rows // CHUNK)]
    for _ in _scan_stages(members, *refs):
        next(filler, None)
    for _ in filler:
        pass


def _scan_stages(members, rf, vf, kkf, lwf, kdf, bf, rb, vb, kkb, lwb, kdb, bb, yf_o, yb_o, h_scr):
    cs = CHUNK
    subs = rf.shape[1] // cs
    n_pairs = rf.shape[2] // PAIR

    cs_shift = cs.bit_length() - 1
    assert 1 << cs_shift == cs
    t_row = lax.broadcasted_iota(jnp.int32, (cs, 2 * cs), 0)
    t_col = lax.broadcasted_iota(jnp.int32, (cs, 2 * cs), 1) & (cs - 1)
    eye_cat = (t_row == t_col).astype(F32)
    same_block = lambda log2_size: (t_row >> log2_size) == (t_col >> log2_size)
    ri = lax.broadcasted_iota(jnp.int32, (PAIR, PAIR), 0)
    ci = lax.broadcasted_iota(jnp.int32, (PAIR, PAIR), 1)
    eye = (ri == ci).astype(F32)
    same_head = (ri >> HEAD_SHIFT) == (ci >> HEAD_SHIFT)
    tr = lax.broadcasted_iota(jnp.int32, (cs, cs), 0)
    tc = lax.broadcasted_iota(jnp.int32, (cs, cs), 1)
    lane_even =lax.broadcasted_iota(jnp.int32, (cs, PAIR), 1) < HEAD_DIM
    col_first = lax.broadcasted_iota(jnp.int32, (cs, 2 * cs), 1) < cs

    def stack(x):
        zero = jnp.zeros_like(x)
        return jnp.concatenate([jnp.where(lane_even, x, zero), jnp.where(lane_even, zero, x)], axis=0)

    def block_diag(m):
        zero = jnp.zeros_like(m)
        return jnp.concatenate([jnp.where(col_first, m, zero), jnp.where(col_first, zero, m)], axis=0)

    dirs = ((0, rf, vf, kkf, lwf, kdf, bf, yf_o), (1, rb, vb, kkb, lwb, kdb, bb, yb_o))
    chains = []
    for bi, sub, (d, r_ref, v_ref, kk_ref, lw_ref, kd_ref, b_ref, y_o) in (
            (bi, sub, dirs[d]) for bi, d, sub in members):
        fwd = d == 0
        rows = slice(sub * cs, (sub + 1) * cs)
        strict = (t_col < t_row) if fwd else (t_col > t_row)
        incl = strict | (t_col == t_row)
        cum = ((tc <= tr) if fwd else (tc >= tr)).astype(BF16)

        lw = lw_ref[bi, rows]
        c_in = sum(_dot(cum, piece) for piece in _split_bf16(lw, 3))
        c_ex = c_in - lw
        c_tot = jnp.sum(lw, axis=0, keepdims=True)
        gam = jnp.exp(c_tot)
        kd = kd_ref[bi, rows]
        bv = b_ref[bi, rows]
        e_neg = jnp.exp(-c_in)
        e_end = jnp.exp(c_tot - c_in)
        a_t = -kk_ref[bi, rows] * jnp.exp(c_ex)
        r_t = r_ref[bi, rows] * jnp.exp(c_in)
        b_t = bv * e_neg
        k_t = kd * e_neg
        b_h = bv * e_end
        k_h = kd * e_end
        vv = v_ref[bi, rows]
        for pr in range(n_pairs):
            sl = slice(pr * PAIR, (pr + 1) * PAIR)
            a_b, v_b = a_t[:, sl].astype(BF16), vv[:, sl].astype(BF16)
            chains.append(dict(
                bi=bi, d=d, pr=pr, sl=sl, rows=rows, order=sub if fwd else subs - 1 - sub,
                y_o=y_o, strict=strict, incl=incl, r_t=r_t[:, sl], gam=gam[:, sl],
                a_b=a_b, r_b=r_t[:, sl].astype(BF16), a_sb=stack(a_b), v_b=v_b, v_sb=stack(v_b),
                bt_sb=stack(b_t[:, sl].astype(BF16)), kt_sb=stack(k_t[:, sl].astype(BF16)),
                bk_h=jnp.concatenate([b_h[:, sl].astype(BF16), k_h[:, sl].astype(BF16)], axis=0)))
    yield

    h2 = 2 * cs
    for c in chains:
        lhs = jnp.concatenate([c["a_b"], c["r_b"]], axis=0)
        rhs = jnp.concatenate([c["bt_sb"], c["kt_sb"]], axis=0)
        gram = _dot_nt(lhs, rhs)
        a_ab = jnp.where(c["strict"], gram[:cs, :h2], 0.0)
        c["a_xk"] = jnp.concatenate([jnp.where(c["strict"], gram[:cs, h2:], 0.0),
                                     jnp.where(c["incl"], gram[cs:, h2:], 0.0)], axis=0).astype(BF16)
        c["a_rb"] = jnp.where(c["incl"], gram[cs:, :h2], 0.0).astype(BF16)
        c["n_cat"] = a_ab.astype(BF16)
        c["s_cat"] = eye_cat + jnp.where(same_block(1), a_ab, 0.0)
    yield

    for level in range(1, cs_shift):
        join = same_block(level + 1) & ~same_block(level)
        zero = jnp.zeros((cs, h2), BF16)
        for c in chains:
            c["z_cat"] = _dot(c["s_cat"].astype(BF16), block_diag(jnp.where(join, c["n_cat"], zero))).astype(BF16)
        yield
        for c in chains:
            c["s_cat"] = c["s_cat"] + _dot(c["z_cat"], block_diag(c["s_cat"].astype(BF16)))
        yield
    for c in chains:
        c["s_cat"] = c["s_cat"].astype(BF16)

    zero_b = jnp.zeros((cs, PAIR), BF16)
    for c in chains:
        av = _dot(c["a_xk"], c["v_sb"])
        c["w1_sb"] = stack(av[:cs].astype(BF16))
        c["av"] = av[cs:]
        c["gam_col"] = jnp.sum(eye * c["gam"], axis=1, keepdims=True)
    yield
    for c in chains:
        c["pq"] = _dot(c["s_cat"], jnp.concatenate([c["a_sb"], c["w1_sb"]], axis=1)).astype(BF16)
    yield
    for c in chains:
        pq = c["pq"]
        ry = _dot(c["a_rb"], jnp.concatenate([stack(pq[:, :PAIR]), stack(pq[:, PAIR:])], axis=1))
        c["r_hat"] = (c["r_t"] + ry[:, :PAIR]).astype(BF16)
        c["y_hat"] = ry[:, PAIR:] + c["av"]
        rhs = jnp.concatenate([pq, jnp.concatenate([zero_b, c["v_b"]], axis=1)], axis=0)
        gd = _dot_tn(c["bk_h"], rhs)
        c["btp"] = jnp.where(same_head, gd[:, :PAIR], 0.0).astype(BF16)
        c["dd"] = jnp.where(same_head, gd[:, PAIR:], 0.0)
    for order in range(subs):
        yield
        for c in (c for c in chains if c["order"] == order):
            h0 = h_scr[c["bi"], c["d"], c["pr"]]
            yh = _dot(jnp.concatenate([c["r_hat"], c["btp"]], axis=0), h0.astype(BF16))
            c["y_o"][c["bi"], c["rows"], c["sl"]] = yh[:cs] + c["y_hat"]
            h_scr[c["bi"], c["d"], c["pr"]] = c["gam_col"] * h0 + yh[cs:] + c["dd"]


def _attn_stages(n, nb, q_ref, rows, kv_window, qg_ref, kg_ref, bias_ref, sink_ref, o_ref):
    blk = rows.stop - rows.start
    head_mean = lambda z: _head_sum(z, 1) * (1.0 / HEAD_DIM)

    k_win = jnp.concatenate([kv[:, :KV_COLS] for kv in kv_window], axis=0)
    v_win = jnp.concatenate([kv[:, KV_COLS:] for kv in kv_window], axis=0)
    kn = k_win * lax.rsqrt(head_mean(k_win * k_win) + NORM_EPS) * kg_ref[...]
    v_b = v_win.astype(BF16)

    lane = lax.broadcasted_iota(jnp.int32, kn.shape, 1)
    k_at = []
    for g in range(KV_HEADS):
        own = jnp.where((lane >> HEAD_SHIFT) == g, kn, 0.0)
        other = pltpu.roll(own, HEAD_DIM, 1)
        k_at.append([own if p == g else other for p in range(2)])

    row = lax.broadcasted_iota(jnp.int32, (blk, 3 * blk), 0)
    col = lax.broadcasted_iota(jnp.int32, (blk, 3 * blk), 1)
    rel = col - blk - row
    valid = (jnp.abs(rel) <= WINDOW)
    valid &= (col >= blk) | (n > 0)
    valid &= (col < 2 * blk) | (n < nb - 1)
    out_lane_even = lax.broadcasted_iota(jnp.int32, (blk, LANES), 1) < HEAD_DIM

    group = Q_HEADS // KV_HEADS
    n_slabs = ATTN_WIDTH // LANES
    slabs_per_group = n_slabs // KV_HEADS
    qn = []
    for s in range(n_slabs):
        q = q_ref[0, rows, s * LANES:(s + 1) * LANES]
        qn.append((q * lax.rsqrt(head_mean(q * q) + NORM_EPS) * qg_ref[...] * (HEAD_DIM ** -0.5)).astype(BF16))
    scores = []
    for g in range(KV_HEADS):
        kcat = jnp.concatenate([k_at[g][0], k_at[g][1]], axis=0).astype(BF16)
        q_g = jnp.concatenate(qn[g * slabs_per_group:(g + 1) * slabs_per_group], axis=0)
        scores.append(_dot_nt(q_g, kcat))
    yield
    probs = []
    for s in range(n_slabs):
        heads = (2 * s, 2 * s + 1)
        g, sg = s // slabs_per_group, s % slabs_per_group
        sc = [jnp.where(valid, scores[g][sg * blk:(sg + 1) * blk, p * 3 * blk:(p + 1) * 3 * blk] + bias_ref[h],
                        -jnp.inf) for p, h in enumerate(heads)]
        sinks = [sink_ref[h] for h in heads]
        m = [jnp.maximum(jnp.max(z, axis=-1, keepdims=True), sk) for z, sk in zip(sc, sinks)]
        e = [jnp.exp(z - mm) for z, mm in zip(sc, m)]
        denom = [jnp.sum(ee, axis=-1, keepdims=True) + jnp.exp(sk - mm) for ee, sk, mm in zip(e, sinks, m)]
        probs += [(ee / dd).astype(BF16) for ee, dd in zip(e, denom)]
        yield
    o_all = _dot(jnp.concatenate(probs, axis=0), v_b)
    for s in range(n_slabs):
        g = (2 * s) // group
        halves = []
        for p in range(2):
            h = 2 * s + p
            o = o_all[h * blk:(h + 1) * blk]
            halves.append(o if p == g else pltpu.roll(o, HEAD_DIM, 1))
        o_ref[0, rows, s * LANES:(s + 1) * LANES] = jnp.where(out_lane_even, halves[0], halves[1])


def _t5_bucket(rel):
    nb = REL_BUCKETS // 2
    max_exact = nb // 2
    ret = jnp.where(rel > 0, nb, 0)
    n = jnp.abs(rel)
    large = max_exact + (jnp.log(jnp.maximum(n, 1).astype(F32) / max_exact)
                         / math.log(REL_MAX_DIST / max_exact) * (nb - max_exact)).astype(jnp.int32)
    large = jnp.minimum(large, nb - 1)
    return ret + jnp.where(n < max_exact, n, large)


BIAS_SPAN = 4 * BLOCK


def _bias_body(tab_ref, o_ref):
    blk = o_ref.shape[1]
    x = jnp.broadcast_to(tab_ref[0], (blk, BIAS_SPAN))
    shifted = pltpu.roll(x, BIAS_SPAN - (blk - 1), 1, stride=1, stride_axis=0)
    o_ref[0] = shifted[:, :3 * blk]


def _bias_table(rel_bias):
    rel = jnp.arange(BIAS_SPAN) - (2 * BLOCK - 1)
    tab = jnp.transpose(rel_bias[_t5_bucket(rel)].astype(F32))
    shape = (BLOCK, 3 * BLOCK)
    return pl.pallas_call(
        _bias_body,
        grid=(Q_HEADS,),
        in_specs=[pl.BlockSpec((1, 1, BIAS_SPAN), lambda h: (h, 0, 0))],
        out_specs=pl.BlockSpec((1,) + shape, lambda h: (h, 0, 0)),
        out_shape=jax.ShapeDtypeStruct((Q_HEADS,) + shape, F32),
        name="bias_table",
    )(tab.reshape(Q_HEADS, 1, BIAS_SPAN))


def _mixers_body(*refs, n_blocks, per_step):
    scan_refs = refs[:12]
    q_ref, kvp_ref, kvc_ref, kvn_ref, qg_ref, kg_ref, bias_ref, sink_ref = refs[12:20]
    yf_o, yb_o, attn_o, h_scr = refs[20:]
    j = pl.program_id(0)
    kv_blocks = ([kvp_ref[0]] + [kvc_ref[0, s * BLOCK:(s + 1) * BLOCK] for s in range(per_step)] + [kvn_ref[0]])

    def attention():
        for s in range(per_step):
            n = (j * per_step + s) % n_blocks
            yield from _attn_stages(n, n_blocks, q_ref, slice(s * BLOCK, (s + 1) * BLOCK), kv_blocks[s:s + 3],
                                    qg_ref, kg_ref, bias_ref, sink_ref, attn_o)

    _scan_step(j, *scan_refs, yf_o, yb_o, h_scr, filler=attention())


def _mixers(r, v, kk, lw0, lw1, kd0, kd1, b0, b1, q, kv, q_gain, k_gain, rel_bias, sink):
    b, t, c = r.shape
    steps = t // (CHUNKS_PER_STEP * CHUNK)
    nb = t // BLOCK
    per_step = (b * nb) // steps
    assert per_step * steps == b * nb and nb % per_step == 0
    nq = nb // per_step
    fw = lambda j: (0, j, 0)
    bw = lambda j: (0, steps - 1 - j, 0)
    blk = (b, CHUNKS_PER_STEP * CHUNK, c)
    cur = lambda j: (j // nq, j % nq, 0)
    prv = lambda j: (j // nq, jnp.maximum((j % nq) * per_step - 1, 0), 0)
    nxt = lambda j: (j // nq, jnp.minimum((j % nq + 1) * per_step, nb - 1), 0)
    c2 = lambda j: (0, 0)
    c3 = lambda j: (0, 0, 0)
    kvblk = (1, BLOCK, 2 * KV_COLS)
    return pl.pallas_call(
        functools.partial(_mixers_body, n_blocks=nb, per_step=per_step),
        grid=(steps,),
        in_specs=[pl.BlockSpec(blk, fw)] * 6 + [pl.BlockSpec(blk, bw)] * 6
                 + [pl.BlockSpec((1, per_step * BLOCK, ATTN_WIDTH), cur), pl.BlockSpec(kvblk, prv),
                    pl.BlockSpec((1, per_step * BLOCK, 2 * KV_COLS), cur), pl.BlockSpec(kvblk, nxt),
                    pl.BlockSpec((1, LANES), c2), pl.BlockSpec((1, LANES), c2),
                    pl.BlockSpec((Q_HEADS, BLOCK, 3 * BLOCK), c3), pl.BlockSpec(memory_space=pltpu.SMEM)],
        out_specs=[pl.BlockSpec(blk, fw), pl.BlockSpec(blk, bw),
                   pl.BlockSpec((1, per_step * BLOCK, ATTN_WIDTH), cur)],
        out_shape=[jax.ShapeDtypeStruct((b, t, c), F32)] * 2 + [jax.ShapeDtypeStruct((b, t, ATTN_WIDTH), F32)],
        scratch_shapes=[pltpu.VMEM((b, 2, c // PAIR, PAIR, PAIR), F32)],
        compiler_params=pltpu.CompilerParams(dimension_semantics=("arbitrary",), vmem_limit_bytes=VMEM_LIMIT),
        name="mixers",
    )(r, v, kk, lw0, kd0, b0, r, v, kk, lw1, kd1, b1,
      q, kv, kv, kv, jnp.tile(q_gain, 2).reshape(1, LANES), jnp.tile(k_gain, 2).reshape(1, LANES),
      _bias_table(rel_bias), sink.astype(F32))


def _outffn_body(*refs, tiles_per_seq, ff_chunk):
    main, prev, nxt = refs[0:6], refs[6:12], refs[12:18]
    lw_ref, lb_ref, wo_ref, gf_ref, wu_ref, cw_ref, cb_ref, wd_ref, o_ref, act_scr = refs[18:]
    i = pl.program_id(0)
    tile = main[0].shape[0]
    halo = prev[0].shape[0]
    d_ff = wd_ref.shape[0]
    c = RWKV_WIDTH
    first = (i % tiles_per_seq) == 0
    last = (i % tiles_per_seq) == tiles_per_seq - 1

    def mixed(x_ref, yf_ref, yb_ref, bonus_ref, gate_ref, attn_ref):
        y = yf_ref[...] + yb_ref[...]
        mu = _head_sum(y, 2) * (1.0 / HEAD_DIM)
        yc = y - mu
        var = _head_sum(yc * yc, 1) * (1.0 / HEAD_DIM)
        yn = yc * lax.rsqrt(var + LNX_EPS) * lw_ref[...] + lb_ref[...]
        mix_r = (yn + bonus_ref[...]) * gate_ref[...]
        return (x_ref[...] + _dot(mix_r.astype(BF16), wo_ref[:c, :])
                + _dot(attn_ref[...].astype(BF16), wo_ref[c:, :]))

    x1 = mixed(*main)
    x1cat = jnp.concatenate([mixed(*prev), x1, mixed(*nxt)], axis=0)
    total = tile + 2 * halo
    hcat = x1cat * lax.rsqrt(jnp.mean(x1cat * x1cat, axis=-1, keepdims=True) + NORM_EPS) * gf_ref[...]
    row = lax.broadcasted_iota(jnp.int32, (total, 1), 0)
    outside = (first & (row < halo)) | (last & (row >= halo + tile))
    hcat = jnp.where(outside, 0.0, hcat).astype(BF16)

    def conv(cols):
        u = _dot(hcat, wu_ref[:, cols])
        up = pltpu.roll(u, 1, 0)[halo:halo + tile]
        un = pltpu.roll(u, total - 1, 0)[halo:halo + tile]
        return (up * cw_ref[0:1, cols] + u[halo:halo + tile] * cw_ref[1:2, cols] + un * cw_ref[2:3, cols]
                + cb_ref[:, cols])

    for c in range(d_ff // ff_chunk):
        g = conv(slice(c * ff_chunk, (c + 1) * ff_chunk))
        val = conv(slice(d_ff + c * ff_chunk, d_ff + (c + 1) * ff_chunk))
        act_scr[:, c * ff_chunk:(c + 1) * ff_chunk] = (g * _sigmoid(g) * val).astype(BF16)
    o_ref[...] = x1 + _dot(act_scr[...], wd_ref[...])


def _out_ffn(x2, yf, yb, bonus, gate, attn, lnx_w, lnx_b, w_out, g_ffn, w_up, conv_w, conv_b, w_down,
             seq, tile, ff_chunk):
    rows, d = x2.shape
    c = RWKV_WIDTH
    d_ff = w_down.shape[0]
    halo = SUBLANES_F32
    per = tile // halo
    row = lambda i: (i, 0)
    prv = lambda i: (jnp.maximum(i * per - 1, 0), 0)
    nxt = lambda i: (jnp.minimum((i + 1) * per, rows // halo - 1), 0)
    const = lambda i: (0, 0)
    resident = dict(pipeline_mode=pl.Buffered(1))
    streams = (x2, yf, yb, bonus, gate, attn)

    def stream_specs(nrows, index_map):
        return [pl.BlockSpec((nrows, z.shape[1]), index_map) for z in streams]

    body = functools.partial(_outffn_body, tiles_per_seq=seq // tile, ff_chunk=ff_chunk)
    return pl.pallas_call(
        body,
        grid=(rows // tile,),
        in_specs=stream_specs(tile, row) + stream_specs(halo, prv) + stream_specs(halo, nxt)
                 + [pl.BlockSpec((1, c), const), pl.BlockSpec((1, c), const),
                    pl.BlockSpec(w_out.shape, const, **resident), pl.BlockSpec((1, d), const),
                    pl.BlockSpec((d, 2 * d_ff), const, **resident),
                    pl.BlockSpec((3, 2 * d_ff), const), pl.BlockSpec((1, 2 * d_ff), const),
                    pl.BlockSpec((d_ff, d), const, **resident)],
        out_specs=pl.BlockSpec((tile, d), row),
        out_shape=jax.ShapeDtypeStruct((rows, d), F32),
        scratch_shapes=[pltpu.VMEM((tile, d_ff), BF16)],
        compiler_params=pltpu.CompilerParams(dimension_semantics=("parallel",), vmem_limit_bytes=VMEM_LIMIT),
        name="out_ffn",
    )(*streams, *streams, *streams, lnx_w.reshape(1, c), lnx_b.reshape(1, c), w_out.astype(BF16),
      g_ffn.reshape(1, d), w_up.astype(BF16), conv_w, conv_b.reshape(1, 2 * d_ff), w_down.astype(BF16))


def _layer(x, g_mix, w_in, mu_prev, mu_next, w0, w2, a0, a2, g2, k_k, k_a, r_k, lnx_w, lnx_b,
           q_gain, k_gain, rel_bias, sink, w_out, g_ffn, w_up, conv_w, conv_b, w_down):
    b, t, d = x.shape
    rows = b * t
    x2 = x.reshape(rows, d)
    q, kv, *ops = _inproj_prep(x2, g_mix, w_in, mu_prev, mu_next, w0, w2, a0, a2, g2, k_k, k_a, r_k.reshape(-1),
                               seq=t, tile=min(256, t))
    r, v, kk, lw0, lw1, kd0, kd1, b0, b1 = (z.reshape(b, t, RWKV_WIDTH) for z in ops[:9])
    gate, bonus = ops[9:]
    yf, yb, attn = _mixers(r, v, kk, lw0, lw1, kd0, kd1, b0, b1, q.reshape(b, t, ATTN_WIDTH),
                           kv.reshape(b, t, 2 * KV_COLS), q_gain, k_gain, rel_bias, sink)
    flat = lambda z: z.reshape(rows, z.shape[-1])
    out = _out_ffn(x2, flat(yf), flat(yb), bonus, gate, flat(attn), lnx_w, lnx_b, w_out, g_ffn,
                   w_up, conv_w, conv_b, w_down, seq=t, tile=min(512, t), ff_chunk=256)
    return out.reshape(b, t, d)


def kernel(x, g_mix, w_in, mu_prev, mu_next, w0, w2, a0, a2, g2, k_k, k_a, r_k, lnx_w, lnx_b, q_gain, k_gain,
           rel_bias, sink, w_out, g_ffn, w_up, conv_w, conv_b, w_down):
    depth = g_mix.shape[0]
    for l in range(depth):
        x = _layer(x, g_mix[l], w_in[l], mu_prev[l], mu_next[l], w0[l], w2[l], a0[l], a2[l], g2[l], k_k[l], k_a[l],
                   r_k[l], lnx_w[l], lnx_b[l], q_gain[l], k_gain[l], rel_bias, sink[l], w_out[l], g_ffn[l],
                   w_up[l], conv_w[l], conv_b[l], w_down[l])
    return x
```

```python
import functools
import math

import jax
import jax.numpy as jnp
from jax import lax
from jax.experimental import pallas as pl
from jax.experimental.pallas import tpu as pltpu

F32 = jnp.float32
BF16 = jnp.bfloat16

HEAD_DIM = 64
RWKV_WIDTH = 512
ATTN_WIDTH = 512
KV_HEADS = 2
Q_HEADS = 8
DECAY_LORA = 64
ICLR_LORA = 64
GATE_LORA = 128
RWKV_COLS = 3 * RWKV_WIDTH + DECAY_LORA + ICLR_LORA + GATE_LORA
KV_COLS = KV_HEADS * HEAD_DIM
WINDOW = 128
BLOCK = 128
REL_BUCKETS = 32
REL_MAX_DIST = 128
NORM_EPS = 1e-6
LNX_EPS = 64e-5
KK_EPS = 1e-12

LANES = 128
MXU_DIM = 256
SUBLANES_F32 = 8
SUBLANES_BF16 = 16
VMEM_LIMIT = 48 * 1024 * 1024

CHUNK = 64
CHUNKS_PER_STEP = 1
PAIR = 2 * HEAD_DIM
assert PAIR == LANES
HEAD_SHIFT = HEAD_DIM.bit_length() - 1
assert 1 << HEAD_SHIFT == HEAD_DIM


def _dot(a, b, precision=None):
    return jnp.dot(a, b, preferred_element_type=F32, precision=precision)


def _dot_nt(a, b, precision=None):
    return lax.dot_general(a, b, (((1,), (1,)), ((), ())), preferred_element_type=F32, precision=precision)


def _dot_tn(a, b, precision=None):
    return lax.dot_general(a, b, (((0,), (0,)), ((), ())), preferred_element_type=F32, precision=precision)


def _sigmoid(x):
    return 1.0 / (1.0 + jnp.exp(-x))


def _split_bf16(x, parts):
    out = []
    for _ in range(parts):
        h = x.astype(BF16)
        out.append(h)
        x = x - h.astype(F32)
    return out


def _head_sum(x, parts):
    width = x.shape[-1]
    blk = min(width, MXU_DIM)
    r = lax.broadcasted_iota(jnp.int32, (blk, blk), 0) >> HEAD_SHIFT
    c = lax.broadcasted_iota(jnp.int32, (blk, blk), 1) >> HEAD_SHIFT
    m = (r == c).astype(BF16)
    pieces = _split_bf16(x, parts)
    cols = []
    for c0 in range(0, width, blk):
        acc = None
        for h in pieces:
            t = _dot(h[:, c0:c0 + blk], m)
            acc = t if acc is None else acc + t
        cols.append(acc)
    return cols[0] if len(cols) == 1 else jnp.concatenate(cols, axis=1)


def _inprep_body(x_ref, xp_ref, xn_ref, g_ref, wr_ref, wq_ref, wkv_ref,
                 mup_ref, mun_ref, w0_ref, w2_ref, a0_ref, a2_ref, g2_ref, kk_ref, ka_ref, rk_ref,
                 q_o, kv_o, r_o, v_o, kk_o, lw0_o, lw1_o, kd0_o, kd1_o, b0_o, b1_o, gate_o, bonus_o,
                 *, tiles_per_seq):
    i = pl.program_id(0)
    tile = x_ref.shape[0]
    halo = xp_ref.shape[0]
    first = (i % tiles_per_seq) == 0
    last = (i % tiles_per_seq) == tiles_per_seq - 1

    def norm(x):
        return (x * lax.rsqrt(jnp.mean(x * x, axis=-1, keepdims=True) + NORM_EPS) * g_ref[...]).astype(BF16)

    h = norm(x_ref[...])
    q_o[...] = _dot(h, wq_ref[...])
    kv_o[...] = _dot(h, wkv_ref[...])
    hp = norm(xp_ref[...])
    hn = norm(xn_ref[...])
    hp = jnp.where(first, jnp.zeros_like(hp), hp)
    hn = jnp.where(last, jnp.zeros_like(hn), hn)
    pcat = _dot(jnp.concatenate([hp, h, hn], axis=0), wr_ref[...])
    total = tile + 2 * halo
    p = pcat[halo:halo + tile]
    prev = pltpu.roll(pcat, 1, 0)[halo:halo + tile]
    nxt = pltpu.roll(pcat, total - 1, 0)[halo:halo + tile]
    pf = p + mup_ref[...] * (prev - p) + mun_ref[...] * (nxt - p)

    c = RWKV_WIDTH
    r, k, v = pf[:, :c], pf[:, c:2 * c], pf[:, 2 * c:3 * c]
    lora = pf[:, 3 * c:3 * c + LANES]
    xg = pf[:, 3 * c + LANES:]
    lora_t = jnp.tanh(lora).astype(BF16)
    lora_b = lora.astype(BF16)

    kx = k * kk_ref[...]
    kk = kx * lax.rsqrt(_head_sum(kx * kx, 1) + KK_EPS)
    gate = _dot(_sigmoid(xg).astype(BF16), g2_ref[...])

    kds = []
    for d, (lw_o, kd_o, b_o) in enumerate(((lw0_o, kd0_o, b0_o), (lw1_o, kd1_o, b1_o))):
        w_raw = w0_ref[d:d + 1, :] + _dot(lora_t, w2_ref[d])
        lw_o[...] = (-math.exp(-0.5)) * _sigmoid(w_raw)
        iclr = _sigmoid(a0_ref[d:d + 1, :] + _dot(lora_b, a2_ref[d]))
        kd = k * (1.0 + (iclr - 1.0) * ka_ref[...])
        kd_o[...] = kd
        b_o[...] = iclr * kk
        kds.append(kd)

    r_o[...] = r
    v_o[...] = v
    kk_o[...] = kk
    gate_o[...] = gate
    bonus_o[...] = _head_sum(r * (kds[0] + kds[1]) * rk_ref[...], 1) * v


def _inproj_prep(x2, g_mix, w_in, mu_prev, mu_next, w0, w2, a0, a2, g2, k_k, k_a, r_k, seq, tile):
    rows, d = x2.shape
    c = RWKV_WIDTH
    cols = RWKV_COLS
    halo = SUBLANES_BF16
    per = tile // halo
    wb = w_in.astype(BF16)
    wr, wq, wkv = wb[:, :cols], wb[:, cols:cols + ATTN_WIDTH], wb[:, cols + ATTN_WIDTH:]
    zeros = jnp.zeros((2, DECAY_LORA, c), F32)
    w2p = jnp.concatenate([w2, zeros], axis=1).astype(BF16)
    a2p = jnp.concatenate([zeros, a2], axis=1).astype(BF16)
    row = lambda i: (i, 0)
    prv = lambda i: (jnp.maximum(i * per - 1, 0), 0)
    nxt = lambda i: (jnp.minimum((i + 1) * per, rows // halo - 1), 0)
    c2 = lambda i: (0, 0)
    c3 = lambda i: (0, 0, 0)
    resident = dict(pipeline_mode=pl.Buffered(1))
    out = jax.ShapeDtypeStruct((rows, c), F32)
    body = functools.partial(_inprep_body, tiles_per_seq=seq // tile)
    return pl.pallas_call(
        body,
        grid=(rows // tile,),
        in_specs=[pl.BlockSpec((tile, d), row), pl.BlockSpec((halo, d), prv), pl.BlockSpec((halo, d), nxt),
                  pl.BlockSpec((1, d), c2),
                  pl.BlockSpec(wr.shape, c2, **resident), pl.BlockSpec(wq.shape, c2, **resident),
                  pl.BlockSpec(wkv.shape, c2, **resident),
                  pl.BlockSpec((1, cols), c2), pl.BlockSpec((1, cols), c2),
                  pl.BlockSpec((2, c), c2), pl.BlockSpec((2, LANES, c), c3),
                  pl.BlockSpec((2, c), c2), pl.BlockSpec((2, LANES, c), c3),
                  pl.BlockSpec((GATE_LORA, c), c2),
                  pl.BlockSpec((1, c), c2), pl.BlockSpec((1, c), c2), pl.BlockSpec((1, c), c2)],
        out_specs=[pl.BlockSpec((tile, ATTN_WIDTH), row), pl.BlockSpec((tile, 2 * KV_COLS), row)]
                  + [pl.BlockSpec((tile, c), row)] * 11,
        out_shape=[jax.ShapeDtypeStruct((rows, ATTN_WIDTH), F32), jax.ShapeDtypeStruct((rows, 2 * KV_COLS), F32)]
                  + [out] * 11,
        compiler_params=pltpu.CompilerParams(dimension_semantics=("parallel",), vmem_limit_bytes=VMEM_LIMIT),
        name="inproj_prep",
    )(x2, x2, x2, g_mix.reshape(1, d), wr, wq, wkv, mu_prev.reshape(1, cols), mu_next.reshape(1, cols),
      w0, w2p, a0, a2p, g2.astype(BF16), k_k.reshape(1, c), k_a.reshape(1, c), r_k.reshape(1, c))


def _scan_step(j, *refs, filler=iter(())):
    h_scr = refs[-1]

    @pl.when(j == 0)
    def _():
        h_scr[...] = jnp.zeros_like(h_scr)

    n_batch, rows = refs[0].shape[0], refs[0].shape[1]
    members = [(bi, d, sub) for bi in range(n_batch) for d in range(2) for sub in range(rows // CHUNK)]
    for _ in _scan_stages(members, *refs):
        next(filler, None)
    for _ in filler:
        pass


def _scan_stages(members, rf, vf, kkf, lwf, kdf, bf, rb, vb, kkb, lwb, kdb, bb, yf_o, yb_o, h_scr):
    cs = CHUNK
    subs = rf.shape[1] // cs
    n_pairs = rf.shape[2] // PAIR

    cs_shift = cs.bit_length() - 1
    assert 1 << cs_shift == cs
    t_row = lax.broadcasted_iota(jnp.int32, (cs, 2 * cs), 0)
    t_col = lax.broadcasted_iota(jnp.int32, (cs, 2 * cs), 1) & (cs - 1)
    eye_cat = (t_row == t_col).astype(F32)
    same_block = lambda log2_size: (t_row >> log2_size) == (t_col >> log2_size)
    ri = lax.broadcasted_iota(jnp.int32, (PAIR, PAIR), 0)
    ci = lax.broadcasted_iota(jnp.int32, (PAIR, PAIR), 1)
    eye = (ri == ci).astype(F32)
    same_head = (ri >> HEAD_SHIFT) == (ci >> HEAD_SHIFT)
    tr = lax.broadcasted_iota(jnp.int32, (cs, cs), 0)
    tc = lax.broadcasted_iota(jnp.int32, (cs, cs), 1)
    lane_even =lax.broadcasted_iota(jnp.int32, (cs, PAIR), 1) < HEAD_DIM
    col_first = lax.broadcasted_iota(jnp.int32, (cs, 2 * cs), 1) < cs

    def stack(x):
        zero = jnp.zeros_like(x)
        return jnp.concatenate([jnp.where(lane_even, x, zero), jnp.where(lane_even, zero, x)], axis=0)

    def block_diag(m):
        zero = jnp.zeros_like(m)
        return jnp.concatenate([jnp.where(col_first, m, zero), jnp.where(col_first, zero, m)], axis=0)

    dirs = ((0, rf, vf, kkf, lwf, kdf, bf, yf_o), (1, rb, vb, kkb, lwb, kdb, bb, yb_o))
    chains = []
    for bi, sub, (d, r_ref, v_ref, kk_ref, lw_ref, kd_ref, b_ref, y_o) in (
            (bi, sub, dirs[d]) for bi, d, sub in members):
        fwd = d == 0
        rows = slice(sub * cs, (sub + 1) * cs)
        strict = (t_col < t_row) if fwd else (t_col > t_row)
        incl = strict | (t_col == t_row)
        cum = ((tc <= tr) if fwd else (tc >= tr)).astype(BF16)

        lw = lw_ref[bi, rows]
        c_in = sum(_dot(cum, piece) for piece in _split_bf16(lw, 3))
        c_ex = c_in - lw
        c_tot = jnp.sum(lw, axis=0, keepdims=True)
        gam = jnp.exp(c_tot)
        kd = kd_ref[bi, rows]
        bv = b_ref[bi, rows]
        e_neg = jnp.exp(-c_in)
        e_end = jnp.exp(c_tot - c_in)
        a_t = -kk_ref[bi, rows] * jnp.exp(c_ex)
        r_t = r_ref[bi, rows] * jnp.exp(c_in)
        b_t = bv * e_neg
        k_t = kd * e_neg
        b_h = bv * e_end
        k_h = kd * e_end
        vv = v_ref[bi, rows]
        for pr in range(n_pairs):
            sl = slice(pr * PAIR, (pr + 1) * PAIR)
            a_b, v_b = a_t[:, sl].astype(BF16), vv[:, sl].astype(BF16)
            chains.append(dict(
                bi=bi, d=d, pr=pr, sl=sl, rows=rows, order=sub if fwd else subs - 1 - sub,
                y_o=y_o, strict=strict, incl=incl, r_t=r_t[:, sl], gam=gam[:, sl],
                a_b=a_b, r_b=r_t[:, sl].astype(BF16), a_sb=stack(a_b), v_b=v_b, v_sb=stack(v_b),
                bt_sb=stack(b_t[:, sl].astype(BF16)), kt_sb=stack(k_t[:, sl].astype(BF16)),
                bk_h=jnp.concatenate([b_h[:, sl].astype(BF16), k_h[:, sl].astype(BF16)], axis=0)))
    yield

    h2 = 2 * cs
    for c in chains:
        lhs = jnp.concatenate([c["a_b"], c["r_b"]], axis=0)
        rhs = jnp.concatenate([c["bt_sb"], c["kt_sb"]], axis=0)
        gram = _dot_nt(lhs, rhs)
        a_ab = jnp.where(c["strict"], gram[:cs, :h2], 0.0)
        c["a_xk"] = jnp.concatenate([jnp.where(c["strict"], gram[:cs, h2:], 0.0),
                                     jnp.where(c["incl"], gram[cs:, h2:], 0.0)], axis=0).astype(BF16)
        c["a_rb"] = jnp.where(c["incl"], gram[cs:, :h2], 0.0).astype(BF16)
        c["n_cat"] = a_ab.astype(BF16)
        c["s_cat"] = eye_cat + jnp.where(same_block(1), a_ab, 0.0)
    yield

    for level in range(1, cs_shift):
        join = same_block(level + 1) & ~same_block(level)
        zero = jnp.zeros((cs, h2), BF16)
        for c in chains:
            c["z_cat"] = _dot(c["s_cat"].astype(BF16), block_diag(jnp.where(join, c["n_cat"], zero))).astype(BF16)
        yield
        for c in chains:
            c["s_cat"] = c["s_cat"] + _dot(c["z_cat"], block_diag(c["s_cat"].astype(BF16)))
        yield
    for c in chains:
        c["s_cat"] = c["s_cat"].astype(BF16)

    zero_b = jnp.zeros((cs, PAIR), BF16)
    for c in chains:
        av = _dot(c["a_xk"], c["v_sb"])
        c["w1_sb"] = stack(av[:cs].astype(BF16))
        c["av"] = av[cs:]
        c["gam_col"] = jnp.sum(eye * c["gam"], axis=1, keepdims=True)
    yield
    for c in chains:
        c["pq"] = _dot(c["s_cat"], jnp.concatenate([c["a_sb"], c["w1_sb"]], axis=1)).astype(BF16)
    yield
    for c in chains:
        pq = c["pq"]
        ry = _dot(c["a_rb"], jnp.concatenate([stack(pq[:, :PAIR]), stack(pq[:, PAIR:])], axis=1))
        c["r_hat"] = (c["r_t"] + ry[:, :PAIR]).astype(BF16)
        c["y_hat"] = ry[:, PAIR:] + c["av"]
        rhs = jnp.concatenate([pq, jnp.concatenate([zero_b, c["v_b"]], axis=1)], axis=0)
        gd = _dot_tn(c["bk_h"], rhs)
        c["btp"] = jnp.where(same_head, gd[:, :PAIR], 0.0).astype(BF16)
        c["dd"] = jnp.where(same_head, gd[:, PAIR:], 0.0)
    for order in range(subs):
        yield
        for c in (c for c in chains if c["order"] == order):
            h0 = h_scr[c["bi"], c["d"], c["pr"]]
            yh = _dot(jnp.concatenate([c["r_hat"], c["btp"]], axis=0), h0.astype(BF16))
            c["y_o"][c["bi"], c["rows"], c["sl"]] = yh[:cs] + c["y_hat"]
            h_scr[c["bi"], c["d"], c["pr"]] = c["gam_col"] * h0 + yh[cs:] + c["dd"]


def _attn_stages(n, nb, q_ref, rows, kv_window, qg_ref, kg_ref, bias_ref, sink_ref, o_ref):
    blk = rows.stop - rows.start
    head_mean = lambda z: _head_sum(z, 1) * (1.0 / HEAD_DIM)

    k_win = jnp.concatenate([kv[:, :KV_COLS] for kv in kv_window], axis=0)
    v_win = jnp.concatenate([kv[:, KV_COLS:] for kv in kv_window], axis=0)
    kn = k_win * lax.rsqrt(head_mean(k_win * k_win) + NORM_EPS) * kg_ref[...]
    v_b = v_win.astype(BF16)

    lane = lax.broadcasted_iota(jnp.int32, kn.shape, 1)
    k_at = []
    for g in range(KV_HEADS):
        own = jnp.where((lane >> HEAD_SHIFT) == g, kn, 0.0)
        other = pltpu.roll(own, HEAD_DIM, 1)
        k_at.append([own if p == g else other for p in range(2)])

    row = lax.broadcasted_iota(jnp.int32, (blk, 3 * blk), 0)
    col = lax.broadcasted_iota(jnp.int32, (blk, 3 * blk), 1)
    rel = col - blk - row
    valid = (jnp.abs(rel) <= WINDOW)
    valid &= (col >= blk) | (n > 0)
    valid &= (col < 2 * blk) | (n < nb - 1)
    out_lane_even = lax.broadcasted_iota(jnp.int32, (blk, LANES), 1) < HEAD_DIM

    group = Q_HEADS // KV_HEADS
    n_slabs = ATTN_WIDTH // LANES
    slabs_per_group = n_slabs // KV_HEADS
    qn = []
    for s in range(n_slabs):
        q = q_ref[0, rows, s * LANES:(s + 1) * LANES]
        qn.append((q * lax.rsqrt(head_mean(q * q) + NORM_EPS) * qg_ref[...] * (HEAD_DIM ** -0.5)).astype(BF16))
    scores = []
    for g in range(KV_HEADS):
        kcat = jnp.concatenate([k_at[g][0], k_at[g][1]], axis=0).astype(BF16)
        q_g = jnp.concatenate(qn[g * slabs_per_group:(g + 1) * slabs_per_group], axis=0)
        scores.append(_dot_nt(q_g, kcat))
    yield
    probs = []
    for s in range(n_slabs):
        heads = (2 * s, 2 * s + 1)
        g, sg = s // slabs_per_group, s % slabs_per_group
        sc = [jnp.where(valid, scores[g][sg * blk:(sg + 1) * blk, p * 3 * blk:(p + 1) * 3 * blk] + bias_ref[h],
                        -jnp.inf) for p, h in enumerate(heads)]
        sinks = [sink_ref[h] for h in heads]
        m = [jnp.maximum(jnp.max(z, axis=-1, keepdims=True), sk) for z, sk in zip(sc, sinks)]
        e = [jnp.exp(z - mm) for z, mm in zip(sc, m)]
        denom = [jnp.sum(ee, axis=-1, keepdims=True) + jnp.exp(sk - mm) for ee, sk, mm in zip(e, sinks, m)]
        probs += [(ee / dd).astype(BF16) for ee, dd in zip(e, denom)]
        yield
    o_all = _dot(jnp.concatenate(probs, axis=0), v_b)
    for s in range(n_slabs):
        g = (2 * s) // group
        halves = []
        for p in range(2):
            h = 2 * s + p
            o = o_all[h * blk:(h + 1) * blk]
            halves.append(o if p == g else pltpu.roll(o, HEAD_DIM, 1))
        o_ref[0, rows, s * LANES:(s + 1) * LANES] = jnp.where(out_lane_even, halves[0], halves[1])


def _t5_bucket(rel):
    nb = REL_BUCKETS // 2
    max_exact = nb // 2
    ret = jnp.where(rel > 0, nb, 0)
    n = jnp.abs(rel)
    large = max_exact + (jnp.log(jnp.maximum(n, 1).astype(F32) / max_exact)
                         / math.log(REL_MAX_DIST / max_exact) * (nb - max_exact)).astype(jnp.int32)
    large = jnp.minimum(large, nb - 1)
    return ret + jnp.where(n < max_exact, n, large)


BIAS_SPAN = 4 * BLOCK


def _bias_body(tab_ref, o_ref):
    blk = o_ref.shape[1]
    x = jnp.broadcast_to(tab_ref[0], (blk, BIAS_SPAN))
    shifted = pltpu.roll(x, BIAS_SPAN - (blk - 1), 1, stride=1, stride_axis=0)
    o_ref[0] = shifted[:, :3 * blk]


def _bias_table(rel_bias):
    rel = jnp.arange(BIAS_SPAN) - (2 * BLOCK - 1)
    tab = jnp.transpose(rel_bias[_t5_bucket(rel)].astype(F32))
    shape = (BLOCK, 3 * BLOCK)
    return pl.pallas_call(
        _bias_body,
        grid=(Q_HEADS,),
        in_specs=[pl.BlockSpec((1, 1, BIAS_SPAN), lambda h: (h, 0, 0))],
        out_specs=pl.BlockSpec((1,) + shape, lambda h: (h, 0, 0)),
        out_shape=jax.ShapeDtypeStruct((Q_HEADS,) + shape, F32),
        name="bias_table",
    )(tab.reshape(Q_HEADS, 1, BIAS_SPAN))


def _mixers_body(*refs, n_blocks, per_step):
    scan_refs = refs[:12]
    q_ref, kvp_ref, kvc_ref, kvn_ref, qg_ref, kg_ref, bias_ref, sink_ref = refs[12:20]
    yf_o, yb_o, attn_o, h_scr = refs[20:]
    j = pl.program_id(0)
    kv_blocks = ([kvp_ref[0]] + [kvc_ref[0, s * BLOCK:(s + 1) * BLOCK] for s in range(per_step)] + [kvn_ref[0]])

    def attention():
        for s in range(per_step):
            n = (j * per_step + s) % n_blocks
            yield from _attn_stages(n, n_blocks, q_ref, slice(s * BLOCK, (s + 1) * BLOCK), kv_blocks[s:s + 3],
                                    qg_ref, kg_ref, bias_ref, sink_ref, attn_o)

    _scan_step(j, *scan_refs, yf_o, yb_o, h_scr, filler=attention())


def _mixers(r, v, kk, lw0, lw1, kd0, kd1, b0, b1, q, kv, q_gain, k_gain, rel_bias, sink):
    b, t, c = r.shape
    steps = t // (CHUNKS_PER_STEP * CHUNK)
    nb = t // BLOCK
    per_step = (b * nb) // steps
    assert per_step * steps == b * nb and nb % per_step == 0
    nq = nb // per_step
    fw = lambda j: (0, j, 0)
    bw = lambda j: (0, steps - 1 - j, 0)
    blk = (b, CHUNKS_PER_STEP * CHUNK, c)
    cur = lambda j: (j // nq, j % nq, 0)
    prv = lambda j: (j // nq, jnp.maximum((j % nq) * per_step - 1, 0), 0)
    nxt = lambda j: (j // nq, jnp.minimum((j % nq + 1) * per_step, nb - 1), 0)
    c2 = lambda j: (0, 0)
    c3 = lambda j: (0, 0, 0)
    kvblk = (1, BLOCK, 2 * KV_COLS)
    return pl.pallas_call(
        functools.partial(_mixers_body, n_blocks=nb, per_step=per_step),
        grid=(steps,),
        in_specs=[pl.BlockSpec(blk, fw)] * 6 + [pl.BlockSpec(blk, bw)] * 6
                 + [pl.BlockSpec((1, per_step * BLOCK, ATTN_WIDTH), cur), pl.BlockSpec(kvblk, prv),
                    pl.BlockSpec((1, per_step * BLOCK, 2 * KV_COLS), cur), pl.BlockSpec(kvblk, nxt),
                    pl.BlockSpec((1, LANES), c2), pl.BlockSpec((1, LANES), c2),
                    pl.BlockSpec((Q_HEADS, BLOCK, 3 * BLOCK), c3), pl.BlockSpec(memory_space=pltpu.SMEM)],
        out_specs=[pl.BlockSpec(blk, fw), pl.BlockSpec(blk, bw),
                   pl.BlockSpec((1, per_step * BLOCK, ATTN_WIDTH), cur)],
        out_shape=[jax.ShapeDtypeStruct((b, t, c), F32)] * 2 + [jax.ShapeDtypeStruct((b, t, ATTN_WIDTH), F32)],
        scratch_shapes=[pltpu.VMEM((b, 2, c // PAIR, PAIR, PAIR), F32)],
        compiler_params=pltpu.CompilerParams(dimension_semantics=("arbitrary",), vmem_limit_bytes=VMEM_LIMIT),
        name="mixers",
    )(r, v, kk, lw0, kd0, b0, r, v, kk, lw1, kd1, b1,
      q, kv, kv, kv, jnp.tile(q_gain, 2).reshape(1, LANES), jnp.tile(k_gain, 2).reshape(1, LANES),
      _bias_table(rel_bias), sink.astype(F32))


def _outffn_body(*refs, tiles_per_seq, ff_chunk):
    main, prev, nxt = refs[0:6], refs[6:12], refs[12:18]
    lw_ref, lb_ref, wo_ref, gf_ref, wu_ref, cw_ref, cb_ref, wd_ref, o_ref, act_scr = refs[18:]
    i = pl.program_id(0)
    tile = main[0].shape[0]
    halo = prev[0].shape[0]
    d_ff = wd_ref.shape[0]
    c = RWKV_WIDTH
    first = (i % tiles_per_seq) == 0
    last = (i % tiles_per_seq) == tiles_per_seq - 1

    def mixed(x_ref, yf_ref, yb_ref, bonus_ref, gate_ref, attn_ref):
        y = yf_ref[...] + yb_ref[...]
        mu = _head_sum(y, 2) * (1.0 / HEAD_DIM)
        yc = y - mu
        var = _head_sum(yc * yc, 1) * (1.0 / HEAD_DIM)
        yn = yc * lax.rsqrt(var + LNX_EPS) * lw_ref[...] + lb_ref[...]
        mix_r = (yn + bonus_ref[...]) * gate_ref[...]
        return (x_ref[...] + _dot(mix_r.astype(BF16), wo_ref[:c, :])
                + _dot(attn_ref[...].astype(BF16), wo_ref[c:, :]))

    x1 = mixed(*main)
    x1cat = jnp.concatenate([mixed(*prev), x1, mixed(*nxt)], axis=0)
    total = tile + 2 * halo
    hcat = x1cat * lax.rsqrt(jnp.mean(x1cat * x1cat, axis=-1, keepdims=True) + NORM_EPS) * gf_ref[...]
    row = lax.broadcasted_iota(jnp.int32, (total, 1), 0)
    outside = (first & (row < halo)) | (last & (row >= halo + tile))
    hcat = jnp.where(outside, 0.0, hcat).astype(BF16)

    def conv(cols):
        u = _dot(hcat, wu_ref[:, cols])
        up = pltpu.roll(u, 1, 0)[halo:halo + tile]
        un = pltpu.roll(u, total - 1, 0)[halo:halo + tile]
        return (up * cw_ref[0:1, cols] + u[halo:halo + tile] * cw_ref[1:2, cols] + un * cw_ref[2:3, cols]
                + cb_ref[:, cols])

    for c in range(d_ff // ff_chunk):
        g = conv(slice(c * ff_chunk, (c + 1) * ff_chunk))
        val = conv(slice(d_ff + c * ff_chunk, d_ff + (c + 1) * ff_chunk))
        act_scr[:, c * ff_chunk:(c + 1) * ff_chunk] = (g * _sigmoid(g) * val).astype(BF16)
    o_ref[...] = x1 + _dot(act_scr[...], wd_ref[...])


def _out_ffn(x2, yf, yb, bonus, gate, attn, lnx_w, lnx_b, w_out, g_ffn, w_up, conv_w, conv_b, w_down,
             seq, tile, ff_chunk):
    rows, d = x2.shape
    c = RWKV_WIDTH
    d_ff = w_down.shape[0]
    halo = SUBLANES_F32
    per = tile // halo
    row = lambda i: (i, 0)
    prv = lambda i: (jnp.maximum(i * per - 1, 0), 0)
    nxt = lambda i: (jnp.minimum((i + 1) * per, rows // halo - 1), 0)
    const = lambda i: (0, 0)
    resident = dict(pipeline_mode=pl.Buffered(1))
    streams = (x2, yf, yb, bonus, gate, attn)

    def stream_specs(nrows, index_map):
        return [pl.BlockSpec((nrows, z.shape[1]), index_map) for z in streams]

    body = functools.partial(_outffn_body, tiles_per_seq=seq // tile, ff_chunk=ff_chunk)
    return pl.pallas_call(
        body,
        grid=(rows // tile,),
        in_specs=stream_specs(tile, row) + stream_specs(halo, prv) + stream_specs(halo, nxt)
                 + [pl.BlockSpec((1, c), const), pl.BlockSpec((1, c), const),
                    pl.BlockSpec(w_out.shape, const, **resident), pl.BlockSpec((1, d), const),
                    pl.BlockSpec((d, 2 * d_ff), const, **resident),
                    pl.BlockSpec((3, 2 * d_ff), const), pl.BlockSpec((1, 2 * d_ff), const),
                    pl.BlockSpec((d_ff, d), const, **resident)],
        out_specs=pl.BlockSpec((tile, d), row),
        out_shape=jax.ShapeDtypeStruct((rows, d), F32),
        scratch_shapes=[pltpu.VMEM((tile, d_ff), BF16)],
        compiler_params=pltpu.CompilerParams(dimension_semantics=("parallel",), vmem_limit_bytes=VMEM_LIMIT),
        name="out_ffn",
    )(*streams, *streams, *streams, lnx_w.reshape(1, c), lnx_b.reshape(1, c), w_out.astype(BF16),
      g_ffn.reshape(1, d), w_up.astype(BF16), conv_w, conv_b.reshape(1, 2 * d_ff), w_down.astype(BF16))


def _layer(x, g_mix, w_in, mu_prev, mu_next, w0, w2, a0, a2, g2, k_k, k_a, r_k, lnx_w, lnx_b,
           q_gain, k_gain, rel_bias, sink, w_out, g_ffn, w_up, conv_w, conv_b, w_down):
    b, t, d = x.shape
    rows = b * t
    x2 = x.reshape(rows, d)
    q, kv, *ops = _inproj_prep(x2, g_mix, w_in, mu_prev, mu_next, w0, w2, a0, a2, g2, k_k, k_a, r_k.reshape(-1),
                               seq=t, tile=min(256, t))
    r, v, kk, lw0, lw1, kd0, kd1, b0, b1 = (z.reshape(b, t, RWKV_WIDTH) for z in ops[:9])
    gate, bonus = ops[9:]
    yf, yb, attn = _mixers(r, v, kk, lw0, lw1, kd0, kd1, b0, b1, q.reshape(b, t, ATTN_WIDTH),
                           kv.reshape(b, t, 2 * KV_COLS), q_gain, k_gain, rel_bias, sink)
    flat = lambda z: z.reshape(rows, z.shape[-1])
    out = _out_ffn(x2, flat(yf), flat(yb), bonus, gate, flat(attn), lnx_w, lnx_b, w_out, g_ffn,
                   w_up, conv_w, conv_b, w_down, seq=t, tile=min(512, t), ff_chunk=256)
    return out.reshape(b, t, d)


def kernel(x, g_mix, w_in, mu_prev, mu_next, w0, w2, a0, a2, g2, k_k, k_a, r_k, lnx_w, lnx_b, q_gain, k_gain,
           rel_bias, sink, w_out, g_ffn, w_up, conv_w, conv_b, w_down):
    depth = g_mix.shape[0]
    for l in range(depth):
        x = _layer(x, g_mix[l], w_in[l], mu_prev[l], mu_next[l], w0[l], w2[l], a0[l], a2[l], g2[l], k_k[l], k_a[l],
                   r_k[l], lnx_w[l], lnx_b[l], q_gain[l], k_gain[l], rel_bias, sink[l], w_out[l], g_ffn[l],
                   w_up[l], conv_w[l], conv_b[l], w_down[l])
    return x
```

```python
import functools
import math

import jax
import jax.numpy as jnp
from jax import lax
from jax.experimental import pallas as pl
from jax.experimental.pallas import tpu as pltpu

F32 = jnp.float32
BF16 = jnp.bfloat16

HEAD_DIM = 64
RWKV_WIDTH = 512
ATTN_WIDTH = 512
KV_HEADS = 2
Q_HEADS = 8
DECAY_LORA = 64
ICLR_LORA = 64
GATE_LORA = 128
RWKV_COLS = 3 * RWKV_WIDTH + DECAY_LORA + ICLR_LORA + GATE_LORA
KV_COLS = KV_HEADS * HEAD_DIM
WINDOW = 128
BLOCK = 128
REL_BUCKETS = 32
REL_MAX_DIST = 128
NORM_EPS = 1e-6
LNX_EPS = 64e-5
KK_EPS = 1e-12
LOG2E = math.log2(math.e)

LANES = 128
MXU_DIM = 256
SUBLANES_F32 = 8
SUBLANES_BF16 = 16
VMEM_LIMIT = 48 * 1024 * 1024

CHUNK = 64
CHUNKS_PER_STEP = 2
PAIR = 2 * HEAD_DIM
assert PAIR == LANES
HEAD_SHIFT = HEAD_DIM.bit_length() - 1
assert 1 << HEAD_SHIFT == HEAD_DIM


def _dot(a, b, precision=None):
    return jnp.dot(a, b, preferred_element_type=F32, precision=precision)


def _dot_nt(a, b, precision=None):
    return lax.dot_general(a, b, (((1,), (1,)), ((), ())), preferred_element_type=F32, precision=precision)


def _dot_tn(a, b, precision=None):
    return lax.dot_general(a, b, (((0,), (0,)), ((), ())), preferred_element_type=F32, precision=precision)


def _sigmoid(x):
    return 1.0 / (1.0 + jnp.exp(-x))


def _split_bf16(x, parts):
    out = []
    for _ in range(parts):
        h = x.astype(BF16)
        out.append(h)
        x = x - h.astype(F32)
    return out


def _head_sum(x, parts):
    width = x.shape[-1]
    blk = min(width, MXU_DIM)
    r = lax.broadcasted_iota(jnp.int32, (blk, blk), 0) >> HEAD_SHIFT
    c = lax.broadcasted_iota(jnp.int32, (blk, blk), 1) >> HEAD_SHIFT
    m = (r == c).astype(BF16)
    pieces = _split_bf16(x, parts)
    cols = []
    for c0 in range(0, width, blk):
        acc = None
        for h in pieces:
            t = _dot(h[:, c0:c0 + blk], m)
            acc = t if acc is None else acc + t
        cols.append(acc)
    return cols[0] if len(cols) == 1 else jnp.concatenate(cols, axis=1)


def _inprep_body(x_ref, xp_ref, xn_ref, g_ref, wr_ref, wq_ref, wkv_ref,
                 mup_ref, mun_ref, w0_ref, w2_ref, a0_ref, a2_ref, g2_ref, kk_ref, ka_ref, rk_ref,
                 q_o, kv_o, r_o, v_o, kk_o, lw0_o, lw1_o, kd0_o, kd1_o, b0_o, b1_o, gate_o, bonus_o,
                 *, tiles_per_seq):
    i = pl.program_id(0)
    tile = x_ref.shape[0]
    halo = xp_ref.shape[0]
    first = (i % tiles_per_seq) == 0
    last = (i % tiles_per_seq) == tiles_per_seq - 1

    def norm(x):
        return (x * lax.rsqrt(jnp.mean(x * x, axis=-1, keepdims=True) + NORM_EPS) * g_ref[...]).astype(BF16)

    h = norm(x_ref[...])
    q_o[...] = _dot(h, wq_ref[...])
    kv_o[...] = _dot(h, wkv_ref[...])
    hp = norm(xp_ref[...])
    hn = norm(xn_ref[...])
    hp = jnp.where(first, jnp.zeros_like(hp), hp)
    hn = jnp.where(last, jnp.zeros_like(hn), hn)
    pcat = _dot(jnp.concatenate([hp, h, hn], axis=0), wr_ref[...])
    total = tile + 2 * halo
    p = pcat[halo:halo + tile]
    prev = pltpu.roll(pcat, 1, 0)[halo:halo + tile]
    nxt = pltpu.roll(pcat, total - 1, 0)[halo:halo + tile]
    pf = p + mup_ref[...] * (prev - p) + mun_ref[...] * (nxt - p)

    c = RWKV_WIDTH
    r, k, v = pf[:, :c], pf[:, c:2 * c], pf[:, 2 * c:3 * c]
    lora = pf[:, 3 * c:3 * c + LANES]
    xg = pf[:, 3 * c + LANES:]
    lora_t = jnp.tanh(lora).astype(BF16)
    lora_b = lora.astype(BF16)

    kx = k * kk_ref[...]
    kk = kx * lax.rsqrt(_head_sum(kx * kx, 1) + KK_EPS)
    gate = _dot(_sigmoid(xg).astype(BF16), g2_ref[...])

    kds = []
    for d, (lw_o, kd_o, b_o) in enumerate(((lw0_o, kd0_o, b0_o), (lw1_o, kd1_o, b1_o))):
        w_raw = w0_ref[d:d + 1, :] + _dot(lora_t, w2_ref[d])
        lw_o[...] = (-math.exp(-0.5)) * _sigmoid(w_raw)
        iclr = _sigmoid(a0_ref[d:d + 1, :] + _dot(lora_b, a2_ref[d]))
        kd = k * (1.0 + (iclr - 1.0) * ka_ref[...])
        kd_o[...] = kd
        b_o[...] = iclr * kk
        kds.append(kd)

    r_o[...] = r
    v_o[...] = v
    kk_o[...] = kk
    gate_o[...] = gate
    bonus_o[...] = _head_sum(r * (kds[0] + kds[1]) * rk_ref[...], 1) * v


def _inproj_prep(x2, g_mix, w_in, mu_prev, mu_next, w0, w2, a0, a2, g2, k_k, k_a, r_k, seq, tile):
    rows, d = x2.shape
    c = RWKV_WIDTH
    cols = RWKV_COLS
    halo = SUBLANES_BF16
    per = tile // halo
    wb = w_in.astype(BF16)
    wr, wq, wkv = wb[:, :cols], wb[:, cols:cols + ATTN_WIDTH], wb[:, cols + ATTN_WIDTH:]
    zeros = jnp.zeros((2, DECAY_LORA, c), F32)
    w2p = jnp.concatenate([w2, zeros], axis=1).astype(BF16)
    a2p = jnp.concatenate([zeros, a2], axis=1).astype(BF16)
    row = lambda i: (i, 0)
    prv = lambda i: (jnp.maximum(i * per - 1, 0), 0)
    nxt = lambda i: (jnp.minimum((i + 1) * per, rows // halo - 1), 0)
    c2 = lambda i: (0, 0)
    c3 = lambda i: (0, 0, 0)
    resident = dict(pipeline_mode=pl.Buffered(1))
    out = jax.ShapeDtypeStruct((rows, c), F32)
    body = functools.partial(_inprep_body, tiles_per_seq=seq // tile)
    return pl.pallas_call(
        body,
        grid=(rows // tile,),
        in_specs=[pl.BlockSpec((tile, d), row), pl.BlockSpec((halo, d), prv), pl.BlockSpec((halo, d), nxt),
                  pl.BlockSpec((1, d), c2),
                  pl.BlockSpec(wr.shape, c2, **resident), pl.BlockSpec(wq.shape, c2, **resident),
                  pl.BlockSpec(wkv.shape, c2, **resident),
                  pl.BlockSpec((1, cols), c2), pl.BlockSpec((1, cols), c2),
                  pl.BlockSpec((2, c), c2), pl.BlockSpec((2, LANES, c), c3),
                  pl.BlockSpec((2, c), c2), pl.BlockSpec((2, LANES, c), c3),
                  pl.BlockSpec((GATE_LORA, c), c2),
                  pl.BlockSpec((1, c), c2), pl.BlockSpec((1, c), c2), pl.BlockSpec((1, c), c2)],
        out_specs=[pl.BlockSpec((tile, ATTN_WIDTH), row), pl.BlockSpec((tile, 2 * KV_COLS), row)]
                  + [pl.BlockSpec((tile, c), row)] * 11,
        out_shape=[jax.ShapeDtypeStruct((rows, ATTN_WIDTH), F32), jax.ShapeDtypeStruct((rows, 2 * KV_COLS), F32)]
                  + [out] * 11,
        compiler_params=pltpu.CompilerParams(dimension_semantics=("parallel",), vmem_limit_bytes=VMEM_LIMIT),
        name="inproj_prep",
    )(x2, x2, x2, g_mix.reshape(1, d), wr, wq, wkv, mu_prev.reshape(1, cols), mu_next.reshape(1, cols),
      w0, w2p, a0, a2p, g2.astype(BF16), k_k.reshape(1, c), k_a.reshape(1, c), r_k.reshape(1, c))


def _scan_step(j, *refs, filler=iter(())):
    h_scr = refs[-1]

    @pl.when(j == 0)
    def _():
        h_scr[...] = jnp.zeros_like(h_scr)

    n_batch, rows = refs[0].shape[0], refs[0].shape[1]
    members = [(bi, d, sub) for bi in range(n_batch) for d in range(2) for sub in range(rows // CHUNK)]
    for _ in _scan_stages(members, *refs):
        next(filler, None)
    for _ in filler:
        pass


def _scan_stages(members, rf, vf, kkf, lwf, kdf, bf, rb, vb, kkb, lwb, kdb, bb, yf_o, yb_o, h_scr):
    cs = CHUNK
    subs = rf.shape[1] // cs
    n_pairs = rf.shape[2] // PAIR

    cs_shift = cs.bit_length() - 1
    assert 1 << cs_shift == cs
    t_row = lax.broadcasted_iota(jnp.int32, (cs, 2 * cs), 0)
    t_col = lax.broadcasted_iota(jnp.int32, (cs, 2 * cs), 1) & (cs - 1)
    eye_cat = (t_row == t_col).astype(F32)
    same_block = lambda log2_size: (t_row >> log2_size) == (t_col >> log2_size)
    ri = lax.broadcasted_iota(jnp.int32, (PAIR, PAIR), 0)
    ci = lax.broadcasted_iota(jnp.int32, (PAIR, PAIR), 1)
    eye = (ri == ci).astype(F32)
    same_head = (ri >> HEAD_SHIFT) == (ci >> HEAD_SHIFT)
    tr = lax.broadcasted_iota(jnp.int32, (cs, cs), 0)
    tc = lax.broadcasted_iota(jnp.int32, (cs, cs), 1)
    lane_even =lax.broadcasted_iota(jnp.int32, (cs, PAIR), 1) < HEAD_DIM
    col_first = lax.broadcasted_iota(jnp.int32, (cs, 2 * cs), 1) < cs

    def stack(x):
        zero = jnp.zeros_like(x)
        return jnp.concatenate([jnp.where(lane_even, x, zero), jnp.where(lane_even, zero, x)], axis=0)

    def block_diag(m):
        zero = jnp.zeros_like(m)
        return jnp.concatenate([jnp.where(col_first, m, zero), jnp.where(col_first, zero, m)], axis=0)

    dirs = ((0, rf, vf, kkf, lwf, kdf, bf, yf_o), (1, rb, vb, kkb, lwb, kdb, bb, yb_o))
    chains = []
    for bi, sub, (d, r_ref, v_ref, kk_ref, lw_ref, kd_ref, b_ref, y_o) in (
            (bi, sub, dirs[d]) for bi, d, sub in members):
        fwd = d == 0
        rows = slice(sub * cs, (sub + 1) * cs)
        strict = (t_col < t_row) if fwd else (t_col > t_row)
        incl = strict | (t_col == t_row)
        cum = ((tc <= tr) if fwd else (tc >= tr)).astype(BF16)

        lw = lw_ref[bi, rows]
        c_in = sum(_dot(cum, piece) for piece in _split_bf16(lw, 2))
        c_ex = c_in - lw
        c_tot = jnp.sum(lw, axis=0, keepdims=True)
        gam = jnp.exp(c_tot)
        kd = kd_ref[bi, rows]
        bv = b_ref[bi, rows]
        e_neg = jnp.exp(-c_in)
        e_end = jnp.exp(c_tot - c_in)
        a_t = -kk_ref[bi, rows] * jnp.exp(c_ex)
        r_t = r_ref[bi, rows] * jnp.exp(c_in)
        b_t = bv * e_neg
        k_t = kd * e_neg
        b_h = bv * e_end
        k_h = kd * e_end
        vv = v_ref[bi, rows]
        for pr in range(n_pairs):
            sl = slice(pr * PAIR, (pr + 1) * PAIR)
            a_b, v_b = a_t[:, sl].astype(BF16), vv[:, sl].astype(BF16)
            chains.append(dict(
                bi=bi, d=d, pr=pr, sl=sl, rows=rows, order=sub if fwd else subs - 1 - sub,
                y_o=y_o, strict=strict, incl=incl, r_t=r_t[:, sl], gam=gam[:, sl],
                a_b=a_b, r_b=r_t[:, sl].astype(BF16), a_sb=stack(a_b), v_b=v_b, v_sb=stack(v_b),
                bt_sb=stack(b_t[:, sl].astype(BF16)), kt_sb=stack(k_t[:, sl].astype(BF16)),
                bk_h=jnp.concatenate([b_h[:, sl].astype(BF16), k_h[:, sl].astype(BF16)], axis=0)))
    yield

    h2 = 2 * cs
    for c in chains:
        lhs = jnp.concatenate([c["a_b"], c["r_b"]], axis=0)
        rhs = jnp.concatenate([c["bt_sb"], c["kt_sb"]], axis=0)
        gram = _dot_nt(lhs, rhs)
        a_ab = jnp.where(c["strict"], gram[:cs, :h2], 0.0)
        c["a_xk"] = jnp.concatenate([jnp.where(c["strict"], gram[:cs, h2:], 0.0),
                                     jnp.where(c["incl"], gram[cs:, h2:], 0.0)], axis=0).astype(BF16)
        c["a_rb"] = jnp.where(c["incl"], gram[cs:, :h2], 0.0).astype(BF16)
        c["n_cat"] = a_ab.astype(BF16)
        c["s_cat"] = (eye_cat + jnp.where(same_block(1), a_ab, 0.0)).astype(BF16)
    yield

    for level in range(1, cs_shift):
        join = same_block(level + 1) & ~same_block(level)
        zero = jnp.zeros((cs, h2), BF16)
        for c in chains:
            tn = _dot(c["s_cat"], block_diag(jnp.where(join, c["n_cat"], zero)))
            c["z_cat"] = (eye_cat + tn).astype(BF16)
        yield
        for c in chains:
            c["s_cat"] = _dot(c["z_cat"], block_diag(c["s_cat"])).astype(BF16)
        yield

    zero_b = jnp.zeros((cs, PAIR), BF16)
    for c in chains:
        av = _dot(c["a_xk"], c["v_sb"])
        c["w1_sb"] = stack(av[:cs].astype(BF16))
        c["av"] = av[cs:]
        c["gam_col"] = jnp.sum(eye * c["gam"], axis=1, keepdims=True)
    yield
    for c in chains:
        c["pq"] = _dot(c["s_cat"], jnp.concatenate([c["a_sb"], c["w1_sb"]], axis=1)).astype(BF16)
    yield
    for c in chains:
        pq = c["pq"]
        ry = _dot(c["a_rb"], jnp.concatenate([stack(pq[:, :PAIR]), stack(pq[:, PAIR:])], axis=1))
        c["r_hat"] = (c["r_t"] + ry[:, :PAIR]).astype(BF16)
        c["y_hat"] = ry[:, PAIR:] + c["av"]
        rhs = jnp.concatenate([pq, jnp.concatenate([zero_b, c["v_b"]], axis=1)], axis=0)
        gd = _dot_tn(c["bk_h"], rhs)
        c["btp"] = jnp.where(same_head, gd[:, :PAIR], 0.0).astype(BF16)
        c["dd"] = jnp.where(same_head, gd[:, PAIR:], 0.0)
    for order in range(subs):
        yield
        for c in (c for c in chains if c["order"] == order):
            h0 = h_scr[c["bi"], c["d"], c["pr"]]
            yh = _dot(jnp.concatenate([c["r_hat"], c["btp"]], axis=0), h0.astype(BF16))
            c["y_o"][c["bi"], c["rows"], c["sl"]] = yh[:cs] + c["y_hat"]
            h_scr[c["bi"], c["d"], c["pr"]] = c["gam_col"] * h0 + yh[cs:] + c["dd"]


def _attn_stages(n, nb, q_ref, rows, kv_window, qg_ref, kg_ref, bias_ref, sink_ref, o_ref):
    blk = rows.stop - rows.start
    head_mean = lambda z: _head_sum(z, 1) * (1.0 / HEAD_DIM)

    k_win = jnp.concatenate([kv[:, :KV_COLS] for kv in kv_window], axis=0)
    v_win = jnp.concatenate([kv[:, KV_COLS:] for kv in kv_window], axis=0)
    kn = k_win * lax.rsqrt(head_mean(k_win * k_win) + NORM_EPS) * kg_ref[...]
    v_b = v_win.astype(BF16)

    lane = lax.broadcasted_iota(jnp.int32, kn.shape, 1)
    k_at = []
    for g in range(KV_HEADS):
        own = jnp.where((lane >> HEAD_SHIFT) == g, kn, 0.0)
        other = pltpu.roll(own, HEAD_DIM, 1)
        k_at.append([own if p == g else other for p in range(2)])

    row = lax.broadcasted_iota(jnp.int32, (blk, 3 * blk), 0)
    col = lax.broadcasted_iota(jnp.int32, (blk, 3 * blk), 1)
    rel = col - blk - row
    valid = (jnp.abs(rel) <= WINDOW)
    valid &= (col >= blk) | (n > 0)
    valid &= (col < 2 * blk) | (n < nb - 1)
    out_lane_even = lax.broadcasted_iota(jnp.int32, (blk, LANES), 1) < HEAD_DIM

    group = Q_HEADS // KV_HEADS
    n_slabs = ATTN_WIDTH // LANES
    slabs_per_group = n_slabs // KV_HEADS
    qn = []
    for s in range(n_slabs):
        q = q_ref[0, rows, s * LANES:(s + 1) * LANES]
        qn.append((q * lax.rsqrt(head_mean(q * q) + NORM_EPS) * qg_ref[...]
                   * (HEAD_DIM ** -0.5 * LOG2E)).astype(BF16))
    scores = []
    for g in range(KV_HEADS):
        kcat = jnp.concatenate([k_at[g][0], k_at[g][1]], axis=0).astype(BF16)
        q_g = jnp.concatenate(qn[g * slabs_per_group:(g + 1) * slabs_per_group], axis=0)
        scores.append(_dot_nt(q_g, kcat))
    yield
    probs, inv_denom = [], []
    for s in range(n_slabs):
        heads = (2 * s, 2 * s + 1)
        g, sg = s // slabs_per_group, s % slabs_per_group
        sc = [jnp.where(valid, scores[g][sg * blk:(sg + 1) * blk, p * 3 * blk:(p + 1) * 3 * blk] + bias_ref[h],
                        -jnp.inf) for p, h in enumerate(heads)]
        sinks = [sink_ref[h] * LOG2E for h in heads]
        m = [jnp.maximum(jnp.max(z, axis=-1, keepdims=True), sk) for z, sk in zip(sc, sinks)]
        e = [jnp.exp2(z - mm) for z, mm in zip(sc, m)]
        inv_denom += [1.0 / (jnp.sum(ee, axis=-1, keepdims=True) + jnp.exp2(sk - mm))
                      for ee, sk, mm in zip(e, sinks, m)]
        probs += [ee.astype(BF16) for ee in e]
        yield
    o_all = _dot(jnp.concatenate(probs, axis=0), v_b)
    for s in range(n_slabs):
        g = (2 * s) // group
        halves = []
        for p in range(2):
            h = 2 * s + p
            o = o_all[h * blk:(h + 1) * blk] * inv_denom[h]
            halves.append(o if p == g else pltpu.roll(o, HEAD_DIM, 1))
        o_ref[0, rows, s * LANES:(s + 1) * LANES] = jnp.where(out_lane_even, halves[0], halves[1])


def _t5_bucket(rel):
    nb = REL_BUCKETS // 2
    max_exact = nb // 2
    ret = jnp.where(rel > 0, nb, 0)
    n = jnp.abs(rel)
    large = max_exact + (jnp.log(jnp.maximum(n, 1).astype(F32) / max_exact)
                         / math.log(REL_MAX_DIST / max_exact) * (nb - max_exact)).astype(jnp.int32)
    large = jnp.minimum(large, nb - 1)
    return ret + jnp.where(n < max_exact, n, large)


BIAS_SPAN = 4 * BLOCK


def _bias_body(tab_ref, o_ref):
    blk = o_ref.shape[1]
    x = jnp.broadcast_to(tab_ref[0], (blk, BIAS_SPAN))
    shifted = pltpu.roll(x, BIAS_SPAN - (blk - 1), 1, stride=1, stride_axis=0)
    o_ref[0] = shifted[:, :3 * blk]


def _bias_table(rel_bias):
    rel = jnp.arange(BIAS_SPAN) - (2 * BLOCK - 1)
    tab = jnp.transpose(rel_bias[_t5_bucket(rel)].astype(F32)) * LOG2E
    shape = (BLOCK, 3 * BLOCK)
    return pl.pallas_call(
        _bias_body,
        grid=(Q_HEADS,),
        in_specs=[pl.BlockSpec((1, 1, BIAS_SPAN), lambda h: (h, 0, 0))],
        out_specs=pl.BlockSpec((1,) + shape, lambda h: (h, 0, 0)),
        out_shape=jax.ShapeDtypeStruct((Q_HEADS,) + shape, F32),
        name="bias_table",
    )(tab.reshape(Q_HEADS, 1, BIAS_SPAN))


def _mixers_body(*refs, n_blocks, per_step):
    scan_refs = refs[:12]
    q_ref, kvp_ref, kvc_ref, kvn_ref, qg_ref, kg_ref, bias_ref, sink_ref = refs[12:20]
    yf_o, yb_o, attn_o, h_scr = refs[20:]
    j = pl.program_id(0)
    kv_blocks = ([kvp_ref[0]] + [kvc_ref[0, s * BLOCK:(s + 1) * BLOCK] for s in range(per_step)] + [kvn_ref[0]])

    def attention():
        for s in range(per_step):
            n = (j * per_step + s) % n_blocks
            yield from _attn_stages(n, n_blocks, q_ref, slice(s * BLOCK, (s + 1) * BLOCK), kv_blocks[s:s + 3],
                                    qg_ref, kg_ref, bias_ref, sink_ref, attn_o)

    _scan_step(j, *scan_refs, yf_o, yb_o, h_scr, filler=attention())


def _mixers(r, v, kk, lw0, lw1, kd0, kd1, b0, b1, q, kv, q_gain, k_gain, rel_bias, sink):
    b, t, c = r.shape
    steps = t // (CHUNKS_PER_STEP * CHUNK)
    nb = t // BLOCK
    per_step = (b * nb) // steps
    assert per_step * steps == b * nb and nb % per_step == 0
    nq = nb // per_step
    fw = lambda j: (0, j, 0)
    bw = lambda j: (0, steps - 1 - j, 0)
    blk = (b, CHUNKS_PER_STEP * CHUNK, c)
    cur = lambda j: (j // nq, j % nq, 0)
    prv = lambda j: (j // nq, jnp.maximum((j % nq) * per_step - 1, 0), 0)
    nxt = lambda j: (j // nq, jnp.minimum((j % nq + 1) * per_step, nb - 1), 0)
    c2 = lambda j: (0, 0)
    c3 = lambda j: (0, 0, 0)
    kvblk = (1, BLOCK, 2 * KV_COLS)
    return pl.pallas_call(
        functools.partial(_mixers_body, n_blocks=nb, per_step=per_step),
        grid=(steps,),
        in_specs=[pl.BlockSpec(blk, fw)] * 6 + [pl.BlockSpec(blk, bw)] * 6
                 + [pl.BlockSpec((1, per_step * BLOCK, ATTN_WIDTH), cur), pl.BlockSpec(kvblk, prv),
                    pl.BlockSpec((1, per_step * BLOCK, 2 * KV_COLS), cur), pl.BlockSpec(kvblk, nxt),
                    pl.BlockSpec((1, LANES), c2), pl.BlockSpec((1, LANES), c2),
                    pl.BlockSpec((Q_HEADS, BLOCK, 3 * BLOCK), c3), pl.BlockSpec(memory_space=pltpu.SMEM)],
        out_specs=[pl.BlockSpec(blk, fw), pl.BlockSpec(blk, bw),
                   pl.BlockSpec((1, per_step * BLOCK, ATTN_WIDTH), cur)],
        out_shape=[jax.ShapeDtypeStruct((b, t, c), F32)] * 2 + [jax.ShapeDtypeStruct((b, t, ATTN_WIDTH), F32)],
        scratch_shapes=[pltpu.VMEM((b, 2, c // PAIR, PAIR, PAIR), F32)],
        compiler_params=pltpu.CompilerParams(dimension_semantics=("arbitrary",), vmem_limit_bytes=VMEM_LIMIT),
        name="mixers",
    )(r, v, kk, lw0, kd0, b0, r, v, kk, lw1, kd1, b1,
      q, kv, kv, kv, jnp.tile(q_gain, 2).reshape(1, LANES), jnp.tile(k_gain, 2).reshape(1, LANES),
      _bias_table(rel_bias), sink.astype(F32))


def _outffn_body(*refs, tiles_per_seq, ff_chunk):
    main, prev, nxt = refs[0:6], refs[6:12], refs[12:18]
    lw_ref, lb_ref, wo_ref, gf_ref, wu_ref, cw_ref, cb_ref, wd_ref, o_ref, act_scr = refs[18:]
    i = pl.program_id(0)
    tile = main[0].shape[0]
    halo = prev[0].shape[0]
    d_ff = wd_ref.shape[0]
    c = RWKV_WIDTH
    first = (i % tiles_per_seq) == 0
    last = (i % tiles_per_seq) == tiles_per_seq - 1

    def mixed(x_ref, yf_ref, yb_ref, bonus_ref, gate_ref, attn_ref):
        y = yf_ref[...] + yb_ref[...]
        mu = _head_sum(y, 2) * (1.0 / HEAD_DIM)
        yc = y - mu
        var = _head_sum(yc * yc, 1) * (1.0 / HEAD_DIM)
        yn = yc * lax.rsqrt(var + LNX_EPS) * lw_ref[...] + lb_ref[...]
        mix_r = (yn + bonus_ref[...]) * gate_ref[...]
        return (x_ref[...] + _dot(mix_r.astype(BF16), wo_ref[:c, :])
                + _dot(attn_ref[...].astype(BF16), wo_ref[c:, :]))

    x1 = mixed(*main)
    x1cat = jnp.concatenate([mixed(*prev), x1, mixed(*nxt)], axis=0)
    total = tile + 2 * halo
    hcat = x1cat * lax.rsqrt(jnp.mean(x1cat * x1cat, axis=-1, keepdims=True) + NORM_EPS) * gf_ref[...]
    row = lax.broadcasted_iota(jnp.int32, (total, 1), 0)
    outside = (first & (row < halo)) | (last & (row >= halo + tile))
    hcat = jnp.where(outside, 0.0, hcat).astype(BF16)

    def conv(cols):
        u = _dot(hcat, wu_ref[:, cols])
        up = pltpu.roll(u, 1, 0)[halo:halo + tile]
        un = pltpu.roll(u, total - 1, 0)[halo:halo + tile]
        return (up * cw_ref[0:1, cols] + u[halo:halo + tile] * cw_ref[1:2, cols] + un * cw_ref[2:3, cols]
                + cb_ref[:, cols])

    for c in range(d_ff // ff_chunk):
        g = conv(slice(c * ff_chunk, (c + 1) * ff_chunk))
        val = conv(slice(d_ff + c * ff_chunk, d_ff + (c + 1) * ff_chunk))
        act_scr[:, c * ff_chunk:(c + 1) * ff_chunk] = (g * _sigmoid(g) * val).astype(BF16)
    o_ref[...] = x1 + _dot(act_scr[...], wd_ref[...])


def _out_ffn(x2, yf, yb, bonus, gate, attn, lnx_w, lnx_b, w_out, g_ffn, w_up, conv_w, conv_b, w_down,
             seq, tile, ff_chunk):
    rows, d = x2.shape
    c = RWKV_WIDTH
    d_ff = w_down.shape[0]
    halo = SUBLANES_F32
    per = tile // halo
    row = lambda i: (i, 0)
    prv = lambda i: (jnp.maximum(i * per - 1, 0), 0)
    nxt = lambda i: (jnp.minimum((i + 1) * per, rows // halo - 1), 0)
    const = lambda i: (0, 0)
    resident = dict(pipeline_mode=pl.Buffered(1))
    streams = (x2, yf, yb, bonus, gate, attn)

    def stream_specs(nrows, index_map):
        return [pl.BlockSpec((nrows, z.shape[1]), index_map) for z in streams]

    body = functools.partial(_outffn_body, tiles_per_seq=seq // tile, ff_chunk=ff_chunk)
    return pl.pallas_call(
        body,
        grid=(rows // tile,),
        in_specs=stream_specs(tile, row) + stream_specs(halo, prv) + stream_specs(halo, nxt)
                 + [pl.BlockSpec((1, c), const), pl.BlockSpec((1, c), const),
                    pl.BlockSpec(w_out.shape, const, **resident), pl.BlockSpec((1, d), const),
                    pl.BlockSpec((d, 2 * d_ff), const, **resident),
                    pl.BlockSpec((3, 2 * d_ff), const), pl.BlockSpec((1, 2 * d_ff), const),
                    pl.BlockSpec((d_ff, d), const, **resident)],
        out_specs=pl.BlockSpec((tile, d), row),
        out_shape=jax.ShapeDtypeStruct((rows, d), F32),
        scratch_shapes=[pltpu.VMEM((tile, d_ff), BF16)],
        compiler_params=pltpu.CompilerParams(dimension_semantics=("parallel",), vmem_limit_bytes=VMEM_LIMIT),
        name="out_ffn",
    )(*streams, *streams, *streams, lnx_w.reshape(1, c), lnx_b.reshape(1, c), w_out.astype(BF16),
      g_ffn.reshape(1, d), w_up.astype(BF16), conv_w, conv_b.reshape(1, 2 * d_ff), w_down.astype(BF16))


def _layer(x, g_mix, w_in, mu_prev, mu_next, w0, w2, a0, a2, g2, k_k, k_a, r_k, lnx_w, lnx_b,
           q_gain, k_gain, rel_bias, sink, w_out, g_ffn, w_up, conv_w, conv_b, w_down):
    b, t, d = x.shape
    rows = b * t
    x2 = x.reshape(rows, d)
    q, kv, *ops = _inproj_prep(x2, g_mix, w_in, mu_prev, mu_next, w0, w2, a0, a2, g2, k_k, k_a, r_k.reshape(-1),
                               seq=t, tile=min(256, t))
    r, v, kk, lw0, lw1, kd0, kd1, b0, b1 = (z.reshape(b, t, RWKV_WIDTH) for z in ops[:9])
    gate, bonus = ops[9:]
    yf, yb, attn = _mixers(r, v, kk, lw0, lw1, kd0, kd1, b0, b1, q.reshape(b, t, ATTN_WIDTH),
                           kv.reshape(b, t, 2 * KV_COLS), q_gain, k_gain, rel_bias, sink)
    flat = lambda z: z.reshape(rows, z.shape[-1])
    out = _out_ffn(x2, flat(yf), flat(yb), bonus, gate, flat(attn), lnx_w, lnx_b, w_out, g_ffn,
                   w_up, conv_w, conv_b, w_down, seq=t, tile=min(512, t), ff_chunk=256)
    return out.reshape(b, t, d)


def kernel(x, g_mix, w_in, mu_prev, mu_next, w0, w2, a0, a2, g2, k_k, k_a, r_k, lnx_w, lnx_b, q_gain, k_gain,
           rel_bias, sink, w_out, g_ffn, w_up, conv_w, conv_b, w_down):
    depth = g_mix.shape[0]
    for l in range(depth):
        x = _layer(x, g_mix[l], w_in[l], mu_prev[l], mu_next[l], w0[l], w2[l], a0[l], a2[l], g2[l], k_k[l], k_a[l],
                   r_k[l], lnx_w[l], lnx_b[l], q_gain[l], k_gain[l], rel_bias, sink[l], w_out[l], g_ffn[l],
                   w_up[l], conv_w[l], conv_b[l], w_down[l])
    return x
```

```python
import functools
import math

import jax
import jax.numpy as jnp
from jax import lax
from jax.experimental import pallas as pl
from jax.experimental.pallas import tpu as pltpu

F32 = jnp.float32
BF16 = jnp.bfloat16

HEAD_DIM = 64
RWKV_WIDTH = 512
ATTN_WIDTH = 512
KV_HEADS = 2
Q_HEADS = 8
DECAY_LORA = 64
ICLR_LORA = 64
GATE_LORA = 128
RWKV_COLS = 3 * RWKV_WIDTH + DECAY_LORA + ICLR_LORA + GATE_LORA
KV_COLS = KV_HEADS * HEAD_DIM
WINDOW = 128
BLOCK = 128
REL_BUCKETS = 32
REL_MAX_DIST = 128
NORM_EPS = 1e-6
LNX_EPS = 64e-5
KK_EPS = 1e-12
LOG2E = math.log2(math.e)

LANES = 128
MXU_DIM = 256
SUBLANES_F32 = 8
SUBLANES_BF16 = 16
VMEM_LIMIT = 48 * 1024 * 1024

CHUNK = 64
CHUNKS_PER_STEP = 2
PAIR = 2 * HEAD_DIM
assert PAIR == LANES
HEAD_SHIFT = HEAD_DIM.bit_length() - 1
assert 1 << HEAD_SHIFT == HEAD_DIM


def _dot(a, b, precision=None):
    return jnp.dot(a, b, preferred_element_type=F32, precision=precision)


def _dot_nt(a, b, precision=None):
    return lax.dot_general(a, b, (((1,), (1,)), ((), ())), preferred_element_type=F32, precision=precision)


def _dot_tn(a, b, precision=None):
    return lax.dot_general(a, b, (((0,), (0,)), ((), ())), preferred_element_type=F32, precision=precision)


def _sigmoid(x):
    return 1.0 / (1.0 + jnp.exp(-x))


def _split_bf16(x, parts):
    out = []
    for _ in range(parts):
        h = x.astype(BF16)
        out.append(h)
        x = x - h.astype(F32)
    return out


def _head_sum(x, parts):
    width = x.shape[-1]
    blk = min(width, MXU_DIM)
    r = lax.broadcasted_iota(jnp.int32, (blk, blk), 0) >> HEAD_SHIFT
    c = lax.broadcasted_iota(jnp.int32, (blk, blk), 1) >> HEAD_SHIFT
    m = (r == c).astype(BF16)
    pieces = _split_bf16(x, parts)
    cols = []
    for c0 in range(0, width, blk):
        acc = None
        for h in pieces:
            t = _dot(h[:, c0:c0 + blk], m)
            acc = t if acc is None else acc + t
        cols.append(acc)
    return cols[0] if len(cols) == 1 else jnp.concatenate(cols, axis=1)


def _inprep_body(x_ref, xp_ref, xn_ref, g_ref, wr_ref, wq_ref, wkv_ref,
                 mup_ref, mun_ref, w0_ref, w2_ref, a0_ref, a2_ref, g2_ref, kk_ref, ka_ref, rk_ref,
                 q_o, kv_o, r_o, v_o, kk_o, lw0_o, lw1_o, kd0_o, kd1_o, b0_o, b1_o, gate_o, bonus_o,
                 *, tiles_per_seq):
    i = pl.program_id(0)
    tile = x_ref.shape[0]
    halo = xp_ref.shape[0]
    first = (i % tiles_per_seq) == 0
    last = (i % tiles_per_seq) == tiles_per_seq - 1

    def norm(x):
        return (x * lax.rsqrt(jnp.mean(x * x, axis=-1, keepdims=True) + NORM_EPS) * g_ref[...]).astype(BF16)

    h = norm(x_ref[...])
    q_o[...] = _dot(h, wq_ref[...])
    kv_o[...] = _dot(h, wkv_ref[...])
    hp = norm(xp_ref[...])
    hn = norm(xn_ref[...])
    hp = jnp.where(first, jnp.zeros_like(hp), hp)
    hn = jnp.where(last, jnp.zeros_like(hn), hn)
    hcat = jnp.concatenate([hp, h, hn], axis=0)
    total = tile + 2 * halo

    def shifted(lo, hi):
        pc = _dot(hcat, wr_ref[:, lo:hi])
        p = pc[halo:halo + tile]
        prev = pltpu.roll(pc, 1, 0)[halo:halo + tile]
        nxt = pltpu.roll(pc, total - 1, 0)[halo:halo + tile]
        return p + mup_ref[:, lo:hi] * (prev - p) + mun_ref[:, lo:hi] * (nxt - p)

    c = RWKV_WIDTH
    r, k, v = shifted(0, c), shifted(c, 2 * c), shifted(2 * c, 3 * c)
    tail = shifted(3 * c, RWKV_COLS)
    lora = tail[:, :LANES]
    xg = tail[:, LANES:]
    lora_t = jnp.tanh(lora).astype(BF16)
    lora_b = lora.astype(BF16)

    kx = k * kk_ref[...]
    kk = kx * lax.rsqrt(_head_sum(kx * kx, 1) + KK_EPS)
    gate = _dot(_sigmoid(xg).astype(BF16), g2_ref[...])

    kds = []
    for d, (lw_o, kd_o, b_o) in enumerate(((lw0_o, kd0_o, b0_o), (lw1_o, kd1_o, b1_o))):
        w_raw = w0_ref[d:d + 1, :] + _dot(lora_t, w2_ref[d])
        lw_o[...] = (-math.exp(-0.5)) * _sigmoid(w_raw)
        iclr = _sigmoid(a0_ref[d:d + 1, :] + _dot(lora_b, a2_ref[d]))
        kd = k * (1.0 + (iclr - 1.0) * ka_ref[...])
        kd_o[...] = kd
        b_o[...] = iclr * kk
        kds.append(kd)

    r_o[...] = r
    v_o[...] = v
    kk_o[...] = kk
    gate_o[...] = gate
    bonus_o[...] = _head_sum(r * (kds[0] + kds[1]) * rk_ref[...], 1) * v


def _inproj_prep(x2, g_mix, w_in, mu_prev, mu_next, w0, w2, a0, a2, g2, k_k, k_a, r_k, seq, tile):
    rows, d = x2.shape
    c = RWKV_WIDTH
    cols = RWKV_COLS
    halo = SUBLANES_BF16
    per = tile // halo
    wb = w_in.astype(BF16)
    wr, wq, wkv = wb[:, :cols], wb[:, cols:cols + ATTN_WIDTH], wb[:, cols + ATTN_WIDTH:]
    zeros = jnp.zeros((2, DECAY_LORA, c), F32)
    w2p = jnp.concatenate([w2, zeros], axis=1).astype(BF16)
    a2p = jnp.concatenate([zeros, a2], axis=1).astype(BF16)
    row = lambda i: (i, 0)
    prv = lambda i: (jnp.maximum(i * per - 1, 0), 0)
    nxt = lambda i: (jnp.minimum((i + 1) * per, rows // halo - 1), 0)
    c2 = lambda i: (0, 0)
    c3 = lambda i: (0, 0, 0)
    resident = dict(pipeline_mode=pl.Buffered(1))
    out = jax.ShapeDtypeStruct((rows, c), F32)
    body = functools.partial(_inprep_body, tiles_per_seq=seq // tile)
    return pl.pallas_call(
        body,
        grid=(rows // tile,),
        in_specs=[pl.BlockSpec((tile, d), row), pl.BlockSpec((halo, d), prv), pl.BlockSpec((halo, d), nxt),
                  pl.BlockSpec((1, d), c2),
                  pl.BlockSpec(wr.shape, c2, **resident), pl.BlockSpec(wq.shape, c2, **resident),
                  pl.BlockSpec(wkv.shape, c2, **resident),
                  pl.BlockSpec((1, cols), c2), pl.BlockSpec((1, cols), c2),
                  pl.BlockSpec((2, c), c2), pl.BlockSpec((2, LANES, c), c3),
                  pl.BlockSpec((2, c), c2), pl.BlockSpec((2, LANES, c), c3),
                  pl.BlockSpec((GATE_LORA, c), c2),
                  pl.BlockSpec((1, c), c2), pl.BlockSpec((1, c), c2), pl.BlockSpec((1, c), c2)],
        out_specs=[pl.BlockSpec((tile, ATTN_WIDTH), row), pl.BlockSpec((tile, 2 * KV_COLS), row)]
                  + [pl.BlockSpec((tile, c), row)] * 11,
        out_shape=[jax.ShapeDtypeStruct((rows, ATTN_WIDTH), F32), jax.ShapeDtypeStruct((rows, 2 * KV_COLS), F32)]
                  + [out] * 11,
        compiler_params=pltpu.CompilerParams(dimension_semantics=("parallel",), vmem_limit_bytes=VMEM_LIMIT),
        name="inproj_prep",
    )(x2, x2, x2, g_mix.reshape(1, d), wr, wq, wkv, mu_prev.reshape(1, cols), mu_next.reshape(1, cols),
      w0, w2p, a0, a2p, g2.astype(BF16), k_k.reshape(1, c), k_a.reshape(1, c), r_k.reshape(1, c))


def _scan_step(j, *refs, filler=iter(())):
    h_scr = refs[-1]

    @pl.when(j == 0)
    def _():
        h_scr[...] = jnp.zeros_like(h_scr)

    n_batch, rows = refs[0].shape[0], refs[0].shape[1]
    members = [(bi, d, sub) for bi in range(n_batch) for d in range(2) for sub in range(rows // CHUNK)]
    for _ in _scan_stages(members, *refs):
        next(filler, None)
    for _ in filler:
        pass


def _scan_stages(members, rf, vf, kkf, lwf, kdf, bf, rb, vb, kkb, lwb, kdb, bb, yf_o, yb_o, h_scr):
    cs = CHUNK
    subs = rf.shape[1] // cs
    n_pairs = rf.shape[2] // PAIR

    cs_shift = cs.bit_length() - 1
    assert 1 << cs_shift == cs
    t_row = lax.broadcasted_iota(jnp.int32, (cs, 2 * cs), 0)
    t_col = lax.broadcasted_iota(jnp.int32, (cs, 2 * cs), 1) & (cs - 1)
    eye_cat = (t_row == t_col).astype(F32)
    same_block = lambda log2_size: (t_row >> log2_size) == (t_col >> log2_size)
    ri = lax.broadcasted_iota(jnp.int32, (PAIR, PAIR), 0)
    ci = lax.broadcasted_iota(jnp.int32, (PAIR, PAIR), 1)
    eye = (ri == ci).astype(F32)
    same_head = (ri >> HEAD_SHIFT) == (ci >> HEAD_SHIFT)
    tr = lax.broadcasted_iota(jnp.int32, (cs, cs), 0)
    tc = lax.broadcasted_iota(jnp.int32, (cs, cs), 1)
    lane_even =lax.broadcasted_iota(jnp.int32, (cs, PAIR), 1) < HEAD_DIM
    col_first = lax.broadcasted_iota(jnp.int32, (cs, 2 * cs), 1) < cs

    def stack(x):
        zero = jnp.zeros_like(x)
        return jnp.concatenate([jnp.where(lane_even, x, zero), jnp.where(lane_even, zero, x)], axis=0)

    def block_diag(m):
        zero = jnp.zeros_like(m)
        return jnp.concatenate([jnp.where(col_first, m, zero), jnp.where(col_first, zero, m)], axis=0)

    dirs = ((0, rf, vf, kkf, lwf, kdf, bf, yf_o), (1, rb, vb, kkb, lwb, kdb, bb, yb_o))
    chains = []
    for bi, sub, (d, r_ref, v_ref, kk_ref, lw_ref, kd_ref, b_ref, y_o) in (
            (bi, sub, dirs[d]) for bi, d, sub in members):
        fwd = d == 0
        rows = slice(sub * cs, (sub + 1) * cs)
        strict = (t_col < t_row) if fwd else (t_col > t_row)
        incl = strict | (t_col == t_row)
        cum = ((tc <= tr) if fwd else (tc >= tr)).astype(BF16)

        lw = lw_ref[bi, rows]
        c_in = sum(_dot(cum, piece) for piece in _split_bf16(lw, 2))
        c_ex = c_in - lw
        c_tot = jnp.sum(lw, axis=0, keepdims=True)
        gam = jnp.exp(c_tot)
        kd = kd_ref[bi, rows]
        bv = b_ref[bi, rows]
        e_neg = jnp.exp(-c_in)
        e_end = jnp.exp(c_tot - c_in)
        a_t = -kk_ref[bi, rows] * jnp.exp(c_ex)
        r_t = r_ref[bi, rows] * jnp.exp(c_in)
        b_t = bv * e_neg
        k_t = kd * e_neg
        b_h = bv * e_end
        k_h = kd * e_end
        vv = v_ref[bi, rows]
        for pr in range(n_pairs):
            sl = slice(pr * PAIR, (pr + 1) * PAIR)
            a_b, v_b = a_t[:, sl].astype(BF16), vv[:, sl].astype(BF16)
            chains.append(dict(
                bi=bi, d=d, pr=pr, sl=sl, rows=rows, order=sub if fwd else subs - 1 - sub,
                y_o=y_o, strict=strict, incl=incl, r_t=r_t[:, sl], gam=gam[:, sl],
                a_b=a_b, r_b=r_t[:, sl].astype(BF16), a_sb=stack(a_b), v_b=v_b, v_sb=stack(v_b),
                bt_sb=stack(b_t[:, sl].astype(BF16)), kt_sb=stack(k_t[:, sl].astype(BF16)),
                bk_h=jnp.concatenate([b_h[:, sl].astype(BF16), k_h[:, sl].astype(BF16)], axis=0)))
    yield

    h2 = 2 * cs
    for c in chains:
        lhs = jnp.concatenate([c["a_b"], c["r_b"]], axis=0)
        rhs = jnp.concatenate([c["bt_sb"], c["kt_sb"]], axis=0)
        gram = _dot_nt(lhs, rhs)
        a_ab = jnp.where(c["strict"], gram[:cs, :h2], 0.0)
        c["a_xk"] = jnp.concatenate([jnp.where(c["strict"], gram[:cs, h2:], 0.0),
                                     jnp.where(c["incl"], gram[cs:, h2:], 0.0)], axis=0).astype(BF16)
        c["a_rb"] = jnp.where(c["incl"], gram[cs:, :h2], 0.0).astype(BF16)
        c["n_cat"] = a_ab.astype(BF16)
        c["s_cat"] = (eye_cat + jnp.where(same_block(1), a_ab, 0.0)).astype(BF16)
    yield

    for level in range(1, cs_shift):
        join = same_block(level + 1) & ~same_block(level)
        zero = jnp.zeros((cs, h2), BF16)
        for c in chains:
            tn = _dot(c["s_cat"], block_diag(jnp.where(join, c["n_cat"], zero)))
            c["z_cat"] = (eye_cat + tn).astype(BF16)
        yield
        for c in chains:
            c["s_cat"] = _dot(c["z_cat"], block_diag(c["s_cat"])).astype(BF16)
        yield

    zero_b = jnp.zeros((cs, PAIR), BF16)
    for c in chains:
        av = _dot(c["a_xk"], c["v_sb"])
        c["w1_sb"] = stack(av[:cs].astype(BF16))
        c["av"] = av[cs:]
        c["gam_col"] = jnp.sum(eye * c["gam"], axis=1, keepdims=True)
    yield
    for c in chains:
        c["pq"] = _dot(c["s_cat"], jnp.concatenate([c["a_sb"], c["w1_sb"]], axis=1)).astype(BF16)
    yield
    for c in chains:
        pq = c["pq"]
        ry = _dot(c["a_rb"], jnp.concatenate([stack(pq[:, :PAIR]), stack(pq[:, PAIR:])], axis=1))
        c["r_hat"] = (c["r_t"] + ry[:, :PAIR]).astype(BF16)
        c["y_hat"] = ry[:, PAIR:] + c["av"]
        rhs = jnp.concatenate([pq, jnp.concatenate([zero_b, c["v_b"]], axis=1)], axis=0)
        gd = _dot_tn(c["bk_h"], rhs)
        c["btp"] = jnp.where(same_head, gd[:, :PAIR], 0.0).astype(BF16)
        c["dd"] = jnp.where(same_head, gd[:, PAIR:], 0.0)
    for order in range(subs):
        yield
        for c in (c for c in chains if c["order"] == order):
            h0 = h_scr[c["bi"], c["d"], c["pr"]]
            yh = _dot(jnp.concatenate([c["r_hat"], c["btp"]], axis=0), h0.astype(BF16))
            c["y_o"][c["bi"], c["rows"], c["sl"]] = yh[:cs] + c["y_hat"]
            h_scr[c["bi"], c["d"], c["pr"]] = c["gam_col"] * h0 + yh[cs:] + c["dd"]


def _attn_stages(n, nb, q_ref, rows, kv_window, qg_ref, kg_ref, bias_ref, sink_ref, o_ref):
    blk = rows.stop - rows.start
    head_mean = lambda z: _head_sum(z, 1) * (1.0 / HEAD_DIM)

    k_win = jnp.concatenate([kv[:, :KV_COLS] for kv in kv_window], axis=0)
    v_win = jnp.concatenate([kv[:, KV_COLS:] for kv in kv_window], axis=0)
    kn = k_win * lax.rsqrt(head_mean(k_win * k_win) + NORM_EPS) * kg_ref[...]
    v_b = v_win.astype(BF16)

    lane = lax.broadcasted_iota(jnp.int32, kn.shape, 1)
    k_at = []
    for g in range(KV_HEADS):
        own = jnp.where((lane >> HEAD_SHIFT) == g, kn, 0.0)
        other = pltpu.roll(own, HEAD_DIM, 1)
        k_at.append([own if p == g else other for p in range(2)])

    row = lax.broadcasted_iota(jnp.int32, (blk, 3 * blk), 0)
    col = lax.broadcasted_iota(jnp.int32, (blk, 3 * blk), 1)
    rel = col - blk - row
    valid = (jnp.abs(rel) <= WINDOW)
    valid &= (col >= blk) | (n > 0)
    valid &= (col < 2 * blk) | (n < nb - 1)
    out_lane_even = lax.broadcasted_iota(jnp.int32, (blk, LANES), 1) < HEAD_DIM

    group = Q_HEADS // KV_HEADS
    n_slabs = ATTN_WIDTH // LANES
    slabs_per_group = n_slabs // KV_HEADS
    qn = []
    for s in range(n_slabs):
        q = q_ref[0, rows, s * LANES:(s + 1) * LANES]
        qn.append((q * lax.rsqrt(head_mean(q * q) + NORM_EPS) * qg_ref[...]
                   * (HEAD_DIM ** -0.5 * LOG2E)).astype(BF16))
    scores = []
    for g in range(KV_HEADS):
        kcat = jnp.concatenate([k_at[g][0], k_at[g][1]], axis=0).astype(BF16)
        q_g = jnp.concatenate(qn[g * slabs_per_group:(g + 1) * slabs_per_group], axis=0)
        scores.append(_dot_nt(q_g, kcat))
    yield
    probs, inv_denom = [], []
    for s in range(n_slabs):
        heads = (2 * s, 2 * s + 1)
        g, sg = s // slabs_per_group, s % slabs_per_group
        sc = [jnp.where(valid, scores[g][sg * blk:(sg + 1) * blk, p * 3 * blk:(p + 1) * 3 * blk] + bias_ref[h],
                        -jnp.inf) for p, h in enumerate(heads)]
        sinks = [sink_ref[h] * LOG2E for h in heads]
        m = [jnp.maximum(jnp.max(z, axis=-1, keepdims=True), sk) for z, sk in zip(sc, sinks)]
        e = [jnp.exp2(z - mm) for z, mm in zip(sc, m)]
        inv_denom += [1.0 / (jnp.sum(ee, axis=-1, keepdims=True) + jnp.exp2(sk - mm))
                      for ee, sk, mm in zip(e, sinks, m)]
        probs += [ee.astype(BF16) for ee in e]
        yield
    o_all = _dot(jnp.concatenate(probs, axis=0), v_b)
    for s in range(n_slabs):
        g = (2 * s) // group
        halves = []
        for p in range(2):
            h = 2 * s + p
            o = o_all[h * blk:(h + 1) * blk] * inv_denom[h]
            halves.append(o if p == g else pltpu.roll(o, HEAD_DIM, 1))
        o_ref[0, rows, s * LANES:(s + 1) * LANES] = jnp.where(out_lane_even, halves[0], halves[1])


def _t5_bucket(rel):
    nb = REL_BUCKETS // 2
    max_exact = nb // 2
    ret = jnp.where(rel > 0, nb, 0)
    n = jnp.abs(rel)
    large = max_exact + (jnp.log(jnp.maximum(n, 1).astype(F32) / max_exact)
                         / math.log(REL_MAX_DIST / max_exact) * (nb - max_exact)).astype(jnp.int32)
    large = jnp.minimum(large, nb - 1)
    return ret + jnp.where(n < max_exact, n, large)


BIAS_SPAN = 4 * BLOCK


def _bias_body(tab_ref, o_ref):
    blk = o_ref.shape[1]
    x = jnp.broadcast_to(tab_ref[0], (blk, BIAS_SPAN))
    shifted = pltpu.roll(x, BIAS_SPAN - (blk - 1), 1, stride=1, stride_axis=0)
    o_ref[0] = shifted[:, :3 * blk]


def _bias_table(rel_bias):
    rel = jnp.arange(BIAS_SPAN) - (2 * BLOCK - 1)
    tab = jnp.transpose(rel_bias[_t5_bucket(rel)].astype(F32)) * LOG2E
    shape = (BLOCK, 3 * BLOCK)
    return pl.pallas_call(
        _bias_body,
        grid=(Q_HEADS,),
        in_specs=[pl.BlockSpec((1, 1, BIAS_SPAN), lambda h: (h, 0, 0))],
        out_specs=pl.BlockSpec((1,) + shape, lambda h: (h, 0, 0)),
        out_shape=jax.ShapeDtypeStruct((Q_HEADS,) + shape, F32),
        name="bias_table",
    )(tab.reshape(Q_HEADS, 1, BIAS_SPAN))


def _mixers_body(*refs, n_blocks, per_step):
    scan_refs = refs[:12]
    q_ref, kvp_ref, kvc_ref, kvn_ref, qg_ref, kg_ref, bias_ref, sink_ref = refs[12:20]
    yf_o, yb_o, attn_o, h_scr = refs[20:]
    j = pl.program_id(0)
    kv_blocks = ([kvp_ref[0]] + [kvc_ref[0, s * BLOCK:(s + 1) * BLOCK] for s in range(per_step)] + [kvn_ref[0]])

    def attention():
        for s in range(per_step):
            n = (j * per_step + s) % n_blocks
            yield from _attn_stages(n, n_blocks, q_ref, slice(s * BLOCK, (s + 1) * BLOCK), kv_blocks[s:s + 3],
                                    qg_ref, kg_ref, bias_ref, sink_ref, attn_o)

    _scan_step(j, *scan_refs, yf_o, yb_o, h_scr, filler=attention())


def _mixers(r, v, kk, lw0, lw1, kd0, kd1, b0, b1, q, kv, q_gain, k_gain, rel_bias, sink):
    b, t, c = r.shape
    steps = t // (CHUNKS_PER_STEP * CHUNK)
    nb = t // BLOCK
    per_step = (b * nb) // steps
    assert per_step * steps == b * nb and nb % per_step == 0
    nq = nb // per_step
    fw = lambda j: (0, j, 0)
    bw = lambda j: (0, steps - 1 - j, 0)
    blk = (b, CHUNKS_PER_STEP * CHUNK, c)
    cur = lambda j: (j // nq, j % nq, 0)
    prv = lambda j: (j // nq, jnp.maximum((j % nq) * per_step - 1, 0), 0)
    nxt = lambda j: (j // nq, jnp.minimum((j % nq + 1) * per_step, nb - 1), 0)
    c2 = lambda j: (0, 0)
    c3 = lambda j: (0, 0, 0)
    kvblk = (1, BLOCK, 2 * KV_COLS)
    return pl.pallas_call(
        functools.partial(_mixers_body, n_blocks=nb, per_step=per_step),
        grid=(steps,),
        in_specs=[pl.BlockSpec(blk, fw)] * 6 + [pl.BlockSpec(blk, bw)] * 6
                 + [pl.BlockSpec((1, per_step * BLOCK, ATTN_WIDTH), cur), pl.BlockSpec(kvblk, prv),
                    pl.BlockSpec((1, per_step * BLOCK, 2 * KV_COLS), cur), pl.BlockSpec(kvblk, nxt),
                    pl.BlockSpec((1, LANES), c2), pl.BlockSpec((1, LANES), c2),
                    pl.BlockSpec((Q_HEADS, BLOCK, 3 * BLOCK), c3), pl.BlockSpec(memory_space=pltpu.SMEM)],
        out_specs=[pl.BlockSpec(blk, fw), pl.BlockSpec(blk, bw),
                   pl.BlockSpec((1, per_step * BLOCK, ATTN_WIDTH), cur)],
        out_shape=[jax.ShapeDtypeStruct((b, t, c), F32)] * 2 + [jax.ShapeDtypeStruct((b, t, ATTN_WIDTH), F32)],
        scratch_shapes=[pltpu.VMEM((b, 2, c // PAIR, PAIR, PAIR), F32)],
        compiler_params=pltpu.CompilerParams(dimension_semantics=("arbitrary",), vmem_limit_bytes=VMEM_LIMIT),
        name="mixers",
    )(r, v, kk, lw0, kd0, b0, r, v, kk, lw1, kd1, b1,
      q, kv, kv, kv, jnp.tile(q_gain, 2).reshape(1, LANES), jnp.tile(k_gain, 2).reshape(1, LANES),
      _bias_table(rel_bias), sink.astype(F32))


def _outffn_body(*refs, tiles_per_seq, ff_chunk):
    main, prev, nxt = refs[0:6], refs[6:12], refs[12:18]
    lw_ref, lb_ref, wo_ref, gf_ref, wu_ref, cw_ref, cb_ref, wd_ref, o_ref, act_scr = refs[18:]
    i = pl.program_id(0)
    tile = main[0].shape[0]
    halo = prev[0].shape[0]
    d_ff = wd_ref.shape[0]
    c = RWKV_WIDTH
    total = tile + 2 * halo
    first = (i % tiles_per_seq) == 0
    last = (i % tiles_per_seq) == tiles_per_seq - 1

    def mixed(x, yf, yb, bonus, gate, attn):
        y = yf + yb
        mu = _head_sum(y, 2) * (1.0 / HEAD_DIM)
        yc = y - mu
        var = _head_sum(yc * yc, 1) * (1.0 / HEAD_DIM)
        yn = yc * lax.rsqrt(var + LNX_EPS) * lw_ref[...] + lb_ref[...]
        mix_r = (yn + bonus) * gate
        return x + _dot(mix_r.astype(BF16), wo_ref[:c, :]) + _dot(attn.astype(BF16), wo_ref[c:, :])

    x1 = mixed(*(r[...] for r in main))
    x1cat = jnp.concatenate([mixed(*(r[...] for r in prev)), x1, mixed(*(r[...] for r in nxt))], axis=0)
    hcat = x1cat * lax.rsqrt(jnp.mean(x1cat * x1cat, axis=-1, keepdims=True) + NORM_EPS) * gf_ref[...]
    row = lax.broadcasted_iota(jnp.int32, (total, 1), 0)
    outside = (first & (row < halo)) | (last & (row >= halo + tile))
    hcat = jnp.where(outside, 0.0, hcat).astype(BF16)

    def conv(cols):
        u = _dot(hcat, wu_ref[:, cols])
        up = pltpu.roll(u, 1, 0)[halo:halo + tile]
        un = pltpu.roll(u, total - 1, 0)[halo:halo + tile]
        return (up * cw_ref[0:1, cols] + u[halo:halo + tile] * cw_ref[1:2, cols] + un * cw_ref[2:3, cols]
                + cb_ref[:, cols])

    for k in range(d_ff // ff_chunk):
        g = conv(slice(k * ff_chunk, (k + 1) * ff_chunk))
        val = conv(slice(d_ff + k * ff_chunk, d_ff + (k + 1) * ff_chunk))
        act_scr[:, k * ff_chunk:(k + 1) * ff_chunk] = (g * _sigmoid(g) * val).astype(BF16)
    o_ref[...] = x1 + _dot(act_scr[...], wd_ref[...])


def _out_ffn(x2, yf, yb, bonus, gate, attn, lnx_w, lnx_b, w_out, g_ffn, w_up, conv_w, conv_b, w_down,
             seq, tile, ff_chunk):
    rows, d = x2.shape
    c = RWKV_WIDTH
    d_ff = w_down.shape[0]
    halo = SUBLANES_F32
    per = tile // halo
    row = lambda i: (i, 0)
    prv = lambda i: (jnp.maximum(i * per - 1, 0), 0)
    nxt = lambda i: (jnp.minimum((i + 1) * per, rows // halo - 1), 0)
    const = lambda i: (0, 0)
    resident = dict(pipeline_mode=pl.Buffered(1))
    streams = (x2, yf, yb, bonus, gate, attn)

    def stream_specs(nrows, index_map):
        return [pl.BlockSpec((nrows, z.shape[1]), index_map) for z in streams]

    body = functools.partial(_outffn_body, tiles_per_seq=seq // tile, ff_chunk=ff_chunk)
    return pl.pallas_call(
        body,
        grid=(rows // tile,),
        in_specs=stream_specs(tile, row) + stream_specs(halo, prv) + stream_specs(halo, nxt)
                 + [pl.BlockSpec((1, c), const), pl.BlockSpec((1, c), const),
                    pl.BlockSpec(w_out.shape, const, **resident), pl.BlockSpec((1, d), const),
                    pl.BlockSpec((d, 2 * d_ff), const, **resident),
                    pl.BlockSpec((3, 2 * d_ff), const), pl.BlockSpec((1, 2 * d_ff), const),
                    pl.BlockSpec((d_ff, d), const, **resident)],
        out_specs=pl.BlockSpec((tile, d), row),
        out_shape=jax.ShapeDtypeStruct((rows, d), F32),
        scratch_shapes=[pltpu.VMEM((tile, d_ff), BF16)],
        compiler_params=pltpu.CompilerParams(dimension_semantics=("parallel",), vmem_limit_bytes=VMEM_LIMIT),
        name="out_ffn",
    )(*streams, *streams, *streams, lnx_w.reshape(1, c), lnx_b.reshape(1, c), w_out.astype(BF16),
      g_ffn.reshape(1, d), w_up.astype(BF16), conv_w, conv_b.reshape(1, 2 * d_ff), w_down.astype(BF16))


def _layer(x, g_mix, w_in, mu_prev, mu_next, w0, w2, a0, a2, g2, k_k, k_a, r_k, lnx_w, lnx_b,
           q_gain, k_gain, rel_bias, sink, w_out, g_ffn, w_up, conv_w, conv_b, w_down):
    b, t, d = x.shape
    rows = b * t
    x2 = x.reshape(rows, d)
    q, kv, *ops = _inproj_prep(x2, g_mix, w_in, mu_prev, mu_next, w0, w2, a0, a2, g2, k_k, k_a, r_k.reshape(-1),
                               seq=t, tile=min(512, t))
    r, v, kk, lw0, lw1, kd0, kd1, b0, b1 = (z.reshape(b, t, RWKV_WIDTH) for z in ops[:9])
    gate, bonus = ops[9:]
    yf, yb, attn = _mixers(r, v, kk, lw0, lw1, kd0, kd1, b0, b1, q.reshape(b, t, ATTN_WIDTH),
                           kv.reshape(b, t, 2 * KV_COLS), q_gain, k_gain, rel_bias, sink)
    flat = lambda z: z.reshape(rows, z.shape[-1])
    out = _out_ffn(x2, flat(yf), flat(yb), bonus, gate, flat(attn), lnx_w, lnx_b, w_out, g_ffn,
                   w_up, conv_w, conv_b, w_down, seq=t, tile=min(512, t), ff_chunk=256)
    return out.reshape(b, t, d)


def kernel(x, g_mix, w_in, mu_prev, mu_next, w0, w2, a0, a2, g2, k_k, k_a, r_k, lnx_w, lnx_b, q_gain, k_gain,
           rel_bias, sink, w_out, g_ffn, w_up, conv_w, conv_b, w_down):
    depth = g_mix.shape[0]
    for l in range(depth):
        x = _layer(x, g_mix[l], w_in[l], mu_prev[l], mu_next[l], w0[l], w2[l], a0[l], a2[l], g2[l], k_k[l], k_a[l],
                   r_k[l], lnx_w[l], lnx_b[l], q_gain[l], k_gain[l], rel_bias, sink[l], w_out[l], g_ffn[l],
                   w_up[l], conv_w[l], conv_b[l], w_down[l])
    return x
```

```python
import functools
import math

import jax
import jax.numpy as jnp
from jax import lax
from jax.experimental import pallas as pl
from jax.experimental.pallas import tpu as pltpu

F32 = jnp.float32
BF16 = jnp.bfloat16

HEAD_DIM = 64
RWKV_WIDTH = 512
ATTN_WIDTH = 512
KV_HEADS = 2
Q_HEADS = 8
DECAY_LORA = 64
ICLR_LORA = 64
GATE_LORA = 128
RWKV_COLS = 3 * RWKV_WIDTH + DECAY_LORA + ICLR_LORA + GATE_LORA
KV_COLS = KV_HEADS * HEAD_DIM
WINDOW = 128
BLOCK = 128
REL_BUCKETS = 32
REL_MAX_DIST = 128
NORM_EPS = 1e-6
LNX_EPS = 64e-5
KK_EPS = 1e-12
LOG2E = math.log2(math.e)

LANES = 128
MXU_DIM = 256
SUBLANES_F32 = 8
SUBLANES_BF16 = 16
VMEM_LIMIT = 48 * 1024 * 1024

CHUNK = 64
CHUNKS_PER_STEP = 2
PAIR = 2 * HEAD_DIM
assert PAIR == LANES
HEAD_SHIFT = HEAD_DIM.bit_length() - 1
assert 1 << HEAD_SHIFT == HEAD_DIM


def _dot(a, b, precision=None):
    return jnp.dot(a, b, preferred_element_type=F32, precision=precision)


def _dot_nt(a, b, precision=None):
    return lax.dot_general(a, b, (((1,), (1,)), ((), ())), preferred_element_type=F32, precision=precision)


def _dot_tn(a, b, precision=None):
    return lax.dot_general(a, b, (((0,), (0,)), ((), ())), preferred_element_type=F32, precision=precision)


def _sigmoid(x):
    return 1.0 / (1.0 + jnp.exp(-x))


def _split_bf16(x, parts):
    out = []
    for _ in range(parts):
        h = x.astype(BF16)
        out.append(h)
        x = x - h.astype(F32)
    return out


def _head_sum(x, parts):
    width = x.shape[-1]
    blk = min(width, MXU_DIM)
    r = lax.broadcasted_iota(jnp.int32, (blk, blk), 0) >> HEAD_SHIFT
    c = lax.broadcasted_iota(jnp.int32, (blk, blk), 1) >> HEAD_SHIFT
    m = (r == c).astype(BF16)
    pieces = _split_bf16(x, parts)
    cols = []
    for c0 in range(0, width, blk):
        acc = None
        for h in pieces:
            t = _dot(h[:, c0:c0 + blk], m)
            acc = t if acc is None else acc + t
        cols.append(acc)
    return cols[0] if len(cols) == 1 else jnp.concatenate(cols, axis=1)


def _inprep_body(x_ref, xp_ref, xn_ref, g_ref, wr_ref, wq_ref, wkv_ref,
                 mup_ref, mun_ref, w0_ref, w2_ref, a0_ref, a2_ref, g2_ref, kk_ref, ka_ref, rk_ref,
                 q_o, kv_o, r_o, v_o, kk_o, lw0_o, lw1_o, kd0_o, kd1_o, b0_o, b1_o, gate_o, bonus_o,
                 *, tiles_per_seq):
    i = pl.program_id(0)
    tile = x_ref.shape[0]
    halo = xp_ref.shape[0]
    first = (i % tiles_per_seq) == 0
    last = (i % tiles_per_seq) == tiles_per_seq - 1

    def norm(x):
        return (x * lax.rsqrt(jnp.mean(x * x, axis=-1, keepdims=True) + NORM_EPS) * g_ref[...]).astype(BF16)

    h = norm(x_ref[...])
    q_o[...] = _dot(h, wq_ref[...])
    kv_o[...] = _dot(h, wkv_ref[...])
    hp = norm(xp_ref[...])
    hn = norm(xn_ref[...])
    hp = jnp.where(first, jnp.zeros_like(hp), hp)
    hn = jnp.where(last, jnp.zeros_like(hn), hn)
    hcat = jnp.concatenate([hp, h, hn], axis=0)
    total = tile + 2 * halo

    def shifted(lo, hi):
        pc = _dot(hcat, wr_ref[:, lo:hi])
        p = pc[halo:halo + tile]
        prev = pltpu.roll(pc, 1, 0)[halo:halo + tile]
        nxt = pltpu.roll(pc, total - 1, 0)[halo:halo + tile]
        return p + mup_ref[:, lo:hi] * (prev - p) + mun_ref[:, lo:hi] * (nxt - p)

    c = RWKV_WIDTH
    r, k, v = shifted(0, c), shifted(c, 2 * c), shifted(2 * c, 3 * c)
    tail = shifted(3 * c, RWKV_COLS)
    lora = tail[:, :LANES]
    xg = tail[:, LANES:]
    lora_t = jnp.tanh(lora).astype(BF16)
    lora_b = lora.astype(BF16)

    kx = k * kk_ref[...]
    kk = kx * lax.rsqrt(_head_sum(kx * kx, 1) + KK_EPS)
    gate = _dot(_sigmoid(xg).astype(BF16), g2_ref[...])

    kds = []
    for d, (lw_o, kd_o, b_o) in enumerate(((lw0_o, kd0_o, b0_o), (lw1_o, kd1_o, b1_o))):
        w_raw = w0_ref[d:d + 1, :] + _dot(lora_t, w2_ref[d])
        lw_o[...] = (-math.exp(-0.5)) * _sigmoid(w_raw)
        iclr = _sigmoid(a0_ref[d:d + 1, :] + _dot(lora_b, a2_ref[d]))
        kd = k * (1.0 + (iclr - 1.0) * ka_ref[...])
        kd_o[...] = kd
        b_o[...] = iclr * kk
        kds.append(kd)

    r_o[...] = r
    v_o[...] = v
    kk_o[...] = kk
    gate_o[...] = gate
    bonus_o[...] = _head_sum(r * (kds[0] + kds[1]) * rk_ref[...], 1) * v


def _inproj_prep(x2, g_mix, w_in, mu_prev, mu_next, w0, w2, a0, a2, g2, k_k, k_a, r_k, seq, tile):
    rows, d = x2.shape
    c = RWKV_WIDTH
    cols = RWKV_COLS
    halo = SUBLANES_BF16
    per = tile // halo
    wb = w_in.astype(BF16)
    wr, wq, wkv = wb[:, :cols], wb[:, cols:cols + ATTN_WIDTH], wb[:, cols + ATTN_WIDTH:]
    zeros = jnp.zeros((2, DECAY_LORA, c), F32)
    w2p = jnp.concatenate([w2, zeros], axis=1).astype(BF16)
    a2p = jnp.concatenate([zeros, a2], axis=1).astype(BF16)
    row = lambda i: (i, 0)
    prv = lambda i: (jnp.maximum(i * per - 1, 0), 0)
    nxt = lambda i: (jnp.minimum((i + 1) * per, rows // halo - 1), 0)
    c2 = lambda i: (0, 0)
    c3 = lambda i: (0, 0, 0)
    resident = dict(pipeline_mode=pl.Buffered(1))
    out = jax.ShapeDtypeStruct((rows, c), F32)
    body = functools.partial(_inprep_body, tiles_per_seq=seq // tile)
    return pl.pallas_call(
        body,
        grid=(rows // tile,),
        in_specs=[pl.BlockSpec((tile, d), row), pl.BlockSpec((halo, d), prv), pl.BlockSpec((halo, d), nxt),
                  pl.BlockSpec((1, d), c2),
                  pl.BlockSpec(wr.shape, c2, **resident), pl.BlockSpec(wq.shape, c2, **resident),
                  pl.BlockSpec(wkv.shape, c2, **resident),
                  pl.BlockSpec((1, cols), c2), pl.BlockSpec((1, cols), c2),
                  pl.BlockSpec((2, c), c2), pl.BlockSpec((2, LANES, c), c3),
                  pl.BlockSpec((2, c), c2), pl.BlockSpec((2, LANES, c), c3),
                  pl.BlockSpec((GATE_LORA, c), c2),
                  pl.BlockSpec((1, c), c2), pl.BlockSpec((1, c), c2), pl.BlockSpec((1, c), c2)],
        out_specs=[pl.BlockSpec((tile, ATTN_WIDTH), row), pl.BlockSpec((tile, 2 * KV_COLS), row)]
                  + [pl.BlockSpec((tile, c), row)] * 11,
        out_shape=[jax.ShapeDtypeStruct((rows, ATTN_WIDTH), F32), jax.ShapeDtypeStruct((rows, 2 * KV_COLS), F32)]
                  + [out] * 11,
        compiler_params=pltpu.CompilerParams(dimension_semantics=("parallel",), vmem_limit_bytes=VMEM_LIMIT),
        name="inproj_prep",
    )(x2, x2, x2, g_mix.reshape(1, d), wr, wq, wkv, mu_prev.reshape(1, cols), mu_next.reshape(1, cols),
      w0, w2p, a0, a2p, g2.astype(BF16), k_k.reshape(1, c), k_a.reshape(1, c), r_k.reshape(1, c))


def _scan_step(j, *refs, filler=iter(())):
    h_scr = refs[-1]

    @pl.when(j == 0)
    def _():
        h_scr[...] = jnp.zeros_like(h_scr)

    n_batch, rows = refs[0].shape[0], refs[0].shape[1]
    members = [(bi, d, sub) for bi in range(n_batch) for d in range(2) for sub in range(rows // CHUNK)]
    for stage, _ in enumerate(_scan_stages(members, *refs)):
        if stage >= 2:
            next(filler, None)
    for _ in filler:
        pass


def _scan_stages(members, rf, vf, kkf, lwf, kdf, bf, rb, vb, kkb, lwb, kdb, bb, yf_o, yb_o, h_scr):
    cs = CHUNK
    subs = rf.shape[1] // cs
    n_pairs = rf.shape[2] // PAIR

    cs_shift = cs.bit_length() - 1
    assert 1 << cs_shift == cs
    t_row = lax.broadcasted_iota(jnp.int32, (cs, 2 * cs), 0)
    t_col = lax.broadcasted_iota(jnp.int32, (cs, 2 * cs), 1) & (cs - 1)
    eye_cat = (t_row == t_col).astype(F32)
    same_block = lambda log2_size: (t_row >> log2_size) == (t_col >> log2_size)
    ri = lax.broadcasted_iota(jnp.int32, (PAIR, PAIR), 0)
    ci = lax.broadcasted_iota(jnp.int32, (PAIR, PAIR), 1)
    eye = (ri == ci).astype(F32)
    same_head = (ri >> HEAD_SHIFT) == (ci >> HEAD_SHIFT)
    tr = lax.broadcasted_iota(jnp.int32, (cs, cs), 0)
    tc = lax.broadcasted_iota(jnp.int32, (cs, cs), 1)
    lane_even =lax.broadcasted_iota(jnp.int32, (cs, PAIR), 1) < HEAD_DIM
    col_first = lax.broadcasted_iota(jnp.int32, (cs, 2 * cs), 1) < cs

    def stack(x):
        zero = jnp.zeros_like(x)
        return jnp.concatenate([jnp.where(lane_even, x, zero), jnp.where(lane_even, zero, x)], axis=0)

    def block_diag(m):
        zero = jnp.zeros_like(m)
        return jnp.concatenate([jnp.where(col_first, m, zero), jnp.where(col_first, zero, m)], axis=0)

    dirs = ((0, rf, vf, kkf, lwf, kdf, bf, yf_o), (1, rb, vb, kkb, lwb, kdb, bb, yb_o))
    chains = []
    for bi, sub, (d, r_ref, v_ref, kk_ref, lw_ref, kd_ref, b_ref, y_o) in (
            (bi, sub, dirs[d]) for bi, d, sub in members):
        fwd = d == 0
        rows = slice(sub * cs, (sub + 1) * cs)
        strict = (t_col < t_row) if fwd else (t_col > t_row)
        incl = strict | (t_col == t_row)
        cum = ((tc <= tr) if fwd else (tc >= tr)).astype(BF16)

        lw = lw_ref[bi, rows]
        c_in = sum(_dot(cum, piece) for piece in _split_bf16(lw, 2))
        c_ex = c_in - lw
        c_tot = jnp.sum(lw, axis=0, keepdims=True)
        gam = jnp.exp(c_tot)
        kd = kd_ref[bi, rows]
        bv = b_ref[bi, rows]
        e_neg = jnp.exp(-c_in)
        e_end = jnp.exp(c_tot - c_in)
        a_t = -kk_ref[bi, rows] * jnp.exp(c_ex)
        r_t = r_ref[bi, rows] * jnp.exp(c_in)
        b_t = bv * e_neg
        k_t = kd * e_neg
        b_h = bv * e_end
        k_h = kd * e_end
        vv = v_ref[bi, rows]
        for pr in range(n_pairs):
            sl = slice(pr * PAIR, (pr + 1) * PAIR)
            a_b, v_b = a_t[:, sl].astype(BF16), vv[:, sl].astype(BF16)
            chains.append(dict(
                bi=bi, d=d, pr=pr, sl=sl, rows=rows, order=sub if fwd else subs - 1 - sub,
                y_o=y_o, strict=strict, incl=incl, r_t=r_t[:, sl], gam=gam[:, sl],
                a_b=a_b, r_b=r_t[:, sl].astype(BF16), a_sb=stack(a_b), v_b=v_b, v_sb=stack(v_b),
                bt_sb=stack(b_t[:, sl].astype(BF16)), kt_sb=stack(k_t[:, sl].astype(BF16)),
                bk_h=jnp.concatenate([b_h[:, sl].astype(BF16), k_h[:, sl].astype(BF16)], axis=0)))
    yield

    h2 = 2 * cs
    for c in chains:
        lhs = jnp.concatenate([c["a_b"], c["r_b"]], axis=0)
        rhs = jnp.concatenate([c["bt_sb"], c["kt_sb"]], axis=0)
        gram = _dot_nt(lhs, rhs)
        a_ab = jnp.where(c["strict"], gram[:cs, :h2], 0.0)
        c["a_xk"] = jnp.concatenate([jnp.where(c["strict"], gram[:cs, h2:], 0.0),
                                     jnp.where(c["incl"], gram[cs:, h2:], 0.0)], axis=0).astype(BF16)
        c["a_rb"] = jnp.where(c["incl"], gram[cs:, :h2], 0.0).astype(BF16)
        c["n_cat"] = a_ab.astype(BF16)
        c["s_cat"] = (eye_cat + jnp.where(same_block(1), a_ab, 0.0)).astype(BF16)
    yield

    for level in range(1, cs_shift):
        join = same_block(level + 1) & ~same_block(level)
        zero = jnp.zeros((cs, h2), BF16)
        for c in chains:
            tn = _dot(c["s_cat"], block_diag(jnp.where(join, c["n_cat"], zero)))
            c["z_cat"] = (eye_cat + tn).astype(BF16)
        yield
        for c in chains:
            c["s_cat"] = _dot(c["z_cat"], block_diag(c["s_cat"])).astype(BF16)
        yield

    zero_b = jnp.zeros((cs, PAIR), BF16)
    for c in chains:
        av = _dot(c["a_xk"], c["v_sb"])
        c["w1_sb"] = stack(av[:cs].astype(BF16))
        c["av"] = av[cs:]
        c["gam_col"] = jnp.sum(eye * c["gam"], axis=1, keepdims=True)
    yield
    for c in chains:
        c["pq"] = _dot(c["s_cat"], jnp.concatenate([c["a_sb"], c["w1_sb"]], axis=1)).astype(BF16)
    yield
    for c in chains:
        pq = c["pq"]
        ry = _dot(c["a_rb"], jnp.concatenate([stack(pq[:, :PAIR]), stack(pq[:, PAIR:])], axis=1))
        c["r_hat"] = (c["r_t"] + ry[:, :PAIR]).astype(BF16)
        c["y_hat"] = ry[:, PAIR:] + c["av"]
        rhs = jnp.concatenate([pq, jnp.concatenate([zero_b, c["v_b"]], axis=1)], axis=0)
        gd = _dot_tn(c["bk_h"], rhs)
        c["btp"] = jnp.where(same_head, gd[:, :PAIR], 0.0).astype(BF16)
        c["dd"] = jnp.where(same_head, gd[:, PAIR:], 0.0)
    for order in range(subs):
        yield
        for c in (c for c in chains if c["order"] == order):
            h0 = h_scr[c["bi"], c["d"], c["pr"]]
            yh = _dot(jnp.concatenate([c["r_hat"], c["btp"]], axis=0), h0.astype(BF16))
            c["y_o"][c["bi"], c["rows"], c["sl"]] = yh[:cs] + c["y_hat"]
            h_scr[c["bi"], c["d"], c["pr"]] = c["gam_col"] * h0 + yh[cs:] + c["dd"]


def _attn_stages(n, nb, q_ref, rows, kv_window, qg_ref, kg_ref, bias_ref, sink_ref, o_ref):
    blk = rows.stop - rows.start
    head_mean = lambda z: _head_sum(z, 1) * (1.0 / HEAD_DIM)

    k_win = jnp.concatenate([kv[:, :KV_COLS] for kv in kv_window], axis=0)
    v_win = jnp.concatenate([kv[:, KV_COLS:] for kv in kv_window], axis=0)
    kn = k_win * lax.rsqrt(head_mean(k_win * k_win) + NORM_EPS) * kg_ref[...]
    v_b = v_win.astype(BF16)

    lane = lax.broadcasted_iota(jnp.int32, kn.shape, 1)
    k_at = []
    for g in range(KV_HEADS):
        own = jnp.where((lane >> HEAD_SHIFT) == g, kn, 0.0)
        other = pltpu.roll(own, HEAD_DIM, 1)
        k_at.append([own if p == g else other for p in range(2)])

    row = lax.broadcasted_iota(jnp.int32, (blk, 3 * blk), 0)
    col = lax.broadcasted_iota(jnp.int32, (blk, 3 * blk), 1)
    rel = col - blk - row
    valid = (jnp.abs(rel) <= WINDOW)
    valid &= (col >= blk) | (n > 0)
    valid &= (col < 2 * blk) | (n < nb - 1)
    out_lane_even = lax.broadcasted_iota(jnp.int32, (blk, LANES), 1) < HEAD_DIM

    group = Q_HEADS // KV_HEADS
    n_slabs = ATTN_WIDTH // LANES
    slabs_per_group = n_slabs // KV_HEADS
    qn = []
    for s in range(n_slabs):
        q = q_ref[0, rows, s * LANES:(s + 1) * LANES]
        qn.append((q * lax.rsqrt(head_mean(q * q) + NORM_EPS) * qg_ref[...]
                   * (HEAD_DIM ** -0.5 * LOG2E)).astype(BF16))
    scores = []
    for g in range(KV_HEADS):
        kcat = jnp.concatenate([k_at[g][0], k_at[g][1]], axis=0).astype(BF16)
        q_g = jnp.concatenate(qn[g * slabs_per_group:(g + 1) * slabs_per_group], axis=0)
        scores.append(_dot_nt(q_g, kcat))
    yield
    probs, inv_denom = [], []
    for s in range(n_slabs):
        heads = (2 * s, 2 * s + 1)
        g, sg = s // slabs_per_group, s % slabs_per_group
        sc = [jnp.where(valid, scores[g][sg * blk:(sg + 1) * blk, p * 3 * blk:(p + 1) * 3 * blk] + bias_ref[h],
                        -jnp.inf) for p, h in enumerate(heads)]
        sinks = [sink_ref[h] * LOG2E for h in heads]
        m = [jnp.maximum(jnp.max(z, axis=-1, keepdims=True), sk) for z, sk in zip(sc, sinks)]
        e = [jnp.exp2(z - mm) for z, mm in zip(sc, m)]
        inv_denom += [1.0 / (jnp.sum(ee, axis=-1, keepdims=True) + jnp.exp2(sk - mm))
                      for ee, sk, mm in zip(e, sinks, m)]
        probs += [ee.astype(BF16) for ee in e]
        yield
    o_all = _dot(jnp.concatenate(probs, axis=0), v_b)
    for s in range(n_slabs):
        g = (2 * s) // group
        halves = []
        for p in range(2):
            h = 2 * s + p
            o = o_all[h * blk:(h + 1) * blk] * inv_denom[h]
            halves.append(o if p == g else pltpu.roll(o, HEAD_DIM, 1))
        o_ref[0, rows, s * LANES:(s + 1) * LANES] = jnp.where(out_lane_even, halves[0], halves[1])


def _t5_bucket(rel):
    nb = REL_BUCKETS // 2
    max_exact = nb // 2
    ret = jnp.where(rel > 0, nb, 0)
    n = jnp.abs(rel)
    large = max_exact + (jnp.log(jnp.maximum(n, 1).astype(F32) / max_exact)
                         / math.log(REL_MAX_DIST / max_exact) * (nb - max_exact)).astype(jnp.int32)
    large = jnp.minimum(large, nb - 1)
    return ret + jnp.where(n < max_exact, n, large)


BIAS_SPAN = 4 * BLOCK


def _bias_body(tab_ref, o_ref):
    blk = o_ref.shape[1]
    x = jnp.broadcast_to(tab_ref[0], (blk, BIAS_SPAN))
    shifted = pltpu.roll(x, BIAS_SPAN - (blk - 1), 1, stride=1, stride_axis=0)
    o_ref[0] = shifted[:, :3 * blk]


def _bias_table(rel_bias):
    rel = jnp.arange(BIAS_SPAN) - (2 * BLOCK - 1)
    tab = jnp.transpose(rel_bias[_t5_bucket(rel)].astype(F32)) * LOG2E
    shape = (BLOCK, 3 * BLOCK)
    return pl.pallas_call(
        _bias_body,
        grid=(Q_HEADS,),
        in_specs=[pl.BlockSpec((1, 1, BIAS_SPAN), lambda h: (h, 0, 0))],
        out_specs=pl.BlockSpec((1,) + shape, lambda h: (h, 0, 0)),
        out_shape=jax.ShapeDtypeStruct((Q_HEADS,) + shape, F32),
        name="bias_table",
    )(tab.reshape(Q_HEADS, 1, BIAS_SPAN))


def _mixers_body(*refs, n_blocks, per_step):
    scan_refs = refs[:12]
    q_ref, kvp_ref, kvc_ref, kvn_ref, qg_ref, kg_ref, bias_ref, sink_ref = refs[12:20]
    yf_o, yb_o, attn_o, h_scr = refs[20:]
    j = pl.program_id(0)
    kv_blocks = ([kvp_ref[0]] + [kvc_ref[0, s * BLOCK:(s + 1) * BLOCK] for s in range(per_step)] + [kvn_ref[0]])

    def attention():
        for s in range(per_step):
            n = (j * per_step + s) % n_blocks
            yield from _attn_stages(n, n_blocks, q_ref, slice(s * BLOCK, (s + 1) * BLOCK), kv_blocks[s:s + 3],
                                    qg_ref, kg_ref, bias_ref, sink_ref, attn_o)

    _scan_step(j, *scan_refs, yf_o, yb_o, h_scr, filler=attention())


def _mixers(r, v, kk, lw0, lw1, kd0, kd1, b0, b1, q, kv, q_gain, k_gain, rel_bias, sink):
    b, t, c = r.shape
    steps = t // (CHUNKS_PER_STEP * CHUNK)
    nb = t // BLOCK
    per_step = (b * nb) // steps
    assert per_step * steps == b * nb and nb % per_step == 0
    nq = nb // per_step
    fw = lambda j: (0, j, 0)
    bw = lambda j: (0, steps - 1 - j, 0)
    blk = (b, CHUNKS_PER_STEP * CHUNK, c)
    cur = lambda j: (j // nq, j % nq, 0)
    prv = lambda j: (j // nq, jnp.maximum((j % nq) * per_step - 1, 0), 0)
    nxt = lambda j: (j // nq, jnp.minimum((j % nq + 1) * per_step, nb - 1), 0)
    c2 = lambda j: (0, 0)
    c3 = lambda j: (0, 0, 0)
    kvblk = (1, BLOCK, 2 * KV_COLS)
    return pl.pallas_call(
        functools.partial(_mixers_body, n_blocks=nb, per_step=per_step),
        grid=(steps,),
        in_specs=[pl.BlockSpec(blk, fw)] * 6 + [pl.BlockSpec(blk, bw)] * 6
                 + [pl.BlockSpec((1, per_step * BLOCK, ATTN_WIDTH), cur), pl.BlockSpec(kvblk, prv),
                    pl.BlockSpec((1, per_step * BLOCK, 2 * KV_COLS), cur), pl.BlockSpec(kvblk, nxt),
                    pl.BlockSpec((1, LANES), c2), pl.BlockSpec((1, LANES), c2),
                    pl.BlockSpec((Q_HEADS, BLOCK, 3 * BLOCK), c3), pl.BlockSpec(memory_space=pltpu.SMEM)],
        out_specs=[pl.BlockSpec(blk, fw), pl.BlockSpec(blk, bw),
                   pl.BlockSpec((1, per_step * BLOCK, ATTN_WIDTH), cur)],
        out_shape=[jax.ShapeDtypeStruct((b, t, c), F32)] * 2 + [jax.ShapeDtypeStruct((b, t, ATTN_WIDTH), F32)],
        scratch_shapes=[pltpu.VMEM((b, 2, c // PAIR, PAIR, PAIR), F32)],
        compiler_params=pltpu.CompilerParams(dimension_semantics=("arbitrary",), vmem_limit_bytes=VMEM_LIMIT),
        name="mixers",
    )(r, v, kk, lw0, kd0, b0, r, v, kk, lw1, kd1, b1,
      q, kv, kv, kv, jnp.tile(q_gain, 2).reshape(1, LANES), jnp.tile(k_gain, 2).reshape(1, LANES),
      _bias_table(rel_bias), sink.astype(F32))


def _outffn_body(*refs, tiles_per_seq, ff_chunk):
    main, prev, nxt = refs[0:6], refs[6:12], refs[12:18]
    lw_ref, lb_ref, wo_ref, gf_ref, wu_ref, cw_ref, cb_ref, wd_ref, o_ref, act_scr = refs[18:]
    i = pl.program_id(0)
    tile = main[0].shape[0]
    halo = prev[0].shape[0]
    d_ff = wd_ref.shape[0]
    c = RWKV_WIDTH
    total = tile + 2 * halo
    first = (i % tiles_per_seq) == 0
    last = (i % tiles_per_seq) == tiles_per_seq - 1

    def mixed(x, yf, yb, bonus, gate, attn):
        y = yf + yb
        mu = _head_sum(y, 2) * (1.0 / HEAD_DIM)
        yc = y - mu
        var = _head_sum(yc * yc, 1) * (1.0 / HEAD_DIM)
        yn = yc * lax.rsqrt(var + LNX_EPS) * lw_ref[...] + lb_ref[...]
        mix_r = (yn + bonus) * gate
        return x + _dot(mix_r.astype(BF16), wo_ref[:c, :]) + _dot(attn.astype(BF16), wo_ref[c:, :])

    x1 = mixed(*(r[...] for r in main))
    x1cat = jnp.concatenate([mixed(*(r[...] for r in prev)), x1, mixed(*(r[...] for r in nxt))], axis=0)
    hcat = x1cat * lax.rsqrt(jnp.mean(x1cat * x1cat, axis=-1, keepdims=True) + NORM_EPS) * gf_ref[...]
    row = lax.broadcasted_iota(jnp.int32, (total, 1), 0)
    outside = (first & (row < halo)) | (last & (row >= halo + tile))
    hcat = jnp.where(outside, 0.0, hcat).astype(BF16)

    def conv(cols):
        u = _dot(hcat, wu_ref[:, cols])
        up = pltpu.roll(u, 1, 0)[halo:halo + tile]
        un = pltpu.roll(u, total - 1, 0)[halo:halo + tile]
        return (up * cw_ref[0:1, cols] + u[halo:halo + tile] * cw_ref[1:2, cols] + un * cw_ref[2:3, cols]
                + cb_ref[:, cols])

    for lo in range(0, d_ff, ff_chunk):
        hi = min(lo + ff_chunk, d_ff)
        g = conv(slice(lo, hi))
        val = conv(slice(d_ff + lo, d_ff + hi))
        act_scr[:, lo:hi] = (g * _sigmoid(g) * val).astype(BF16)
    o_ref[...] = x1 + _dot(act_scr[...], wd_ref[...])


def _out_ffn(x2, yf, yb, bonus, gate, attn, lnx_w, lnx_b, w_out, g_ffn, w_up, conv_w, conv_b, w_down,
             seq, tile, ff_chunk):
    rows, d = x2.shape
    c = RWKV_WIDTH
    d_ff = w_down.shape[0]
    halo = SUBLANES_F32
    per = tile // halo
    row = lambda i: (i, 0)
    prv = lambda i: (jnp.maximum(i * per - 1, 0), 0)
    nxt = lambda i: (jnp.minimum((i + 1) * per, rows // halo - 1), 0)
    const = lambda i: (0, 0)
    resident = dict(pipeline_mode=pl.Buffered(1))
    streams = (x2, yf, yb, bonus, gate, attn)

    def stream_specs(nrows, index_map):
        return [pl.BlockSpec((nrows, z.shape[1]), index_map) for z in streams]

    body = functools.partial(_outffn_body, tiles_per_seq=seq // tile, ff_chunk=ff_chunk)
    return pl.pallas_call(
        body,
        grid=(rows // tile,),
        in_specs=stream_specs(tile, row) + stream_specs(halo, prv) + stream_specs(halo, nxt)
                 + [pl.BlockSpec((1, c), const), pl.BlockSpec((1, c), const),
                    pl.BlockSpec(w_out.shape, const, **resident), pl.BlockSpec((1, d), const),
                    pl.BlockSpec((d, 2 * d_ff), const, **resident),
                    pl.BlockSpec((3, 2 * d_ff), const), pl.BlockSpec((1, 2 * d_ff), const),
                    pl.BlockSpec((d_ff, d), const, **resident)],
        out_specs=pl.BlockSpec((tile, d), row),
        out_shape=jax.ShapeDtypeStruct((rows, d), F32),
        scratch_shapes=[pltpu.VMEM((tile, d_ff), BF16)],
        compiler_params=pltpu.CompilerParams(dimension_semantics=("parallel",), vmem_limit_bytes=VMEM_LIMIT),
        name="out_ffn",
    )(*streams, *streams, *streams, lnx_w.reshape(1, c), lnx_b.reshape(1, c), w_out.astype(BF16),
      g_ffn.reshape(1, d), w_up.astype(BF16), conv_w, conv_b.reshape(1, 2 * d_ff), w_down.astype(BF16))


def _layer(x, g_mix, w_in, mu_prev, mu_next, w0, w2, a0, a2, g2, k_k, k_a, r_k, lnx_w, lnx_b,
           q_gain, k_gain, rel_bias, sink, w_out, g_ffn, w_up, conv_w, conv_b, w_down):
    b, t, d = x.shape
    rows = b * t
    x2 = x.reshape(rows, d)
    q, kv, *ops = _inproj_prep(x2, g_mix, w_in, mu_prev, mu_next, w0, w2, a0, a2, g2, k_k, k_a, r_k.reshape(-1),
                               seq=t, tile=min(512, t))
    r, v, kk, lw0, lw1, kd0, kd1, b0, b1 = (z.reshape(b, t, RWKV_WIDTH) for z in ops[:9])
    gate, bonus = ops[9:]
    yf, yb, attn = _mixers(r, v, kk, lw0, lw1, kd0, kd1, b0, b1, q.reshape(b, t, ATTN_WIDTH),
                           kv.reshape(b, t, 2 * KV_COLS), q_gain, k_gain, rel_bias, sink)
    flat = lambda z: z.reshape(rows, z.shape[-1])
    out = _out_ffn(x2, flat(yf), flat(yb), bonus, gate, flat(attn), lnx_w, lnx_b, w_out, g_ffn,
                   w_up, conv_w, conv_b, w_down, seq=t, tile=min(512, t), ff_chunk=768)
    return out.reshape(b, t, d)


def kernel(x, g_mix, w_in, mu_prev, mu_next, w0, w2, a0, a2, g2, k_k, k_a, r_k, lnx_w, lnx_b, q_gain, k_gain,
           rel_bias, sink, w_out, g_ffn, w_up, conv_w, conv_b, w_down):
    depth = g_mix.shape[0]
    for l in range(depth):
        x = _layer(x, g_mix[l], w_in[l], mu_prev[l], mu_next[l], w0[l], w2[l], a0[l], a2[l], g2[l], k_k[l], k_a[l],
                   r_k[l], lnx_w[l], lnx_b[l], q_gain[l], k_gain[l], rel_bias, sink[l], w_out[l], g_ffn[l],
                   w_up[l], conv_w[l], conv_b[l], w_down[l])
    return x
```

```python
import functools
import math

import jax
import jax.numpy as jnp
from jax import lax
from jax.experimental import pallas as pl
from jax.experimental.pallas import tpu as pltpu

F32 = jnp.float32
BF16 = jnp.bfloat16

HEAD_DIM = 64
RWKV_WIDTH = 512
ATTN_WIDTH = 512
KV_HEADS = 2
Q_HEADS = 8
DECAY_LORA = 64
ICLR_LORA = 64
GATE_LORA = 128
RWKV_COLS = 3 * RWKV_WIDTH + DECAY_LORA + ICLR_LORA + GATE_LORA
KV_COLS = KV_HEADS * HEAD_DIM
WINDOW = 128
BLOCK = 128
REL_BUCKETS = 32
REL_MAX_DIST = 128
NORM_EPS = 1e-6
LNX_EPS = 64e-5
KK_EPS = 1e-12
LOG2E = math.log2(math.e)

LANES = 128
MXU_DIM = 256
SUBLANES_F32 = 8
SUBLANES_BF16 = 16
VMEM_LIMIT = 48 * 1024 * 1024

ROW_TILE = 512
FF_CHUNK = 768
CHUNK = 64
CHUNKS_PER_STEP = 2
PAIR = 2 * HEAD_DIM
assert PAIR == LANES
HEAD_SHIFT = HEAD_DIM.bit_length() - 1
assert 1 << HEAD_SHIFT == HEAD_DIM


def _dot(a, b):
    return jnp.dot(a, b, preferred_element_type=F32)


def _dot_nt(a, b):
    return lax.dot_general(a, b, (((1,), (1,)), ((), ())), preferred_element_type=F32)


def _dot_tn(a, b):
    return lax.dot_general(a, b, (((0,), (0,)), ((), ())), preferred_element_type=F32)


def _sigmoid(x):
    return 1.0 / (1.0 + jnp.exp(-x))


def _split_bf16(x, parts):
    out = []
    for _ in range(parts):
        h = x.astype(BF16)
        out.append(h)
        x = x - h.astype(F32)
    return out


def _head_sum(x, parts):
    width = x.shape[-1]
    blk = min(width, MXU_DIM)
    r = lax.broadcasted_iota(jnp.int32, (blk, blk), 0) >> HEAD_SHIFT
    c = lax.broadcasted_iota(jnp.int32, (blk, blk), 1) >> HEAD_SHIFT
    m = (r == c).astype(BF16)
    pieces = _split_bf16(x, parts)
    cols = []
    for c0 in range(0, width, blk):
        acc = None
        for h in pieces:
            t = _dot(h[:, c0:c0 + blk], m)
            acc = t if acc is None else acc + t
        cols.append(acc)
    return cols[0] if len(cols) == 1 else jnp.concatenate(cols, axis=1)


def _inprep_body(x_ref, xp_ref, xn_ref, g_ref, wr_ref, wq_ref, wkv_ref,
                 mup_ref, mun_ref, w0_ref, w2_ref, a0_ref, a2_ref, g2_ref, kk_ref, ka_ref, rk_ref,
                 q_o, kv_o, r_o, v_o, kk_o, lw0_o, lw1_o, kd0_o, kd1_o, b0_o, b1_o, gate_o, bonus_o,
                 *, tiles_per_seq):
    i = pl.program_id(0)
    tile = x_ref.shape[0]
    halo = xp_ref.shape[0]
    first = (i % tiles_per_seq) == 0
    last = (i % tiles_per_seq) == tiles_per_seq - 1

    def norm(x):
        return (x * lax.rsqrt(jnp.mean(x * x, axis=-1, keepdims=True) + NORM_EPS) * g_ref[...]).astype(BF16)

    h = norm(x_ref[...])
    q_o[...] = _dot(h, wq_ref[...])
    kv_o[...] = _dot(h, wkv_ref[...])
    hp = norm(xp_ref[...])
    hn = norm(xn_ref[...])
    hp = jnp.where(first, jnp.zeros_like(hp), hp)
    hn = jnp.where(last, jnp.zeros_like(hn), hn)
    hcat = jnp.concatenate([hp, h, hn], axis=0)
    total = tile + 2 * halo

    def shifted(lo, hi):
        pc = _dot(hcat, wr_ref[:, lo:hi])
        p = pc[halo:halo + tile]
        prev = pltpu.roll(pc, 1, 0)[halo:halo + tile]
        nxt = pltpu.roll(pc, total - 1, 0)[halo:halo + tile]
        return p + mup_ref[:, lo:hi] * (prev - p) + mun_ref[:, lo:hi] * (nxt - p)

    c = RWKV_WIDTH
    r, k, v = shifted(0, c), shifted(c, 2 * c), shifted(2 * c, 3 * c)
    tail = shifted(3 * c, RWKV_COLS)
    lora = tail[:, :LANES]
    xg = tail[:, LANES:]
    lora_t = jnp.tanh(lora).astype(BF16)
    lora_b = lora.astype(BF16)

    kx = k * kk_ref[...]
    kk = kx * lax.rsqrt(_head_sum(kx * kx, 1) + KK_EPS)
    gate = _dot(_sigmoid(xg).astype(BF16), g2_ref[...])

    kds = []
    for d, (lw_o, kd_o, b_o) in enumerate(((lw0_o, kd0_o, b0_o), (lw1_o, kd1_o, b1_o))):
        w_raw = w0_ref[d:d + 1, :] + _dot(lora_t, w2_ref[d])
        lw_o[...] = (-math.exp(-0.5)) * _sigmoid(w_raw)
        iclr = _sigmoid(a0_ref[d:d + 1, :] + _dot(lora_b, a2_ref[d]))
        kd = k * (1.0 + (iclr - 1.0) * ka_ref[...])
        kd_o[...] = kd
        b_o[...] = iclr * kk
        kds.append(kd)

    r_o[...] = r
    v_o[...] = v
    kk_o[...] = kk
    gate_o[...] = gate
    bonus_o[...] = _head_sum(r * (kds[0] + kds[1]) * rk_ref[...], 1) * v


def _inproj_prep(x2, g_mix, w_in, mu_prev, mu_next, w0, w2, a0, a2, g2, k_k, k_a, r_k, seq, tile):
    rows, d = x2.shape
    c = RWKV_WIDTH
    cols = RWKV_COLS
    halo = SUBLANES_BF16
    per = tile // halo
    wb = w_in.astype(BF16)
    wr, wq, wkv = wb[:, :cols], wb[:, cols:cols + ATTN_WIDTH], wb[:, cols + ATTN_WIDTH:]
    zeros = jnp.zeros((2, DECAY_LORA, c), F32)
    w2p = jnp.concatenate([w2, zeros], axis=1).astype(BF16)
    a2p = jnp.concatenate([zeros, a2], axis=1).astype(BF16)
    row = lambda i: (i, 0)
    prv = lambda i: (jnp.maximum(i * per - 1, 0), 0)
    nxt = lambda i: (jnp.minimum((i + 1) * per, rows // halo - 1), 0)
    c2 = lambda i: (0, 0)
    c3 = lambda i: (0, 0, 0)
    resident = dict(pipeline_mode=pl.Buffered(1))
    out = jax.ShapeDtypeStruct((rows, c), F32)
    body = functools.partial(_inprep_body, tiles_per_seq=seq // tile)
    return pl.pallas_call(
        body,
        grid=(rows // tile,),
        in_specs=[pl.BlockSpec((tile, d), row), pl.BlockSpec((halo, d), prv), pl.BlockSpec((halo, d), nxt),
                  pl.BlockSpec((1, d), c2),
                  pl.BlockSpec(wr.shape, c2, **resident), pl.BlockSpec(wq.shape, c2, **resident),
                  pl.BlockSpec(wkv.shape, c2, **resident),
                  pl.BlockSpec((1, cols), c2), pl.BlockSpec((1, cols), c2),
                  pl.BlockSpec((2, c), c2), pl.BlockSpec((2, LANES, c), c3),
                  pl.BlockSpec((2, c), c2), pl.BlockSpec((2, LANES, c), c3),
                  pl.BlockSpec((GATE_LORA, c), c2),
                  pl.BlockSpec((1, c), c2), pl.BlockSpec((1, c), c2), pl.BlockSpec((1, c), c2)],
        out_specs=[pl.BlockSpec((tile, ATTN_WIDTH), row), pl.BlockSpec((tile, 2 * KV_COLS), row)]
                  + [pl.BlockSpec((tile, c), row)] * 11,
        out_shape=[jax.ShapeDtypeStruct((rows, ATTN_WIDTH), F32), jax.ShapeDtypeStruct((rows, 2 * KV_COLS), F32)]
                  + [out] * 11,
        compiler_params=pltpu.CompilerParams(dimension_semantics=("parallel",), vmem_limit_bytes=VMEM_LIMIT),
        name="inproj_prep",
    )(x2, x2, x2, g_mix.reshape(1, d), wr, wq, wkv, mu_prev.reshape(1, cols), mu_next.reshape(1, cols),
      w0, w2p, a0, a2p, g2.astype(BF16), k_k.reshape(1, c), k_a.reshape(1, c), r_k.reshape(1, c))


def _scan_step(j, *refs, filler=iter(())):
    h_scr = refs[-1]

    @pl.when(j == 0)
    def _():
        h_scr[...] = jnp.zeros_like(h_scr)

    n_batch, rows = refs[0].shape[0], refs[0].shape[1]
    members = [(bi, d, sub) for bi in range(n_batch) for d in range(2) for sub in range(rows // CHUNK)]
    for stage, _ in enumerate(_scan_stages(members, *refs)):
        if stage >= 2:
            next(filler, None)
    for _ in filler:
        pass


def _scan_stages(members, rf, vf, kkf, lwf, kdf, bf, rb, vb, kkb, lwb, kdb, bb, yf_o, yb_o, h_scr):
    cs = CHUNK
    subs = rf.shape[1] // cs
    n_pairs = rf.shape[2] // PAIR

    cs_shift = cs.bit_length() - 1
    assert 1 << cs_shift == cs
    t_row = lax.broadcasted_iota(jnp.int32, (cs, 2 * cs), 0)
    t_col = lax.broadcasted_iota(jnp.int32, (cs, 2 * cs), 1) & (cs - 1)
    eye_cat = (t_row == t_col).astype(F32)
    same_block = lambda log2_size: (t_row >> log2_size) == (t_col >> log2_size)
    ri = lax.broadcasted_iota(jnp.int32, (PAIR, PAIR), 0)
    ci = lax.broadcasted_iota(jnp.int32, (PAIR, PAIR), 1)
    eye = (ri == ci).astype(F32)
    same_head = (ri >> HEAD_SHIFT) == (ci >> HEAD_SHIFT)
    tr = lax.broadcasted_iota(jnp.int32, (cs, cs), 0)
    tc = lax.broadcasted_iota(jnp.int32, (cs, cs), 1)
    lane_even = lax.broadcasted_iota(jnp.int32, (cs, PAIR), 1) < HEAD_DIM
    col_first = lax.broadcasted_iota(jnp.int32, (cs, 2 * cs), 1) < cs

    def stack(x):
        zero = jnp.zeros_like(x)
        return jnp.concatenate([jnp.where(lane_even, x, zero), jnp.where(lane_even, zero, x)], axis=0)

    def block_diag(m):
        zero = jnp.zeros_like(m)
        return jnp.concatenate([jnp.where(col_first, m, zero), jnp.where(col_first, zero, m)], axis=0)

    dirs = ((0, rf, vf, kkf, lwf, kdf, bf, yf_o), (1, rb, vb, kkb, lwb, kdb, bb, yb_o))
    chains = []
    for bi, sub, (d, r_ref, v_ref, kk_ref, lw_ref, kd_ref, b_ref, y_o) in (
            (bi, sub, dirs[d]) for bi, d, sub in members):
        fwd = d == 0
        rows = slice(sub * cs, (sub + 1) * cs)
        strict = (t_col < t_row) if fwd else (t_col > t_row)
        incl = strict | (t_col == t_row)
        cum = ((tc <= tr) if fwd else (tc >= tr)).astype(BF16)

        lw = lw_ref[bi, rows]
        c_in = sum(_dot(cum, piece) for piece in _split_bf16(lw, 2))
        c_ex = c_in - lw
        c_tot = jnp.sum(lw, axis=0, keepdims=True)
        gam = jnp.exp(c_tot)
        kd = kd_ref[bi, rows]
        bv = b_ref[bi, rows]
        e_neg = jnp.exp(-c_in)
        e_end = jnp.exp(c_tot - c_in)
        a_t = -kk_ref[bi, rows] * jnp.exp(c_ex)
        r_t = r_ref[bi, rows] * jnp.exp(c_in)
        b_t = bv * e_neg
        k_t = kd * e_neg
        b_h = bv * e_end
        k_h = kd * e_end
        vv = v_ref[bi, rows]
        for pr in range(n_pairs):
            sl = slice(pr * PAIR, (pr + 1) * PAIR)
            a_b, v_b = a_t[:, sl].astype(BF16), vv[:, sl].astype(BF16)
            chains.append(dict(
                bi=bi, d=d, pr=pr, sl=sl, rows=rows, order=sub if fwd else subs - 1 - sub,
                y_o=y_o, strict=strict, incl=incl, r_t=r_t[:, sl], gam=gam[:, sl],
                a_b=a_b, r_b=r_t[:, sl].astype(BF16), a_sb=stack(a_b), v_b=v_b, v_sb=stack(v_b),
                bt_sb=stack(b_t[:, sl].astype(BF16)), kt_sb=stack(k_t[:, sl].astype(BF16)),
                bk_h=jnp.concatenate([b_h[:, sl].astype(BF16), k_h[:, sl].astype(BF16)], axis=0)))
    yield

    h2 = 2 * cs
    for c in chains:
        lhs = jnp.concatenate([c["a_b"], c["r_b"]], axis=0)
        rhs = jnp.concatenate([c["bt_sb"], c["kt_sb"]], axis=0)
        gram = _dot_nt(lhs, rhs)
        a_ab = jnp.where(c["strict"], gram[:cs, :h2], 0.0)
        c["a_xk"] = jnp.concatenate([jnp.where(c["strict"], gram[:cs, h2:], 0.0),
                                     jnp.where(c["incl"], gram[cs:, h2:], 0.0)], axis=0).astype(BF16)
        c["a_rb"] = jnp.where(c["incl"], gram[cs:, :h2], 0.0).astype(BF16)
        c["n_cat"] = a_ab.astype(BF16)
        c["s_cat"] = (eye_cat + jnp.where(same_block(1), a_ab, 0.0)).astype(BF16)
    yield

    for level in range(1, cs_shift):
        join = same_block(level + 1) & ~same_block(level)
        zero = jnp.zeros((cs, h2), BF16)
        for c in chains:
            tn = _dot(c["s_cat"], block_diag(jnp.where(join, c["n_cat"], zero)))
            c["z_cat"] = (eye_cat + tn).astype(BF16)
        yield
        for c in chains:
            c["s_cat"] = _dot(c["z_cat"], block_diag(c["s_cat"])).astype(BF16)
        yield

    zero_b = jnp.zeros((cs, PAIR), BF16)
    for c in chains:
        av = _dot(c["a_xk"], c["v_sb"])
        c["w1_sb"] = stack(av[:cs].astype(BF16))
        c["av"] = av[cs:]
        c["gam_col"] = jnp.sum(eye * c["gam"], axis=1, keepdims=True)
    yield
    for c in chains:
        c["pq"] = _dot(c["s_cat"], jnp.concatenate([c["a_sb"], c["w1_sb"]], axis=1)).astype(BF16)
    yield
    for c in chains:
        pq = c["pq"]
        ry = _dot(c["a_rb"], jnp.concatenate([stack(pq[:, :PAIR]), stack(pq[:, PAIR:])], axis=1))
        c["r_hat"] = (c["r_t"] + ry[:, :PAIR]).astype(BF16)
        c["y_hat"] = ry[:, PAIR:] + c["av"]
        rhs = jnp.concatenate([pq, jnp.concatenate([zero_b, c["v_b"]], axis=1)], axis=0)
        gd = _dot_tn(c["bk_h"], rhs)
        c["btp"] = jnp.where(same_head, gd[:, :PAIR], 0.0).astype(BF16)
        c["dd"] = jnp.where(same_head, gd[:, PAIR:], 0.0)
    for order in range(subs):
        yield
        for c in (c for c in chains if c["order"] == order):
            h0 = h_scr[c["bi"], c["d"], c["pr"]]
            yh = _dot(jnp.concatenate([c["r_hat"], c["btp"]], axis=0), h0.astype(BF16))
            c["y_o"][c["bi"], c["rows"], c["sl"]] = yh[:cs] + c["y_hat"]
            h_scr[c["bi"], c["d"], c["pr"]] = c["gam_col"] * h0 + yh[cs:] + c["dd"]


def _attn_stages(n, nb, q_ref, rows, kv_window, qg_ref, kg_ref, bias_ref, sink_ref, o_ref):
    blk = rows.stop - rows.start
    head_mean = lambda z: _head_sum(z, 1) * (1.0 / HEAD_DIM)

    k_win = jnp.concatenate([kv[:, :KV_COLS] for kv in kv_window], axis=0)
    v_win = jnp.concatenate([kv[:, KV_COLS:] for kv in kv_window], axis=0)
    kn = k_win * lax.rsqrt(head_mean(k_win * k_win) + NORM_EPS) * kg_ref[...]
    v_b = v_win.astype(BF16)

    lane = lax.broadcasted_iota(jnp.int32, kn.shape, 1)
    k_at = []
    for g in range(KV_HEADS):
        own = jnp.where((lane >> HEAD_SHIFT) == g, kn, 0.0)
        other = pltpu.roll(own, HEAD_DIM, 1)
        k_at.append([own if p == g else other for p in range(2)])

    row = lax.broadcasted_iota(jnp.int32, (blk, 3 * blk), 0)
    col = lax.broadcasted_iota(jnp.int32, (blk, 3 * blk), 1)
    rel = col - blk - row
    valid = (jnp.abs(rel) <= WINDOW)
    valid &= (col >= blk) | (n > 0)
    valid &= (col < 2 * blk) | (n < nb - 1)
    out_lane_even = lax.broadcasted_iota(jnp.int32, (blk, LANES), 1) < HEAD_DIM

    group = Q_HEADS // KV_HEADS
    n_slabs = ATTN_WIDTH // LANES
    slabs_per_group = n_slabs // KV_HEADS
    qn = []
    for s in range(n_slabs):
        q = q_ref[0, rows, s * LANES:(s + 1) * LANES]
        qn.append((q * lax.rsqrt(head_mean(q * q) + NORM_EPS) * qg_ref[...]
                   * (HEAD_DIM ** -0.5 * LOG2E)).astype(BF16))
    scores = []
    for g in range(KV_HEADS):
        kcat = jnp.concatenate([k_at[g][0], k_at[g][1]], axis=0).astype(BF16)
        q_g = jnp.concatenate(qn[g * slabs_per_group:(g + 1) * slabs_per_group], axis=0)
        scores.append(_dot_nt(q_g, kcat))
    yield
    probs, inv_denom = [], []
    for s in range(n_slabs):
        heads = (2 * s, 2 * s + 1)
        g, sg = s // slabs_per_group, s % slabs_per_group
        sc = [jnp.where(valid, scores[g][sg * blk:(sg + 1) * blk, p * 3 * blk:(p + 1) * 3 * blk] + bias_ref[h],
                        -jnp.inf) for p, h in enumerate(heads)]
        sinks = [sink_ref[h] * LOG2E for h in heads]
        m = [jnp.maximum(jnp.max(z, axis=-1, keepdims=True), sk) for z, sk in zip(sc, sinks)]
        e = [jnp.exp2(z - mm) for z, mm in zip(sc, m)]
        inv_denom += [1.0 / (jnp.sum(ee, axis=-1, keepdims=True) + jnp.exp2(sk - mm))
                      for ee, sk, mm in zip(e, sinks, m)]
        probs += [ee.astype(BF16) for ee in e]
        yield
    o_all = _dot(jnp.concatenate(probs, axis=0), v_b)
    for s in range(n_slabs):
        g = (2 * s) // group
        halves = []
        for p in range(2):
            h = 2 * s + p
            o = o_all[h * blk:(h + 1) * blk] * inv_denom[h]
            halves.append(o if p == g else pltpu.roll(o, HEAD_DIM, 1))
        o_ref[0, rows, s * LANES:(s + 1) * LANES] = jnp.where(out_lane_even, halves[0], halves[1])


def _t5_bucket(rel):
    nb = REL_BUCKETS // 2
    max_exact = nb // 2
    ret = jnp.where(rel > 0, nb, 0)
    n = jnp.abs(rel)
    large = max_exact + (jnp.log(jnp.maximum(n, 1).astype(F32) / max_exact)
                         / math.log(REL_MAX_DIST / max_exact) * (nb - max_exact)).astype(jnp.int32)
    large = jnp.minimum(large, nb - 1)
    return ret + jnp.where(n < max_exact, n, large)


BIAS_SPAN = 4 * BLOCK


def _bias_body(tab_ref, o_ref):
    blk = o_ref.shape[1]
    x = jnp.broadcast_to(tab_ref[0], (blk, BIAS_SPAN))
    shifted = pltpu.roll(x, BIAS_SPAN - (blk - 1), 1, stride=1, stride_axis=0)
    o_ref[0] = shifted[:, :3 * blk]


def _bias_table(rel_bias):
    rel = jnp.arange(BIAS_SPAN) - (2 * BLOCK - 1)
    tab = jnp.transpose(rel_bias[_t5_bucket(rel)].astype(F32)) * LOG2E
    shape = (BLOCK, 3 * BLOCK)
    return pl.pallas_call(
        _bias_body,
        grid=(Q_HEADS,),
        in_specs=[pl.BlockSpec((1, 1, BIAS_SPAN), lambda h: (h, 0, 0))],
        out_specs=pl.BlockSpec((1,) + shape, lambda h: (h, 0, 0)),
        out_shape=jax.ShapeDtypeStruct((Q_HEADS,) + shape, F32),
        name="bias_table",
    )(tab.reshape(Q_HEADS, 1, BIAS_SPAN))


def _mixers_body(*refs, n_blocks, per_step):
    scan_refs = refs[:12]
    q_ref, kvp_ref, kvc_ref, kvn_ref, qg_ref, kg_ref, bias_ref, sink_ref = refs[12:20]
    yf_o, yb_o, attn_o, h_scr = refs[20:]
    j = pl.program_id(0)
    kv_blocks = ([kvp_ref[0]] + [kvc_ref[0, s * BLOCK:(s + 1) * BLOCK] for s in range(per_step)] + [kvn_ref[0]])

    def attention():
        for s in range(per_step):
            n = (j * per_step + s) % n_blocks
            yield from _attn_stages(n, n_blocks, q_ref, slice(s * BLOCK, (s + 1) * BLOCK), kv_blocks[s:s + 3],
                                    qg_ref, kg_ref, bias_ref, sink_ref, attn_o)

    _scan_step(j, *scan_refs, yf_o, yb_o, h_scr, filler=attention())


def _mixers(r, v, kk, lw0, lw1, kd0, kd1, b0, b1, q, kv, q_gain, k_gain, rel_bias, sink):
    b, t, c = r.shape
    steps = t // (CHUNKS_PER_STEP * CHUNK)
    nb = t // BLOCK
    per_step = (b * nb) // steps
    assert per_step * steps == b * nb and nb % per_step == 0
    nq = nb // per_step
    fw = lambda j: (0, j, 0)
    bw = lambda j: (0, steps - 1 - j, 0)
    blk = (b, CHUNKS_PER_STEP * CHUNK, c)
    cur = lambda j: (j // nq, j % nq, 0)
    prv = lambda j: (j // nq, jnp.maximum((j % nq) * per_step - 1, 0), 0)
    nxt = lambda j: (j // nq, jnp.minimum((j % nq + 1) * per_step, nb - 1), 0)
    c2 = lambda j: (0, 0)
    c3 = lambda j: (0, 0, 0)
    kvblk = (1, BLOCK, 2 * KV_COLS)
    return pl.pallas_call(
        functools.partial(_mixers_body, n_blocks=nb, per_step=per_step),
        grid=(steps,),
        in_specs=[pl.BlockSpec(blk, fw)] * 6 + [pl.BlockSpec(blk, bw)] * 6
                 + [pl.BlockSpec((1, per_step * BLOCK, ATTN_WIDTH), cur), pl.BlockSpec(kvblk, prv),
                    pl.BlockSpec((1, per_step * BLOCK, 2 * KV_COLS), cur), pl.BlockSpec(kvblk, nxt),
                    pl.BlockSpec((1, LANES), c2), pl.BlockSpec((1, LANES), c2),
                    pl.BlockSpec((Q_HEADS, BLOCK, 3 * BLOCK), c3), pl.BlockSpec(memory_space=pltpu.SMEM)],
        out_specs=[pl.BlockSpec(blk, fw), pl.BlockSpec(blk, bw),
                   pl.BlockSpec((1, per_step * BLOCK, ATTN_WIDTH), cur)],
        out_shape=[jax.ShapeDtypeStruct((b, t, c), F32)] * 2 + [jax.ShapeDtypeStruct((b, t, ATTN_WIDTH), F32)],
        scratch_shapes=[pltpu.VMEM((b, 2, c // PAIR, PAIR, PAIR), F32)],
        compiler_params=pltpu.CompilerParams(dimension_semantics=("arbitrary",), vmem_limit_bytes=VMEM_LIMIT),
        name="mixers",
    )(r, v, kk, lw0, kd0, b0, r, v, kk, lw1, kd1, b1,
      q, kv, kv, kv, jnp.tile(q_gain, 2).reshape(1, LANES), jnp.tile(k_gain, 2).reshape(1, LANES),
      _bias_table(rel_bias), sink.astype(F32))


def _outffn_body(*refs, tiles_per_seq, ff_chunk):
    main, prev, nxt = refs[0:6], refs[6:12], refs[12:18]
    lw_ref, lb_ref, wo_ref, gf_ref, wu_ref, cw_ref, cb_ref, wd_ref, o_ref, act_scr = refs[18:]
    i = pl.program_id(0)
    tile = main[0].shape[0]
    halo = prev[0].shape[0]
    d_ff = wd_ref.shape[0]
    c = RWKV_WIDTH
    total = tile + 2 * halo
    first = (i % tiles_per_seq) == 0
    last = (i % tiles_per_seq) == tiles_per_seq - 1

    def mixed(x, yf, yb, bonus, gate, attn):
        y = yf + yb
        mu = _head_sum(y, 2) * (1.0 / HEAD_DIM)
        yc = y - mu
        var = _head_sum(yc * yc, 1) * (1.0 / HEAD_DIM)
        yn = yc * lax.rsqrt(var + LNX_EPS) * lw_ref[...] + lb_ref[...]
        mix_r = (yn + bonus) * gate
        return x + _dot(mix_r.astype(BF16), wo_ref[:c, :]) + _dot(attn.astype(BF16), wo_ref[c:, :])

    x1 = mixed(*(r[...] for r in main))
    x1cat = jnp.concatenate([mixed(*(r[...] for r in prev)), x1, mixed(*(r[...] for r in nxt))], axis=0)
    hcat = x1cat * lax.rsqrt(jnp.mean(x1cat * x1cat, axis=-1, keepdims=True) + NORM_EPS) * gf_ref[...]
    row = lax.broadcasted_iota(jnp.int32, (total, 1), 0)
    outside = (first & (row < halo)) | (last & (row >= halo + tile))
    hcat = jnp.where(outside, 0.0, hcat).astype(BF16)

    def conv(cols):
        u = _dot(hcat, wu_ref[:, cols])
        up = pltpu.roll(u, 1, 0)[halo:halo + tile]
        un = pltpu.roll(u, total - 1, 0)[halo:halo + tile]
        return (up * cw_ref[0:1, cols] + u[halo:halo + tile] * cw_ref[1:2, cols] + un * cw_ref[2:3, cols]
                + cb_ref[:, cols])

    for lo in range(0, d_ff, ff_chunk):
        hi = min(lo + ff_chunk, d_ff)
        g = conv(slice(lo, hi))
        val = conv(slice(d_ff + lo, d_ff + hi))
        act_scr[:, lo:hi] = (g * _sigmoid(g) * val).astype(BF16)
    o_ref[...] = x1 + _dot(act_scr[...], wd_ref[...])


def _out_ffn(x2, yf, yb, bonus, gate, attn, lnx_w, lnx_b, w_out, g_ffn, w_up, conv_w, conv_b, w_down,
             seq, tile, ff_chunk):
    rows, d = x2.shape
    c = RWKV_WIDTH
    d_ff = w_down.shape[0]
    halo = SUBLANES_F32
    per = tile // halo
    row = lambda i: (i, 0)
    prv = lambda i: (jnp.maximum(i * per - 1, 0), 0)
    nxt = lambda i: (jnp.minimum((i + 1) * per, rows // halo - 1), 0)
    const = lambda i: (0, 0)
    resident = dict(pipeline_mode=pl.Buffered(1))
    streams = (x2, yf, yb, bonus, gate, attn)

    def stream_specs(nrows, index_map):
        return [pl.BlockSpec((nrows, z.shape[1]), index_map) for z in streams]

    body = functools.partial(_outffn_body, tiles_per_seq=seq // tile, ff_chunk=ff_chunk)
    return pl.pallas_call(
        body,
        grid=(rows // tile,),
        in_specs=stream_specs(tile, row) + stream_specs(halo, prv) + stream_specs(halo, nxt)
                 + [pl.BlockSpec((1, c), const), pl.BlockSpec((1, c), const),
                    pl.BlockSpec(w_out.shape, const, **resident), pl.BlockSpec((1, d), const),
                    pl.BlockSpec((d, 2 * d_ff), const, **resident),
                    pl.BlockSpec((3, 2 * d_ff), const), pl.BlockSpec((1, 2 * d_ff), const),
                    pl.BlockSpec((d_ff, d), const, **resident)],
        out_specs=pl.BlockSpec((tile, d), row),
        out_shape=jax.ShapeDtypeStruct((rows, d), F32),
        scratch_shapes=[pltpu.VMEM((tile, d_ff), BF16)],
        compiler_params=pltpu.CompilerParams(dimension_semantics=("parallel",), vmem_limit_bytes=VMEM_LIMIT),
        name="out_ffn",
    )(*streams, *streams, *streams, lnx_w.reshape(1, c), lnx_b.reshape(1, c), w_out.astype(BF16),
      g_ffn.reshape(1, d), w_up.astype(BF16), conv_w, conv_b.reshape(1, 2 * d_ff), w_down.astype(BF16))


def _layer(x, g_mix, w_in, mu_prev, mu_next, w0, w2, a0, a2, g2, k_k, k_a, r_k, lnx_w, lnx_b,
           q_gain, k_gain, rel_bias, sink, w_out, g_ffn, w_up, conv_w, conv_b, w_down):
    b, t, d = x.shape
    rows = b * t
    x2 = x.reshape(rows, d)
    q, kv, *ops = _inproj_prep(x2, g_mix, w_in, mu_prev, mu_next, w0, w2, a0, a2, g2, k_k, k_a, r_k.reshape(-1),
                               seq=t, tile=min(ROW_TILE, t))
    r, v, kk, lw0, lw1, kd0, kd1, b0, b1 = (z.reshape(b, t, RWKV_WIDTH) for z in ops[:9])
    gate, bonus = ops[9:]
    yf, yb, attn = _mixers(r, v, kk, lw0, lw1, kd0, kd1, b0, b1, q.reshape(b, t, ATTN_WIDTH),
                           kv.reshape(b, t, 2 * KV_COLS), q_gain, k_gain, rel_bias, sink)
    flat = lambda z: z.reshape(rows, z.shape[-1])
    out = _out_ffn(x2, flat(yf), flat(yb), bonus, gate, flat(attn), lnx_w, lnx_b, w_out, g_ffn,
                   w_up, conv_w, conv_b, w_down, seq=t, tile=min(ROW_TILE, t), ff_chunk=FF_CHUNK)
    return out.reshape(b, t, d)


def kernel(x, g_mix, w_in, mu_prev, mu_next, w0, w2, a0, a2, g2, k_k, k_a, r_k, lnx_w, lnx_b, q_gain, k_gain,
           rel_bias, sink, w_out, g_ffn, w_up, conv_w, conv_b, w_down):
    depth = g_mix.shape[0]
    for l in range(depth):
        x = _layer(x, g_mix[l], w_in[l], mu_prev[l], mu_next[l], w0[l], w2[l], a0[l], a2[l], g2[l], k_k[l], k_a[l],
                   r_k[l], lnx_w[l], lnx_b[l], q_gain[l], k_gain[l], rel_bias, sink[l], w_out[l], g_ffn[l],
                   w_up[l], conv_w[l], conv_b[l], w_down[l])
    return x
```

```python
import functools
import math

import jax
import jax.numpy as jnp
from jax import lax
from jax.experimental import pallas as pl
from jax.experimental.pallas import tpu as pltpu

F32 = jnp.float32
BF16 = jnp.bfloat16

HEAD_DIM = 64
RWKV_WIDTH = 512
ATTN_WIDTH = 512
KV_HEADS = 2
Q_HEADS = 8
DECAY_LORA = 64
ICLR_LORA = 64
GATE_LORA = 128
RWKV_COLS = 3 * RWKV_WIDTH + DECAY_LORA + ICLR_LORA + GATE_LORA
KV_COLS = KV_HEADS * HEAD_DIM
WINDOW = 128
BLOCK = 128
REL_BUCKETS = 32
REL_MAX_DIST = 128
NORM_EPS = 1e-6
LNX_EPS = 64e-5
KK_EPS = 1e-12
LOG2E = math.log2(math.e)

LANES = 128
MXU_DIM = 256
SUBLANES_F32 = 8
SUBLANES_BF16 = 16
VMEM_LIMIT = 48 * 1024 * 1024

ROW_TILE = 512
FF_CHUNK = 768
CHUNK = 64
CHUNKS_PER_STEP = 2
PAIR = 2 * HEAD_DIM
assert PAIR == LANES
HEAD_SHIFT = HEAD_DIM.bit_length() - 1
assert 1 << HEAD_SHIFT == HEAD_DIM


def _dot(a, b):
    return jnp.dot(a, b, preferred_element_type=F32)


def _dot_nt(a, b):
    return lax.dot_general(a, b, (((1,), (1,)), ((), ())), preferred_element_type=F32)


def _dot_tn(a, b):
    return lax.dot_general(a, b, (((0,), (0,)), ((), ())), preferred_element_type=F32)


def _sigmoid(x):
    return 1.0 / (1.0 + jnp.exp(-x))


def _split_bf16(x, parts):
    out = []
    for _ in range(parts):
        h = x.astype(BF16)
        out.append(h)
        x = x - h.astype(F32)
    return out


def _head_sum(x, parts):
    width = x.shape[-1]
    blk = min(width, MXU_DIM)
    r = lax.broadcasted_iota(jnp.int32, (blk, blk), 0) >> HEAD_SHIFT
    c = lax.broadcasted_iota(jnp.int32, (blk, blk), 1) >> HEAD_SHIFT
    m = (r == c).astype(BF16)
    pieces = _split_bf16(x, parts)
    cols = []
    for c0 in range(0, width, blk):
        acc = None
        for h in pieces:
            t = _dot(h[:, c0:c0 + blk], m)
            acc = t if acc is None else acc + t
        cols.append(acc)
    return cols[0] if len(cols) == 1 else jnp.concatenate(cols, axis=1)


def _inprep_body(x_ref, xp_ref, xn_ref, g_ref, wr_ref, wq_ref, wkv_ref,
                 mup_ref, mun_ref, w0_ref, w2_ref, a0_ref, a2_ref, g2_ref, kk_ref, ka_ref, rk_ref,
                 q_o, kv_o, r_o, v_o, kk_o, lw0_o, lw1_o, kd0_o, kd1_o, b0_o, b1_o, gate_o, bonus_o,
                 *, tiles_per_seq):
    i = pl.program_id(0)
    tile = x_ref.shape[0]
    halo = xp_ref.shape[0]
    first = (i % tiles_per_seq) == 0
    last = (i % tiles_per_seq) == tiles_per_seq - 1

    def norm(x):
        return (x * lax.rsqrt(jnp.mean(x * x, axis=-1, keepdims=True) + NORM_EPS) * g_ref[...]).astype(BF16)

    h = norm(x_ref[...])
    q_o[...] = _dot(h, wq_ref[...])
    kv_o[...] = _dot(h, wkv_ref[...])
    hp = norm(xp_ref[...])
    hn = norm(xn_ref[...])
    hp = jnp.where(first, jnp.zeros_like(hp), hp)
    hn = jnp.where(last, jnp.zeros_like(hn), hn)
    hcat = jnp.concatenate([hp, h, hn], axis=0)
    total = tile + 2 * halo

    def shifted(lo, hi):
        pc = _dot(hcat, wr_ref[:, lo:hi])
        p = pc[halo:halo + tile]
        prev = pltpu.roll(pc, 1, 0)[halo:halo + tile]
        nxt = pltpu.roll(pc, total - 1, 0)[halo:halo + tile]
        return p + mup_ref[:, lo:hi] * (prev - p) + mun_ref[:, lo:hi] * (nxt - p)

    c = RWKV_WIDTH
    r, k, v = shifted(0, c), shifted(c, 2 * c), shifted(2 * c, 3 * c)
    tail = shifted(3 * c, RWKV_COLS)
    lora = tail[:, :LANES]
    xg = tail[:, LANES:]
    lora_t = jnp.tanh(lora).astype(BF16)
    lora_b = lora.astype(BF16)

    kx = k * kk_ref[...]
    kk = kx * lax.rsqrt(_head_sum(kx * kx, 1) + KK_EPS)
    gate = _dot(_sigmoid(xg).astype(BF16), g2_ref[...])

    kds = []
    for d, (lw_o, kd_o, b_o) in enumerate(((lw0_o, kd0_o, b0_o), (lw1_o, kd1_o, b1_o))):
        w_raw = w0_ref[d:d + 1, :] + _dot(lora_t, w2_ref[d])
        lw_o[...] = (-math.exp(-0.5)) * _sigmoid(w_raw)
        iclr = _sigmoid(a0_ref[d:d + 1, :] + _dot(lora_b, a2_ref[d]))
        kd = k * (1.0 + (iclr - 1.0) * ka_ref[...])
        kd_o[...] = kd
        b_o[...] = iclr * kk
        kds.append(kd)

    r_o[...] = r
    v_o[...] = v
    kk_o[...] = kk
    gate_o[...] = gate
    bonus_o[...] = _head_sum(r * (kds[0] + kds[1]) * rk_ref[...], 1) * v


def _inproj_prep(x2, g_mix, w_in, mu_prev, mu_next, w0, w2, a0, a2, g2, k_k, k_a, r_k, seq, tile):
    rows, d = x2.shape
    c = RWKV_WIDTH
    cols = RWKV_COLS
    halo = SUBLANES_BF16
    per = tile // halo
    wb = w_in.astype(BF16)
    wr, wq, wkv = wb[:, :cols], wb[:, cols:cols + ATTN_WIDTH], wb[:, cols + ATTN_WIDTH:]
    zeros = jnp.zeros((2, DECAY_LORA, c), F32)
    w2p = jnp.concatenate([w2, zeros], axis=1).astype(BF16)
    a2p = jnp.concatenate([zeros, a2], axis=1).astype(BF16)
    row = lambda i: (i, 0)
    prv = lambda i: (jnp.maximum(i * per - 1, 0), 0)
    nxt = lambda i: (jnp.minimum((i + 1) * per, rows // halo - 1), 0)
    c2 = lambda i: (0, 0)
    c3 = lambda i: (0, 0, 0)
    resident = dict(pipeline_mode=pl.Buffered(1))
    out = jax.ShapeDtypeStruct((rows, c), F32)
    body = functools.partial(_inprep_body, tiles_per_seq=seq // tile)
    return pl.pallas_call(
        body,
        grid=(rows // tile,),
        in_specs=[pl.BlockSpec((tile, d), row), pl.BlockSpec((halo, d), prv), pl.BlockSpec((halo, d), nxt),
                  pl.BlockSpec((1, d), c2),
                  pl.BlockSpec(wr.shape, c2, **resident), pl.BlockSpec(wq.shape, c2, **resident),
                  pl.BlockSpec(wkv.shape, c2, **resident),
                  pl.BlockSpec((1, cols), c2), pl.BlockSpec((1, cols), c2),
                  pl.BlockSpec((2, c), c2), pl.BlockSpec((2, LANES, c), c3),
                  pl.BlockSpec((2, c), c2), pl.BlockSpec((2, LANES, c), c3),
                  pl.BlockSpec((GATE_LORA, c), c2),
                  pl.BlockSpec((1, c), c2), pl.BlockSpec((1, c), c2), pl.BlockSpec((1, c), c2)],
        out_specs=[pl.BlockSpec((tile, ATTN_WIDTH), row), pl.BlockSpec((tile, 2 * KV_COLS), row)]
                  + [pl.BlockSpec((tile, c), row)] * 11,
        out_shape=[jax.ShapeDtypeStruct((rows, ATTN_WIDTH), F32), jax.ShapeDtypeStruct((rows, 2 * KV_COLS), F32)]
                  + [out] * 11,
        compiler_params=pltpu.CompilerParams(dimension_semantics=("parallel",), vmem_limit_bytes=VMEM_LIMIT),
        name="inproj_prep",
    )(x2, x2, x2, g_mix.reshape(1, d), wr, wq, wkv, mu_prev.reshape(1, cols), mu_next.reshape(1, cols),
      w0, w2p, a0, a2p, g2.astype(BF16), k_k.reshape(1, c), k_a.reshape(1, c), r_k.reshape(1, c))


def _scan_step(j, *refs, filler=iter(())):
    h_scr = refs[-1]

    @pl.when(j == 0)
    def _():
        h_scr[...] = jnp.zeros_like(h_scr)

    n_batch, rows = refs[0].shape[0], refs[0].shape[1]
    members = [(bi, d, sub) for bi in range(n_batch) for d in range(2) for sub in range(rows // CHUNK)]
    for stage, _ in enumerate(_scan_stages(members, *refs)):
        if stage >= 2:
            next(filler, None)
    for _ in filler:
        pass


def _scan_stages(members, rf, vf, kkf, lwf, kdf, bf, rb, vb, kkb, lwb, kdb, bb, yf_o, yb_o, h_scr):
    cs = CHUNK
    subs = rf.shape[1] // cs
    n_pairs = rf.shape[2] // PAIR

    cs_shift = cs.bit_length() - 1
    assert 1 << cs_shift == cs
    t_row = lax.broadcasted_iota(jnp.int32, (cs, 2 * cs), 0)
    t_col = lax.broadcasted_iota(jnp.int32, (cs, 2 * cs), 1) & (cs - 1)
    eye_cat = (t_row == t_col).astype(F32)
    same_block = lambda log2_size: (t_row >> log2_size) == (t_col >> log2_size)
    ri = lax.broadcasted_iota(jnp.int32, (PAIR, PAIR), 0)
    ci = lax.broadcasted_iota(jnp.int32, (PAIR, PAIR), 1)
    eye = (ri == ci).astype(F32)
    same_head = (ri >> HEAD_SHIFT) == (ci >> HEAD_SHIFT)
    tr = lax.broadcasted_iota(jnp.int32, (cs, cs), 0)
    tc = lax.broadcasted_iota(jnp.int32, (cs, cs), 1)
    lane_even = lax.broadcasted_iota(jnp.int32, (cs, PAIR), 1) < HEAD_DIM
    col_first = lax.broadcasted_iota(jnp.int32, (cs, 2 * cs), 1) < cs

    def stack(x):
        zero = jnp.zeros_like(x)
        return jnp.concatenate([jnp.where(lane_even, x, zero), jnp.where(lane_even, zero, x)], axis=0)

    def block_diag(m):
        zero = jnp.zeros_like(m)
        return jnp.concatenate([jnp.where(col_first, m, zero), jnp.where(col_first, zero, m)], axis=0)

    dirs = ((0, rf, vf, kkf, lwf, kdf, bf, yf_o), (1, rb, vb, kkb, lwb, kdb, bb, yb_o))
    chains = []
    for bi, sub, (d, r_ref, v_ref, kk_ref, lw_ref, kd_ref, b_ref, y_o) in (
            (bi, sub, dirs[d]) for bi, d, sub in members):
        fwd = d == 0
        rows = slice(sub * cs, (sub + 1) * cs)
        strict = (t_col < t_row) if fwd else (t_col > t_row)
        incl = strict | (t_col == t_row)
        cum = ((tc <= tr) if fwd else (tc >= tr)).astype(BF16)

        lw = lw_ref[bi, rows]
        c_in = sum(_dot(cum, piece) for piece in _split_bf16(lw, 2))
        c_ex = c_in - lw
        c_tot = jnp.sum(lw, axis=0, keepdims=True)
        gam = jnp.exp(c_tot)
        kd = kd_ref[bi, rows]
        bv = b_ref[bi, rows]
        e_neg = jnp.exp(-c_in)
        e_end = jnp.exp(c_tot - c_in)
        a_t = -kk_ref[bi, rows] * jnp.exp(c_ex)
        r_t = r_ref[bi, rows] * jnp.exp(c_in)
        b_t = bv * e_neg
        k_t = kd * e_neg
        b_h = bv * e_end
        k_h = kd * e_end
        vv = v_ref[bi, rows]
        for pr in range(n_pairs):
            sl = slice(pr * PAIR, (pr + 1) * PAIR)
            a_b, v_b = a_t[:, sl].astype(BF16), vv[:, sl].astype(BF16)
            chains.append(dict(
                bi=bi, d=d, pr=pr, sl=sl, rows=rows, order=sub if fwd else subs - 1 - sub,
                y_o=y_o, strict=strict, incl=incl, r_t=r_t[:, sl], gam=gam[:, sl],
                a_b=a_b, r_b=r_t[:, sl].astype(BF16), a_sb=stack(a_b), v_b=v_b, v_sb=stack(v_b),
                bt_sb=stack(b_t[:, sl].astype(BF16)), kt_sb=stack(k_t[:, sl].astype(BF16)),
                bk_h=jnp.concatenate([b_h[:, sl].astype(BF16), k_h[:, sl].astype(BF16)], axis=0)))
    yield

    h2 = 2 * cs
    for c in chains:
        lhs = jnp.concatenate([c["a_b"], c["r_b"]], axis=0)
        rhs = jnp.concatenate([c["bt_sb"], c["kt_sb"]], axis=0)
        gram = _dot_nt(lhs, rhs)
        a_ab = jnp.where(c["strict"], gram[:cs, :h2], 0.0)
        c["a_xk"] = jnp.concatenate([jnp.where(c["strict"], gram[:cs, h2:], 0.0),
                                     jnp.where(c["incl"], gram[cs:, h2:], 0.0)], axis=0).astype(BF16)
        c["a_rb"] = jnp.where(c["incl"], gram[cs:, :h2], 0.0).astype(BF16)
        c["n_cat"] = a_ab.astype(BF16)
        c["s_cat"] = (eye_cat + jnp.where(same_block(1), a_ab, 0.0)).astype(BF16)
    yield

    for level in range(1, cs_shift):
        join = same_block(level + 1) & ~same_block(level)
        zero = jnp.zeros((cs, h2), BF16)
        for c in chains:
            tn = _dot(c["s_cat"], block_diag(jnp.where(join, c["n_cat"], zero)))
            c["z_cat"] = (eye_cat + tn).astype(BF16)
        yield
        for c in chains:
            c["s_cat"] = _dot(c["z_cat"], block_diag(c["s_cat"])).astype(BF16)
        yield

    zero_b = jnp.zeros((cs, PAIR), BF16)
    for c in chains:
        av = _dot(c["a_xk"], c["v_sb"])
        c["w1_sb"] = stack(av[:cs].astype(BF16))
        c["av"] = av[cs:]
        c["gam_col"] = jnp.sum(eye * c["gam"], axis=1, keepdims=True)
    yield
    for c in chains:
        c["pq"] = _dot(c["s_cat"], jnp.concatenate([c["a_sb"], c["w1_sb"]], axis=1)).astype(BF16)
    yield
    for c in chains:
        pq = c["pq"]
        ry = _dot(c["a_rb"], jnp.concatenate([stack(pq[:, :PAIR]), stack(pq[:, PAIR:])], axis=1))
        c["r_hat"] = (c["r_t"] + ry[:, :PAIR]).astype(BF16)
        c["y_hat"] = ry[:, PAIR:] + c["av"]
        rhs = jnp.concatenate([pq, jnp.concatenate([zero_b, c["v_b"]], axis=1)], axis=0)
        gd = _dot_tn(c["bk_h"], rhs)
        c["btp"] = jnp.where(same_head, gd[:, :PAIR], 0.0).astype(BF16)
        c["dd"] = jnp.where(same_head, gd[:, PAIR:], 0.0)
    for order in range(subs):
        yield
        for c in (c for c in chains if c["order"] == order):
            h0 = h_scr[c["bi"], c["d"], c["pr"]]
            yh = _dot(jnp.concatenate([c["r_hat"], c["btp"]], axis=0), h0.astype(BF16))
            c["y_o"][c["bi"], c["rows"], c["sl"]] = yh[:cs] + c["y_hat"]
            h_scr[c["bi"], c["d"], c["pr"]] = c["gam_col"] * h0 + yh[cs:] + c["dd"]


def _attn_stages(n, nb, q_ref, rows, kv_window, qg_ref, kg_ref, bias_ref, sink_ref, o_ref):
    blk = rows.stop - rows.start
    head_mean = lambda z: _head_sum(z, 1) * (1.0 / HEAD_DIM)

    k_win = jnp.concatenate([kv[:, :KV_COLS] for kv in kv_window], axis=0)
    v_win = jnp.concatenate([kv[:, KV_COLS:] for kv in kv_window], axis=0)
    kn = k_win * lax.rsqrt(head_mean(k_win * k_win) + NORM_EPS) * kg_ref[...]
    v_b = v_win.astype(BF16)

    lane = lax.broadcasted_iota(jnp.int32, kn.shape, 1)
    k_at = []
    for g in range(KV_HEADS):
        own = jnp.where((lane >> HEAD_SHIFT) == g, kn, 0.0)
        other = pltpu.roll(own, HEAD_DIM, 1)
        k_at.append([own if p == g else other for p in range(2)])

    row = lax.broadcasted_iota(jnp.int32, (blk, 3 * blk), 0)
    col = lax.broadcasted_iota(jnp.int32, (blk, 3 * blk), 1)
    rel = col - blk - row
    valid = (jnp.abs(rel) <= WINDOW)
    valid &= (col >= blk) | (n > 0)
    valid &= (col < 2 * blk) | (n < nb - 1)
    out_lane_even = lax.broadcasted_iota(jnp.int32, (blk, LANES), 1) < HEAD_DIM

    group = Q_HEADS // KV_HEADS
    n_slabs = ATTN_WIDTH // LANES
    slabs_per_group = n_slabs // KV_HEADS
    qn = []
    for s in range(n_slabs):
        q = q_ref[0, rows, s * LANES:(s + 1) * LANES]
        qn.append((q * lax.rsqrt(head_mean(q * q) + NORM_EPS) * qg_ref[...]
                   * (HEAD_DIM ** -0.5 * LOG2E)).astype(BF16))
    scores = []
    for g in range(KV_HEADS):
        kcat = jnp.concatenate([k_at[g][0], k_at[g][1]], axis=0).astype(BF16)
        q_g = jnp.concatenate(qn[g * slabs_per_group:(g + 1) * slabs_per_group], axis=0)
        scores.append(_dot_nt(q_g, kcat))
    yield
    probs, inv_denom = [], []
    for s in range(n_slabs):
        heads = (2 * s, 2 * s + 1)
        g, sg = s // slabs_per_group, s % slabs_per_group
        sc = [jnp.where(valid, scores[g][sg * blk:(sg + 1) * blk, p * 3 * blk:(p + 1) * 3 * blk] + bias_ref[h],
                        -jnp.inf) for p, h in enumerate(heads)]
        sinks = [sink_ref[h] * LOG2E for h in heads]
        m = [jnp.maximum(jnp.max(z, axis=-1, keepdims=True), sk) for z, sk in zip(sc, sinks)]
        e = [jnp.exp2(z - mm) for z, mm in zip(sc, m)]
        inv_denom += [1.0 / (jnp.sum(ee, axis=-1, keepdims=True) + jnp.exp2(sk - mm))
                      for ee, sk, mm in zip(e, sinks, m)]
        probs += [ee.astype(BF16) for ee in e]
        yield
    o_all = _dot(jnp.concatenate(probs, axis=0), v_b)
    for s in range(n_slabs):
        g = (2 * s) // group
        halves = []
        for p in range(2):
            h = 2 * s + p
            o = o_all[h * blk:(h + 1) * blk] * inv_denom[h]
            halves.append(o if p == g else pltpu.roll(o, HEAD_DIM, 1))
        o_ref[0, rows, s * LANES:(s + 1) * LANES] = jnp.where(out_lane_even, halves[0], halves[1])


def _t5_bucket(rel):
    nb = REL_BUCKETS // 2
    max_exact = nb // 2
    ret = jnp.where(rel > 0, nb, 0)
    n = jnp.abs(rel)
    large = max_exact + (jnp.log(jnp.maximum(n, 1).astype(F32) / max_exact)
                         / math.log(REL_MAX_DIST / max_exact) * (nb - max_exact)).astype(jnp.int32)
    large = jnp.minimum(large, nb - 1)
    return ret + jnp.where(n < max_exact, n, large)


BIAS_SPAN = 4 * BLOCK


def _bias_body(tab_ref, o_ref):
    blk = o_ref.shape[1]
    x = jnp.broadcast_to(tab_ref[0], (blk, BIAS_SPAN))
    shifted = pltpu.roll(x, BIAS_SPAN - (blk - 1), 1, stride=1, stride_axis=0)
    o_ref[0] = shifted[:, :3 * blk]


def _bias_table(rel_bias):
    rel = jnp.arange(BIAS_SPAN) - (2 * BLOCK - 1)
    tab = jnp.transpose(rel_bias[_t5_bucket(rel)].astype(F32)) * LOG2E
    shape = (BLOCK, 3 * BLOCK)
    return pl.pallas_call(
        _bias_body,
        grid=(Q_HEADS,),
        in_specs=[pl.BlockSpec((1, 1, BIAS_SPAN), lambda h: (h, 0, 0))],
        out_specs=pl.BlockSpec((1,) + shape, lambda h: (h, 0, 0)),
        out_shape=jax.ShapeDtypeStruct((Q_HEADS,) + shape, F32),
        name="bias_table",
    )(tab.reshape(Q_HEADS, 1, BIAS_SPAN))


def _mixers_body(*refs, n_blocks, per_step):
    scan_refs = refs[:12]
    q_ref, kvp_ref, kvc_ref, kvn_ref, qg_ref, kg_ref, bias_ref, sink_ref = refs[12:20]
    yf_o, yb_o, attn_o, h_scr = refs[20:]
    j = pl.program_id(0)
    kv_blocks = ([kvp_ref[0]] + [kvc_ref[0, s * BLOCK:(s + 1) * BLOCK] for s in range(per_step)] + [kvn_ref[0]])

    def attention():
        for s in range(per_step):
            n = (j * per_step + s) % n_blocks
            yield from _attn_stages(n, n_blocks, q_ref, slice(s * BLOCK, (s + 1) * BLOCK), kv_blocks[s:s + 3],
                                    qg_ref, kg_ref, bias_ref, sink_ref, attn_o)

    _scan_step(j, *scan_refs, yf_o, yb_o, h_scr, filler=attention())


def _mixers(r, v, kk, lw0, lw1, kd0, kd1, b0, b1, q, kv, q_gain, k_gain, rel_bias, sink):
    b, t, c = r.shape
    steps = t // (CHUNKS_PER_STEP * CHUNK)
    nb = t // BLOCK
    per_step = (b * nb) // steps
    assert per_step * steps == b * nb and nb % per_step == 0
    nq = nb // per_step
    fw = lambda j: (0, j, 0)
    bw = lambda j: (0, steps - 1 - j, 0)
    blk = (b, CHUNKS_PER_STEP * CHUNK, c)
    cur = lambda j: (j // nq, j % nq, 0)
    prv = lambda j: (j // nq, jnp.maximum((j % nq) * per_step - 1, 0), 0)
    nxt = lambda j: (j // nq, jnp.minimum((j % nq + 1) * per_step, nb - 1), 0)
    c2 = lambda j: (0, 0)
    c3 = lambda j: (0, 0, 0)
    kvblk = (1, BLOCK, 2 * KV_COLS)
    return pl.pallas_call(
        functools.partial(_mixers_body, n_blocks=nb, per_step=per_step),
        grid=(steps,),
        in_specs=[pl.BlockSpec(blk, fw)] * 6 + [pl.BlockSpec(blk, bw)] * 6
                 + [pl.BlockSpec((1, per_step * BLOCK, ATTN_WIDTH), cur), pl.BlockSpec(kvblk, prv),
                    pl.BlockSpec((1, per_step * BLOCK, 2 * KV_COLS), cur), pl.BlockSpec(kvblk, nxt),
                    pl.BlockSpec((1, LANES), c2), pl.BlockSpec((1, LANES), c2),
                    pl.BlockSpec((Q_HEADS, BLOCK, 3 * BLOCK), c3), pl.BlockSpec(memory_space=pltpu.SMEM)],
        out_specs=[pl.BlockSpec(blk, fw), pl.BlockSpec(blk, bw),
                   pl.BlockSpec((1, per_step * BLOCK, ATTN_WIDTH), cur)],
        out_shape=[jax.ShapeDtypeStruct((b, t, c), F32)] * 2 + [jax.ShapeDtypeStruct((b, t, ATTN_WIDTH), F32)],
        scratch_shapes=[pltpu.VMEM((b, 2, c // PAIR, PAIR, PAIR), F32)],
        compiler_params=pltpu.CompilerParams(dimension_semantics=("arbitrary",), vmem_limit_bytes=VMEM_LIMIT),
        name="mixers",
    )(r, v, kk, lw0, kd0, b0, r, v, kk, lw1, kd1, b1,
      q, kv, kv, kv, jnp.tile(q_gain, 2).reshape(1, LANES), jnp.tile(k_gain, 2).reshape(1, LANES),
      _bias_table(rel_bias), sink.astype(F32))


def _outffn_body(*refs, tiles_per_seq, ff_chunk):
    main, prev, nxt = refs[0:6], refs[6:12], refs[12:18]
    lw_ref, lb_ref, wo_ref, gf_ref, wu_ref, cw_ref, cb_ref, wd_ref, o_ref, act_scr = refs[18:]
    i = pl.program_id(0)
    tile = main[0].shape[0]
    halo = prev[0].shape[0]
    d_ff = wd_ref.shape[0]
    c = RWKV_WIDTH
    total = tile + 2 * halo
    first = (i % tiles_per_seq) == 0
    last = (i % tiles_per_seq) == tiles_per_seq - 1

    def mixed(x, yf, yb, bonus, gate, attn):
        y = yf + yb
        mu = _head_sum(y, 2) * (1.0 / HEAD_DIM)
        yc = y - mu
        var = _head_sum(yc * yc, 1) * (1.0 / HEAD_DIM)
        yn = yc * lax.rsqrt(var + LNX_EPS) * lw_ref[...] + lb_ref[...]
        mix_r = (yn + bonus) * gate
        return x + _dot(mix_r.astype(BF16), wo_ref[:c, :]) + _dot(attn.astype(BF16), wo_ref[c:, :])

    x1cat = mixed(*(jnp.concatenate([p[...], m[...], n[...]], axis=0) for p, m, n in zip(prev, main, nxt)))
    x1 = x1cat[halo:halo + tile]
    hcat = x1cat * lax.rsqrt(jnp.mean(x1cat * x1cat, axis=-1, keepdims=True) + NORM_EPS) * gf_ref[...]
    row = lax.broadcasted_iota(jnp.int32, (total, 1), 0)
    outside = (first & (row < halo)) | (last & (row >= halo + tile))
    hcat = jnp.where(outside, 0.0, hcat).astype(BF16)

    def conv(cols):
        u = _dot(hcat, wu_ref[:, cols])
        up = pltpu.roll(u, 1, 0)[halo:halo + tile]
        un = pltpu.roll(u, total - 1, 0)[halo:halo + tile]
        return (up * cw_ref[0:1, cols] + u[halo:halo + tile] * cw_ref[1:2, cols] + un * cw_ref[2:3, cols]
                + cb_ref[:, cols])

    for lo in range(0, d_ff, ff_chunk):
        hi = min(lo + ff_chunk, d_ff)
        g = conv(slice(lo, hi))
        val = conv(slice(d_ff + lo, d_ff + hi))
        act_scr[:, lo:hi] = (g * _sigmoid(g) * val).astype(BF16)
    o_ref[...] = x1 + _dot(act_scr[...], wd_ref[...])


def _out_ffn(x2, yf, yb, bonus, gate, attn, lnx_w, lnx_b, w_out, g_ffn, w_up, conv_w, conv_b, w_down,
             seq, tile, ff_chunk):
    rows, d = x2.shape
    c = RWKV_WIDTH
    d_ff = w_down.shape[0]
    halo = SUBLANES_F32
    per = tile // halo
    row = lambda i: (i, 0)
    prv = lambda i: (jnp.maximum(i * per - 1, 0), 0)
    nxt = lambda i: (jnp.minimum((i + 1) * per, rows // halo - 1), 0)
    const = lambda i: (0, 0)
    resident = dict(pipeline_mode=pl.Buffered(1))
    streams = (x2, yf, yb, bonus, gate, attn)

    def stream_specs(nrows, index_map):
        return [pl.BlockSpec((nrows, z.shape[1]), index_map) for z in streams]

    body = functools.partial(_outffn_body, tiles_per_seq=seq // tile, ff_chunk=ff_chunk)
    return pl.pallas_call(
        body,
        grid=(rows // tile,),
        in_specs=stream_specs(tile, row) + stream_specs(halo, prv) + stream_specs(halo, nxt)
                 + [pl.BlockSpec((1, c), const), pl.BlockSpec((1, c), const),
                    pl.BlockSpec(w_out.shape, const, **resident), pl.BlockSpec((1, d), const),
                    pl.BlockSpec((d, 2 * d_ff), const, **resident),
                    pl.BlockSpec((3, 2 * d_ff), const), pl.BlockSpec((1, 2 * d_ff), const),
                    pl.BlockSpec((d_ff, d), const, **resident)],
        out_specs=pl.BlockSpec((tile, d), row),
        out_shape=jax.ShapeDtypeStruct((rows, d), F32),
        scratch_shapes=[pltpu.VMEM((tile, d_ff), BF16)],
        compiler_params=pltpu.CompilerParams(dimension_semantics=("parallel",), vmem_limit_bytes=VMEM_LIMIT),
        name="out_ffn",
    )(*streams, *streams, *streams, lnx_w.reshape(1, c), lnx_b.reshape(1, c), w_out.astype(BF16),
      g_ffn.reshape(1, d), w_up.astype(BF16), conv_w, conv_b.reshape(1, 2 * d_ff), w_down.astype(BF16))


def _layer(x, g_mix, w_in, mu_prev, mu_next, w0, w2, a0, a2, g2, k_k, k_a, r_k, lnx_w, lnx_b,
           q_gain, k_gain, rel_bias, sink, w_out, g_ffn, w_up, conv_w, conv_b, w_down):
    b, t, d = x.shape
    rows = b * t
    x2 = x.reshape(rows, d)
    q, kv, *ops = _inproj_prep(x2, g_mix, w_in, mu_prev, mu_next, w0, w2, a0, a2, g2, k_k, k_a, r_k.reshape(-1),
                               seq=t, tile=min(ROW_TILE, t))
    r, v, kk, lw0, lw1, kd0, kd1, b0, b1 = (z.reshape(b, t, RWKV_WIDTH) for z in ops[:9])
    gate, bonus = ops[9:]
    yf, yb, attn = _mixers(r, v, kk, lw0, lw1, kd0, kd1, b0, b1, q.reshape(b, t, ATTN_WIDTH),
                           kv.reshape(b, t, 2 * KV_COLS), q_gain, k_gain, rel_bias, sink)
    flat = lambda z: z.reshape(rows, z.shape[-1])
    out = _out_ffn(x2, flat(yf), flat(yb), bonus, gate, flat(attn), lnx_w, lnx_b, w_out, g_ffn,
                   w_up, conv_w, conv_b, w_down, seq=t, tile=min(ROW_TILE, t), ff_chunk=FF_CHUNK)
    return out.reshape(b, t, d)


def kernel(x, g_mix, w_in, mu_prev, mu_next, w0, w2, a0, a2, g2, k_k, k_a, r_k, lnx_w, lnx_b, q_gain, k_gain,
           rel_bias, sink, w_out, g_ffn, w_up, conv_w, conv_b, w_down):
    depth = g_mix.shape[0]
    for l in range(depth):
        x = _layer(x, g_mix[l], w_in[l], mu_prev[l], mu_next[l], w0[l], w2[l], a0[l], a2[l], g2[l], k_k[l], k_a[l],
                   r_k[l], lnx_w[l], lnx_b[l], q_gain[l], k_gain[l], rel_bias, sink[l], w_out[l], g_ffn[l],
                   w_up[l], conv_w[l], conv_b[l], w_down[l])
    return x
```

```python
import functools
import math

import jax
import jax.numpy as jnp
from jax import lax
from jax.experimental import pallas as pl
from jax.experimental.pallas import tpu as pltpu

F32 = jnp.float32
BF16 = jnp.bfloat16

HEAD_DIM = 64
RWKV_WIDTH = 512
ATTN_WIDTH = 512
KV_HEADS = 2
Q_HEADS = 8
DECAY_LORA = 64
ICLR_LORA = 64
GATE_LORA = 128
RWKV_COLS = 3 * RWKV_WIDTH + DECAY_LORA + ICLR_LORA + GATE_LORA
KV_COLS = KV_HEADS * HEAD_DIM
WINDOW = 128
BLOCK = 128
REL_BUCKETS = 32
REL_MAX_DIST = 128
NORM_EPS = 1e-6
LNX_EPS = 64e-5
KK_EPS = 1e-12
LOG2E = math.log2(math.e)

LANES = 128
MXU_DIM = 256
SUBLANES_F32 = 8
SUBLANES_BF16 = 16
VMEM_LIMIT = 48 * 1024 * 1024

ROW_TILE = 512
FF_CHUNK = 768
CHUNK = 64
CHUNKS_PER_STEP = 2
PAIR = 2 * HEAD_DIM
assert PAIR == LANES
HEAD_SHIFT = HEAD_DIM.bit_length() - 1
assert 1 << HEAD_SHIFT == HEAD_DIM


def _dot(a, b):
    return jnp.dot(a, b, preferred_element_type=F32)


def _dot_nt(a, b):
    return lax.dot_general(a, b, (((1,), (1,)), ((), ())), preferred_element_type=F32)


def _dot_tn(a, b):
    return lax.dot_general(a, b, (((0,), (0,)), ((), ())), preferred_element_type=F32)


def _sigmoid(x):
    return 1.0 / (1.0 + jnp.exp(-x))


def _split_bf16(x, parts):
    out = []
    for _ in range(parts):
        h = x.astype(BF16)
        out.append(h)
        x = x - h.astype(F32)
    return out


def _head_sum(x, parts):
    width = x.shape[-1]
    blk = min(width, MXU_DIM)
    r = lax.broadcasted_iota(jnp.int32, (blk, blk), 0) >> HEAD_SHIFT
    c = lax.broadcasted_iota(jnp.int32, (blk, blk), 1) >> HEAD_SHIFT
    m = (r == c).astype(BF16)
    pieces = _split_bf16(x, parts)
    cols = []
    for c0 in range(0, width, blk):
        acc = None
        for h in pieces:
            t = _dot(h[:, c0:c0 + blk], m)
            acc = t if acc is None else acc + t
        cols.append(acc)
    return cols[0] if len(cols) == 1 else jnp.concatenate(cols, axis=1)


def _inprep_body(x_ref, xp_ref, xn_ref, g_ref, wr_ref, wq_ref, wkv_ref,
                 mup_ref, mun_ref, w0_ref, w2_ref, a0_ref, a2_ref, g2_ref, kk_ref, ka_ref, rk_ref,
                 q_o, kv_o, shared_o, dir0_o, dir1_o, gate_o, bonus_o,
                 *, tiles_per_seq):
    i = pl.program_id(0)
    tile = x_ref.shape[0]
    halo = xp_ref.shape[0]
    first = (i % tiles_per_seq) == 0
    last = (i % tiles_per_seq) == tiles_per_seq - 1

    def norm(x):
        return (x * lax.rsqrt(jnp.mean(x * x, axis=-1, keepdims=True) + NORM_EPS) * g_ref[...]).astype(BF16)

    h = norm(x_ref[...])
    q_o[...] = _dot(h, wq_ref[...])
    kv_o[...] = _dot(h, wkv_ref[...])
    hp = norm(xp_ref[...])
    hn = norm(xn_ref[...])
    hp = jnp.where(first, jnp.zeros_like(hp), hp)
    hn = jnp.where(last, jnp.zeros_like(hn), hn)
    hcat = jnp.concatenate([hp, h, hn], axis=0)
    total = tile + 2 * halo

    def shifted(lo, hi):
        pc = _dot(hcat, wr_ref[:, lo:hi])
        p = pc[halo:halo + tile]
        prev = pltpu.roll(pc, 1, 0)[halo:halo + tile]
        nxt = pltpu.roll(pc, total - 1, 0)[halo:halo + tile]
        mup, mun = mup_ref[:, lo:hi], mun_ref[:, lo:hi]
        return p * (1.0 - mup - mun) + prev * mup + nxt * mun

    c = RWKV_WIDTH
    tail = shifted(3 * c, RWKV_COLS)
    lora = tail[:, :LANES]
    xg = tail[:, LANES:]
    lora_t = jnp.tanh(lora).astype(BF16)
    lora_b = lora.astype(BF16)
    k = shifted(c, 2 * c)

    kx = k * kk_ref[...]
    kk = kx * lax.rsqrt(_head_sum(kx * kx, 1) + KK_EPS)
    gate = _dot(_sigmoid(xg).astype(BF16), g2_ref[...])

    kds = []
    for d, dir_o in enumerate((dir0_o, dir1_o)):
        w_raw = w0_ref[d:d + 1, :] + _dot(lora_t, w2_ref[d])
        dir_o[:, :c] = (-math.exp(-0.5)) * _sigmoid(w_raw)
        iclr = _sigmoid(a0_ref[d:d + 1, :] + _dot(lora_b, a2_ref[d]))
        kd = k * (1.0 + (iclr - 1.0) * ka_ref[...])
        dir_o[:, c:2 * c] = kd
        dir_o[:, 2 * c:] = iclr * kk
        kds.append(kd)

    r, v = shifted(0, c), shifted(2 * c, 3 * c)
    shared_o[:, :c] = r
    shared_o[:, c:2 * c] = v
    shared_o[:, 2 * c:] = kk
    gate_o[...] = gate
    bonus_o[...] = _head_sum(r * (kds[0] + kds[1]) * rk_ref[...], 1) * v


def _inproj_prep(x2, g_mix, w_in, mu_prev, mu_next, w0, w2, a0, a2, g2, k_k, k_a, r_k, seq, tile):
    rows, d = x2.shape
    c = RWKV_WIDTH
    cols = RWKV_COLS
    halo = SUBLANES_BF16
    per = tile // halo
    wb = w_in.astype(BF16)
    wr, wq, wkv = wb[:, :cols], wb[:, cols:cols + ATTN_WIDTH], wb[:, cols + ATTN_WIDTH:]
    zeros = jnp.zeros((2, DECAY_LORA, c), F32)
    w2p = jnp.concatenate([w2, zeros], axis=1).astype(BF16)
    a2p = jnp.concatenate([zeros, a2], axis=1).astype(BF16)
    row = lambda i: (i, 0)
    prv = lambda i: (jnp.maximum(i * per - 1, 0), 0)
    nxt = lambda i: (jnp.minimum((i + 1) * per, rows // halo - 1), 0)
    c2 = lambda i: (0, 0)
    c3 = lambda i: (0, 0, 0)
    resident = dict(pipeline_mode=pl.Buffered(1))
    out = jax.ShapeDtypeStruct((rows, c), F32)
    packed = jax.ShapeDtypeStruct((rows, 3 * c), F32)
    body = functools.partial(_inprep_body, tiles_per_seq=seq // tile)
    return pl.pallas_call(
        body,
        grid=(rows // tile,),
        in_specs=[pl.BlockSpec((tile, d), row), pl.BlockSpec((halo, d), prv), pl.BlockSpec((halo, d), nxt),
                  pl.BlockSpec((1, d), c2),
                  pl.BlockSpec(wr.shape, c2, **resident), pl.BlockSpec(wq.shape, c2, **resident),
                  pl.BlockSpec(wkv.shape, c2, **resident),
                  pl.BlockSpec((1, cols), c2), pl.BlockSpec((1, cols), c2),
                  pl.BlockSpec((2, c), c2), pl.BlockSpec((2, LANES, c), c3),
                  pl.BlockSpec((2, c), c2), pl.BlockSpec((2, LANES, c), c3),
                  pl.BlockSpec((GATE_LORA, c), c2),
                  pl.BlockSpec((1, c), c2), pl.BlockSpec((1, c), c2), pl.BlockSpec((1, c), c2)],
        out_specs=[pl.BlockSpec((tile, ATTN_WIDTH), row), pl.BlockSpec((tile, 2 * KV_COLS), row)]
                  + [pl.BlockSpec((tile, 3 * c), row)] * 3 + [pl.BlockSpec((tile, c), row)] * 2,
        out_shape=[jax.ShapeDtypeStruct((rows, ATTN_WIDTH), F32), jax.ShapeDtypeStruct((rows, 2 * KV_COLS), F32)]
                  + [packed] * 3 + [out] * 2,
        compiler_params=pltpu.CompilerParams(dimension_semantics=("parallel",), vmem_limit_bytes=VMEM_LIMIT),
        name="inproj_prep",
    )(x2, x2, x2, g_mix.reshape(1, d), wr, wq, wkv, mu_prev.reshape(1, cols), mu_next.reshape(1, cols),
      w0, w2p, a0, a2p, g2.astype(BF16), k_k.reshape(1, c), k_a.reshape(1, c), r_k.reshape(1, c))


def _scan_step(j, *refs, filler=iter(())):
    h_scr = refs[-1]

    @pl.when(j == 0)
    def _():
        h_scr[...] = jnp.zeros_like(h_scr)

    n_batch, rows = refs[0].shape[0], refs[0].shape[1]
    members = [(bi, d, sub) for bi in range(n_batch) for d in range(2) for sub in range(rows // CHUNK)]
    for stage, _ in enumerate(_scan_stages(members, *refs)):
        if stage >= 2:
            next(filler, None)
    for _ in filler:
        pass


def _scan_stages(members, rf, vf, kkf, lwf, kdf, bf, rb, vb, kkb, lwb, kdb, bb, yf_o, yb_o, h_scr):
    cs = CHUNK
    subs = rf.shape[1] // cs
    n_pairs = rf.shape[2] // PAIR

    cs_shift = cs.bit_length() - 1
    assert 1 << cs_shift == cs
    t_row = lax.broadcasted_iota(jnp.int32, (cs, 2 * cs), 0)
    t_col = lax.broadcasted_iota(jnp.int32, (cs, 2 * cs), 1) & (cs - 1)
    eye_cat = (t_row == t_col).astype(F32)
    same_block = lambda log2_size: (t_row >> log2_size) == (t_col >> log2_size)
    ri = lax.broadcasted_iota(jnp.int32, (PAIR, PAIR), 0)
    ci = lax.broadcasted_iota(jnp.int32, (PAIR, PAIR), 1)
    eye = (ri == ci).astype(F32)
    same_head = (ri >> HEAD_SHIFT) == (ci >> HEAD_SHIFT)
    tr = lax.broadcasted_iota(jnp.int32, (cs, cs), 0)
    tc = lax.broadcasted_iota(jnp.int32, (cs, cs), 1)
    lane_even = lax.broadcasted_iota(jnp.int32, (cs, PAIR), 1) < HEAD_DIM
    col_first = lax.broadcasted_iota(jnp.int32, (cs, 2 * cs), 1) < cs

    def stack(x):
        zero = jnp.zeros_like(x)
        return jnp.concatenate([jnp.where(lane_even, x, zero), jnp.where(lane_even, zero, x)], axis=0)

    def block_diag(m):
        zero = jnp.zeros_like(m)
        return jnp.concatenate([jnp.where(col_first, m, zero), jnp.where(col_first, zero, m)], axis=0)

    dirs = ((0, rf, vf, kkf, lwf, kdf, bf, yf_o), (1, rb, vb, kkb, lwb, kdb, bb, yb_o))
    chains = []
    for bi, sub, (d, r_ref, v_ref, kk_ref, lw_ref, kd_ref, b_ref, y_o) in (
            (bi, sub, dirs[d]) for bi, d, sub in members):
        fwd = d == 0
        rows = slice(sub * cs, (sub + 1) * cs)
        strict = (t_col < t_row) if fwd else (t_col > t_row)
        incl = strict | (t_col == t_row)
        cum = ((tc <= tr) if fwd else (tc >= tr)).astype(BF16)

        lw = lw_ref[bi, rows]
        c_in = sum(_dot(cum, piece) for piece in _split_bf16(lw, 2))
        c_ex = c_in - lw
        c_tot = jnp.sum(lw, axis=0, keepdims=True)
        gam = jnp.exp(c_tot)
        kd = kd_ref[bi, rows]
        bv = b_ref[bi, rows]
        e_neg = jnp.exp(-c_in)
        e_end = gam * e_neg
        a_t = -kk_ref[bi, rows] * jnp.exp(c_ex)
        r_t = r_ref[bi, rows] * jnp.exp(c_in)
        b_t = bv * e_neg
        k_t = kd * e_neg
        b_h = bv * e_end
        k_h = kd * e_end
        vv = v_ref[bi, rows]
        for pr in range(n_pairs):
            sl = slice(pr * PAIR, (pr + 1) * PAIR)
            a_b, v_b = a_t[:, sl].astype(BF16), vv[:, sl].astype(BF16)
            chains.append(dict(
                bi=bi, d=d, pr=pr, sl=sl, rows=rows, order=sub if fwd else subs - 1 - sub,
                y_o=y_o, strict=strict, incl=incl, r_t=r_t[:, sl], gam=gam[:, sl],
                a_b=a_b, r_b=r_t[:, sl].astype(BF16), a_sb=stack(a_b), v_b=v_b, v_sb=stack(v_b),
                bt_sb=stack(b_t[:, sl].astype(BF16)), kt_sb=stack(k_t[:, sl].astype(BF16)),
                bk_h=jnp.concatenate([b_h[:, sl].astype(BF16), k_h[:, sl].astype(BF16)], axis=0)))
    yield

    h2 = 2 * cs
    for c in chains:
        lhs = jnp.concatenate([c["a_b"], c["r_b"]], axis=0)
        rhs = jnp.concatenate([c["bt_sb"], c["kt_sb"]], axis=0)
        gram = _dot_nt(lhs, rhs)
        a_ab = jnp.where(c["strict"], gram[:cs, :h2], 0.0)
        c["a_xk"] = jnp.concatenate([jnp.where(c["strict"], gram[:cs, h2:], 0.0),
                                     jnp.where(c["incl"], gram[cs:, h2:], 0.0)], axis=0).astype(BF16)
        c["a_rb"] = jnp.where(c["incl"], gram[cs:, :h2], 0.0).astype(BF16)
        c["n_cat"] = a_ab.astype(BF16)
        c["s_cat"] = (eye_cat + jnp.where(same_block(1), a_ab, 0.0)).astype(BF16)
    yield

    for level in range(1, cs_shift):
        join = same_block(level + 1) & ~same_block(level)
        zero = jnp.zeros((cs, h2), BF16)
        for c in chains:
            tn = _dot(c["s_cat"], block_diag(jnp.where(join, c["n_cat"], zero)))
            c["z_cat"] = (eye_cat + tn).astype(BF16)
        yield
        for c in chains:
            c["s_cat"] = _dot(c["z_cat"], block_diag(c["s_cat"])).astype(BF16)
        yield

    zero_b = jnp.zeros((cs, PAIR), BF16)
    for c in chains:
        av = _dot(c["a_xk"], c["v_sb"])
        c["w1_sb"] = stack(av[:cs].astype(BF16))
        c["av"] = av[cs:]
        c["gam_col"] = jnp.sum(eye * c["gam"], axis=1, keepdims=True)
    yield
    for c in chains:
        c["pq"] = _dot(c["s_cat"], jnp.concatenate([c["a_sb"], c["w1_sb"]], axis=1)).astype(BF16)
    yield
    for c in chains:
        pq = c["pq"]
        ry = _dot(c["a_rb"], jnp.concatenate([stack(pq[:, :PAIR]), stack(pq[:, PAIR:])], axis=1))
        c["r_hat"] = (c["r_t"] + ry[:, :PAIR]).astype(BF16)
        c["y_hat"] = ry[:, PAIR:] + c["av"]
        rhs = jnp.concatenate([pq, jnp.concatenate([zero_b, c["v_b"]], axis=1)], axis=0)
        gd = _dot_tn(c["bk_h"], rhs)
        c["btp"] = jnp.where(same_head, gd[:, :PAIR], 0.0).astype(BF16)
        c["dd"] = jnp.where(same_head, gd[:, PAIR:], 0.0)
    for order in range(subs):
        yield
        for c in (c for c in chains if c["order"] == order):
            h0 = h_scr[c["bi"], c["d"], c["pr"]]
            yh = _dot(jnp.concatenate([c["r_hat"], c["btp"]], axis=0), h0.astype(BF16))
            c["y_o"][c["bi"], c["rows"], c["sl"]] = yh[:cs] + c["y_hat"]
            h_scr[c["bi"], c["d"], c["pr"]] = c["gam_col"] * h0 + yh[cs:] + c["dd"]


def _attn_stages(n, nb, q_ref, rows, kv_window, qg_ref, kg_ref, bias_ref, sink_ref, o_ref):
    blk = rows.stop - rows.start
    head_mean = lambda z: _head_sum(z, 1) * (1.0 / HEAD_DIM)

    k_win = jnp.concatenate([kv[:, :KV_COLS] for kv in kv_window], axis=0)
    v_win = jnp.concatenate([kv[:, KV_COLS:] for kv in kv_window], axis=0)
    kn = k_win * lax.rsqrt(head_mean(k_win * k_win) + NORM_EPS) * kg_ref[...]
    v_b = v_win.astype(BF16)

    lane = lax.broadcasted_iota(jnp.int32, kn.shape, 1)
    k_at = []
    for g in range(KV_HEADS):
        own = jnp.where((lane >> HEAD_SHIFT) == g, kn, 0.0)
        other = pltpu.roll(own, HEAD_DIM, 1)
        k_at.append([own if p == g else other for p in range(2)])

    row = lax.broadcasted_iota(jnp.int32, (blk, 3 * blk), 0)
    col = lax.broadcasted_iota(jnp.int32, (blk, 3 * blk), 1)
    rel = col - blk - row
    valid = (jnp.abs(rel) <= WINDOW)
    valid &= (col >= blk) | (n > 0)
    valid &= (col < 2 * blk) | (n < nb - 1)
    out_lane_even = lax.broadcasted_iota(jnp.int32, (blk, LANES), 1) < HEAD_DIM

    group = Q_HEADS // KV_HEADS
    n_slabs = ATTN_WIDTH // LANES
    slabs_per_group = n_slabs // KV_HEADS
    qn = []
    for s in range(n_slabs):
        q = q_ref[0, rows, s * LANES:(s + 1) * LANES]
        qn.append((q * lax.rsqrt(head_mean(q * q) + NORM_EPS) * qg_ref[...]
                   * (HEAD_DIM ** -0.5 * LOG2E)).astype(BF16))
    scores = []
    for g in range(KV_HEADS):
        kcat = jnp.concatenate([k_at[g][0], k_at[g][1]], axis=0).astype(BF16)
        q_g = jnp.concatenate(qn[g * slabs_per_group:(g + 1) * slabs_per_group], axis=0)
        scores.append(_dot_nt(q_g, kcat))
    yield
    probs, inv_denom = [], []
    for s in range(n_slabs):
        heads = (2 * s, 2 * s + 1)
        g, sg = s // slabs_per_group, s % slabs_per_group
        sc = [jnp.where(valid, scores[g][sg * blk:(sg + 1) * blk, p * 3 * blk:(p + 1) * 3 * blk] + bias_ref[h],
                        -jnp.inf) for p, h in enumerate(heads)]
        sinks = [sink_ref[h] * LOG2E for h in heads]
        m = [jnp.maximum(jnp.max(z, axis=-1, keepdims=True), sk) for z, sk in zip(sc, sinks)]
        e = [jnp.exp2(z - mm) for z, mm in zip(sc, m)]
        inv_denom += [1.0 / (jnp.sum(ee, axis=-1, keepdims=True) + jnp.exp2(sk - mm))
                      for ee, sk, mm in zip(e, sinks, m)]
        probs += [ee.astype(BF16) for ee in e]
        yield
    o_all = _dot(jnp.concatenate(probs, axis=0), v_b)
    for s in range(n_slabs):
        g = (2 * s) // group
        halves = []
        for p in range(2):
            h = 2 * s + p
            o = o_all[h * blk:(h + 1) * blk] * inv_denom[h]
            halves.append(o if p == g else pltpu.roll(o, HEAD_DIM, 1))
        o_ref[0, rows, s * LANES:(s + 1) * LANES] = jnp.where(out_lane_even, halves[0], halves[1])


def _t5_bucket(rel):
    nb = REL_BUCKETS // 2
    max_exact = nb // 2
    ret = jnp.where(rel > 0, nb, 0)
    n = jnp.abs(rel)
    large = max_exact + (jnp.log(jnp.maximum(n, 1).astype(F32) / max_exact)
                         / math.log(REL_MAX_DIST / max_exact) * (nb - max_exact)).astype(jnp.int32)
    large = jnp.minimum(large, nb - 1)
    return ret + jnp.where(n < max_exact, n, large)


BIAS_SPAN = 4 * BLOCK


def _bias_body(tab_ref, o_ref):
    blk = o_ref.shape[1]
    x = jnp.broadcast_to(tab_ref[0], (blk, BIAS_SPAN))
    shifted = pltpu.roll(x, BIAS_SPAN - (blk - 1), 1, stride=1, stride_axis=0)
    o_ref[0] = shifted[:, :3 * blk]


def _bias_table(rel_bias):
    rel = jnp.arange(BIAS_SPAN) - (2 * BLOCK - 1)
    tab = jnp.transpose(rel_bias[_t5_bucket(rel)].astype(F32)) * LOG2E
    shape = (BLOCK, 3 * BLOCK)
    return pl.pallas_call(
        _bias_body,
        grid=(Q_HEADS,),
        in_specs=[pl.BlockSpec((1, 1, BIAS_SPAN), lambda h: (h, 0, 0))],
        out_specs=pl.BlockSpec((1,) + shape, lambda h: (h, 0, 0)),
        out_shape=jax.ShapeDtypeStruct((Q_HEADS,) + shape, F32),
        name="bias_table",
    )(tab.reshape(Q_HEADS, 1, BIAS_SPAN))


class _Columns:
    def __init__(self, ref, lo, hi):
        self.ref, self.lo, self.hi = ref, lo, hi
        self.shape = ref.shape[:2] + (hi - lo,)

    def __getitem__(self, idx):
        bi, rows = idx
        return self.ref[bi, rows, self.lo:self.hi]


def _mixers_body(*refs, n_blocks, per_step):
    c = RWKV_WIDTH
    scan_refs = []
    for shared, direc in (refs[0:2], refs[2:4]):
        scan_refs += [_Columns(shared, 0, c), _Columns(shared, c, 2 * c), _Columns(shared, 2 * c, 3 * c),
                      _Columns(direc, 0, c), _Columns(direc, c, 2 * c), _Columns(direc, 2 * c, 3 * c)]
    q_ref, kvp_ref, kvc_ref, kvn_ref, qg_ref, kg_ref, bias_ref, sink_ref = refs[4:12]
    yf_o, yb_o, attn_o, h_scr = refs[12:]
    j = pl.program_id(0)
    kv_blocks = ([kvp_ref[0]] + [kvc_ref[0, s * BLOCK:(s + 1) * BLOCK] for s in range(per_step)] + [kvn_ref[0]])

    def attention():
        for s in range(per_step):
            n = (j * per_step + s) % n_blocks
            yield from _attn_stages(n, n_blocks, q_ref, slice(s * BLOCK, (s + 1) * BLOCK), kv_blocks[s:s + 3],
                                    qg_ref, kg_ref, bias_ref, sink_ref, attn_o)

    _scan_step(j, *scan_refs, yf_o, yb_o, h_scr, filler=attention())


def _mixers(shared, dir0, dir1, q, kv, q_gain, k_gain, rel_bias, sink):
    b, t, _ = shared.shape
    c = RWKV_WIDTH
    steps = t // (CHUNKS_PER_STEP * CHUNK)
    nb = t // BLOCK
    per_step = (b * nb) // steps
    assert per_step * steps == b * nb and nb % per_step == 0
    nq = nb // per_step
    fw = lambda j: (0, j, 0)
    bw = lambda j: (0, steps - 1 - j, 0)
    blk = (b, CHUNKS_PER_STEP * CHUNK, c)
    blk3 = (b, CHUNKS_PER_STEP * CHUNK, 3 * c)
    cur = lambda j: (j // nq, j % nq, 0)
    prv = lambda j: (j // nq, jnp.maximum((j % nq) * per_step - 1, 0), 0)
    nxt = lambda j: (j // nq, jnp.minimum((j % nq + 1) * per_step, nb - 1), 0)
    c2 = lambda j: (0, 0)
    c3 = lambda j: (0, 0, 0)
    kvblk = (1, BLOCK, 2 * KV_COLS)
    return pl.pallas_call(
        functools.partial(_mixers_body, n_blocks=nb, per_step=per_step),
        grid=(steps,),
        in_specs=[pl.BlockSpec(blk3, fw)] * 2 + [pl.BlockSpec(blk3, bw)] * 2
                 + [pl.BlockSpec((1, per_step * BLOCK, ATTN_WIDTH), cur), pl.BlockSpec(kvblk, prv),
                    pl.BlockSpec((1, per_step * BLOCK, 2 * KV_COLS), cur), pl.BlockSpec(kvblk, nxt),
                    pl.BlockSpec((1, LANES), c2), pl.BlockSpec((1, LANES), c2),
                    pl.BlockSpec((Q_HEADS, BLOCK, 3 * BLOCK), c3), pl.BlockSpec(memory_space=pltpu.SMEM)],
        out_specs=[pl.BlockSpec(blk, fw), pl.BlockSpec(blk, bw),
                   pl.BlockSpec((1, per_step * BLOCK, ATTN_WIDTH), cur)],
        out_shape=[jax.ShapeDtypeStruct((b, t, c), F32)] * 2 + [jax.ShapeDtypeStruct((b, t, ATTN_WIDTH), F32)],
        scratch_shapes=[pltpu.VMEM((b, 2, c // PAIR, PAIR, PAIR), F32)],
        compiler_params=pltpu.CompilerParams(dimension_semantics=("arbitrary",), vmem_limit_bytes=VMEM_LIMIT),
        name="mixers",
    )(shared, dir0, shared, dir1,
      q, kv, kv, kv, jnp.tile(q_gain, 2).reshape(1, LANES), jnp.tile(k_gain, 2).reshape(1, LANES),
      _bias_table(rel_bias), sink.astype(F32))


def _outffn_body(*refs, tiles_per_seq, ff_chunk):
    main, prev, nxt = refs[0:6], refs[6:12], refs[12:18]
    lw_ref, lb_ref, wo_ref, gf_ref, wu_ref, cw_ref, cb_ref, wd_ref, o_ref, act_scr = refs[18:]
    i = pl.program_id(0)
    tile = main[0].shape[0]
    halo = prev[0].shape[0]
    d_ff = wd_ref.shape[0]
    c = RWKV_WIDTH
    total = tile + 2 * halo
    first = (i % tiles_per_seq) == 0
    last = (i % tiles_per_seq) == tiles_per_seq - 1

    def mixed(x, yf, yb, bonus, gate, attn):
        y = yf + yb
        mu = _head_sum(y, 2) * (1.0 / HEAD_DIM)
        yc = y - mu
        var = _head_sum(yc * yc, 1) * (1.0 / HEAD_DIM)
        yn = yc * lax.rsqrt(var + LNX_EPS) * lw_ref[...] + lb_ref[...]
        mix_r = (yn + bonus) * gate
        return x + _dot(mix_r.astype(BF16), wo_ref[:c, :]) + _dot(attn.astype(BF16), wo_ref[c:, :])

    x1cat = mixed(*(jnp.concatenate([p[...], m[...], n[...]], axis=0) for p, m, n in zip(prev, main, nxt)))
    x1 = x1cat[halo:halo + tile]
    hcat = x1cat * lax.rsqrt(jnp.mean(x1cat * x1cat, axis=-1, keepdims=True) + NORM_EPS) * gf_ref[...]
    row = lax.broadcasted_iota(jnp.int32, (total, 1), 0)
    outside = (first & (row < halo)) | (last & (row >= halo + tile))
    hcat = jnp.where(outside, 0.0, hcat).astype(BF16)

    def conv(cols):
        u = _dot(hcat, wu_ref[:, cols])
        up = pltpu.roll(u, 1, 0)[halo:halo + tile]
        un = pltpu.roll(u, total - 1, 0)[halo:halo + tile]
        return (up * cw_ref[0:1, cols] + u[halo:halo + tile] * cw_ref[1:2, cols] + un * cw_ref[2:3, cols]
                + cb_ref[:, cols])

    for lo in range(0, d_ff, ff_chunk):
        hi = min(lo + ff_chunk, d_ff)
        g = conv(slice(lo, hi))
        val = conv(slice(d_ff + lo, d_ff + hi))
        act_scr[:, lo:hi] = (g * _sigmoid(g) * val).astype(BF16)
    o_ref[...] = x1 + _dot(act_scr[...], wd_ref[...])


def _out_ffn(x2, yf, yb, bonus, gate, attn, lnx_w, lnx_b, w_out, g_ffn, w_up, conv_w, conv_b, w_down,
             seq, tile, ff_chunk):
    rows, d = x2.shape
    c = RWKV_WIDTH
    d_ff = w_down.shape[0]
    halo = SUBLANES_F32
    per = tile // halo
    row = lambda i: (i, 0)
    prv = lambda i: (jnp.maximum(i * per - 1, 0), 0)
    nxt = lambda i: (jnp.minimum((i + 1) * per, rows // halo - 1), 0)
    const = lambda i: (0, 0)
    resident = dict(pipeline_mode=pl.Buffered(1))
    streams = (x2, yf, yb, bonus, gate, attn)

    def stream_specs(nrows, index_map):
        return [pl.BlockSpec((nrows, z.shape[1]), index_map) for z in streams]

    body = functools.partial(_outffn_body, tiles_per_seq=seq // tile, ff_chunk=ff_chunk)
    return pl.pallas_call(
        body,
        grid=(rows // tile,),
        in_specs=stream_specs(tile, row) + stream_specs(halo, prv) + stream_specs(halo, nxt)
                 + [pl.BlockSpec((1, c), const), pl.BlockSpec((1, c), const),
                    pl.BlockSpec(w_out.shape, const, **resident), pl.BlockSpec((1, d), const),
                    pl.BlockSpec((d, 2 * d_ff), const, **resident),
                    pl.BlockSpec((3, 2 * d_ff), const), pl.BlockSpec((1, 2 * d_ff), const),
                    pl.BlockSpec((d_ff, d), const, **resident)],
        out_specs=pl.BlockSpec((tile, d), row),
        out_shape=jax.ShapeDtypeStruct((rows, d), F32),
        scratch_shapes=[pltpu.VMEM((tile, d_ff), BF16)],
        compiler_params=pltpu.CompilerParams(dimension_semantics=("parallel",), vmem_limit_bytes=VMEM_LIMIT),
        name="out_ffn",
    )(*streams, *streams, *streams, lnx_w.reshape(1, c), lnx_b.reshape(1, c), w_out.astype(BF16),
      g_ffn.reshape(1, d), w_up.astype(BF16), conv_w, conv_b.reshape(1, 2 * d_ff), w_down.astype(BF16))


def _layer(x, g_mix, w_in, mu_prev, mu_next, w0, w2, a0, a2, g2, k_k, k_a, r_k, lnx_w, lnx_b,
           q_gain, k_gain, rel_bias, sink, w_out, g_ffn, w_up, conv_w, conv_b, w_down):
    b, t, d = x.shape
    rows = b * t
    x2 = x.reshape(rows, d)
    q, kv, shared, dir0, dir1, gate, bonus = _inproj_prep(
        x2, g_mix, w_in, mu_prev, mu_next, w0, w2, a0, a2, g2, k_k, k_a, r_k.reshape(-1),
        seq=t, tile=min(ROW_TILE, t))
    seq3 = lambda z: z.reshape(b, t, z.shape[-1])
    yf, yb, attn = _mixers(seq3(shared), seq3(dir0), seq3(dir1), seq3(q), seq3(kv), q_gain, k_gain, rel_bias, sink)
    flat = lambda z: z.reshape(rows, z.shape[-1])
    out = _out_ffn(x2, flat(yf), flat(yb), bonus, gate, flat(attn), lnx_w, lnx_b, w_out, g_ffn,
                   w_up, conv_w, conv_b, w_down, seq=t, tile=min(ROW_TILE, t), ff_chunk=FF_CHUNK)
    return out.reshape(b, t, d)


def kernel(x, g_mix, w_in, mu_prev, mu_next, w0, w2, a0, a2, g2, k_k, k_a, r_k, lnx_w, lnx_b, q_gain, k_gain,
           rel_bias, sink, w_out, g_ffn, w_up, conv_w, conv_b, w_down):
    depth = g_mix.shape[0]
    for l in range(depth):
        x = _layer(x, g_mix[l], w_in[l], mu_prev[l], mu_next[l], w0[l], w2[l], a0[l], a2[l], g2[l], k_k[l], k_a[l],
                   r_k[l], lnx_w[l], lnx_b[l], q_gain[l], k_gain[l], rel_bias, sink[l], w_out[l], g_ffn[l],
                   w_up[l], conv_w[l], conv_b[l], w_down[l])
    return x
```

```python
import functools
import math

import jax
import jax.numpy as jnp
from jax import lax
from jax.experimental import pallas as pl
from jax.experimental.pallas import tpu as pltpu

F32 = jnp.float32
BF16 = jnp.bfloat16

HEAD_DIM = 64
RWKV_WIDTH = 512
ATTN_WIDTH = 512
KV_HEADS = 2
Q_HEADS = 8
DECAY_LORA = 64
ICLR_LORA = 64
GATE_LORA = 128
RWKV_COLS = 3 * RWKV_WIDTH + DECAY_LORA + ICLR_LORA + GATE_LORA
KV_COLS = KV_HEADS * HEAD_DIM
WINDOW = 128
BLOCK = 128
REL_BUCKETS = 32
REL_MAX_DIST = 128
NORM_EPS = 1e-6
LNX_EPS = 64e-5
KK_EPS = 1e-12
LOG2E = math.log2(math.e)

LANES = 128
MXU_DIM = 256
SUBLANES_F32 = 8
SUBLANES_BF16 = 16
VMEM_LIMIT = 48 * 1024 * 1024

ROW_TILE = 512
FF_CHUNK = 768
CHUNK = 64
CHUNKS_PER_STEP = 2
PAIR = 2 * HEAD_DIM
assert PAIR == LANES
HEAD_SHIFT = HEAD_DIM.bit_length() - 1
assert 1 << HEAD_SHIFT == HEAD_DIM


def _dot(a, b):
    return jnp.dot(a, b, preferred_element_type=F32)


def _dot_nt(a, b):
    return lax.dot_general(a, b, (((1,), (1,)), ((), ())), preferred_element_type=F32)


def _dot_tn(a, b):
    return lax.dot_general(a, b, (((0,), (0,)), ((), ())), preferred_element_type=F32)


def _sigmoid(x):
    return 1.0 / (1.0 + jnp.exp(-x))


def _split_bf16(x, parts):
    out = []
    for _ in range(parts):
        h = x.astype(BF16)
        out.append(h)
        x = x - h.astype(F32)
    return out


def _head_sum(x, parts):
    width = x.shape[-1]
    blk = min(width, MXU_DIM)
    r = lax.broadcasted_iota(jnp.int32, (blk, blk), 0) >> HEAD_SHIFT
    c = lax.broadcasted_iota(jnp.int32, (blk, blk), 1) >> HEAD_SHIFT
    m = (r == c).astype(BF16)
    pieces = _split_bf16(x, parts)
    cols = []
    for c0 in range(0, width, blk):
        acc = None
        for h in pieces:
            t = _dot(h[:, c0:c0 + blk], m)
            acc = t if acc is None else acc + t
        cols.append(acc)
    return cols[0] if len(cols) == 1 else jnp.concatenate(cols, axis=1)


def _inprep_body(x_ref, xp_ref, xn_ref, g_ref, win_ref,
                 mup_ref, mun_ref, w0_ref, w2_ref, a0_ref, a2_ref, g2_ref, kk_ref, ka_ref, rk_ref,
                 q_o, kv_o, shared_o, dir0_o, dir1_o, gate_o, bonus_o, wb_scr,
                 *, tiles_per_seq):
    i = pl.program_id(0)
    tile = x_ref.shape[0]
    halo = xp_ref.shape[0]
    first = (i % tiles_per_seq) == 0
    last = (i % tiles_per_seq) == tiles_per_seq - 1

    @pl.when(i == 0)
    def _():
        wb_scr[...] = win_ref[...].astype(BF16)

    wr_ref = wb_scr.at[:, :RWKV_COLS]
    wq_ref = wb_scr.at[:, RWKV_COLS:RWKV_COLS + ATTN_WIDTH]
    wkv_ref = wb_scr.at[:, RWKV_COLS + ATTN_WIDTH:]

    def norm(x):
        return (x * lax.rsqrt(jnp.mean(x * x, axis=-1, keepdims=True) + NORM_EPS) * g_ref[...]).astype(BF16)

    h = norm(x_ref[...])
    q_o[...] = _dot(h, wq_ref[...])
    kv_o[...] = _dot(h, wkv_ref[...])
    hp = norm(xp_ref[...])
    hn = norm(xn_ref[...])
    hp = jnp.where(first, jnp.zeros_like(hp), hp)
    hn = jnp.where(last, jnp.zeros_like(hn), hn)
    hcat = jnp.concatenate([hp, h, hn], axis=0)
    total = tile + 2 * halo

    def shifted(lo, hi):
        pc = _dot(hcat, wr_ref[:, lo:hi])
        p = pc[halo:halo + tile]
        prev = pltpu.roll(pc, 1, 0)[halo:halo + tile]
        nxt = pltpu.roll(pc, total - 1, 0)[halo:halo + tile]
        mup, mun = mup_ref[:, lo:hi], mun_ref[:, lo:hi]
        return p * (1.0 - mup - mun) + prev * mup + nxt * mun

    c = RWKV_WIDTH
    tail = shifted(3 * c, RWKV_COLS)
    lora = tail[:, :LANES]
    xg = tail[:, LANES:]
    lora_t = jnp.tanh(lora).astype(BF16)
    lora_b = lora.astype(BF16)
    k = shifted(c, 2 * c)

    kx = k * kk_ref[...]
    kk = kx * lax.rsqrt(_head_sum(kx * kx, 1) + KK_EPS)
    gate = _dot(_sigmoid(xg).astype(BF16), g2_ref[...])

    kds = []
    for d, dir_o in enumerate((dir0_o, dir1_o)):
        w_raw = w0_ref[d:d + 1, :] + _dot(lora_t, w2_ref[d])
        dir_o[:, :c] = (-math.exp(-0.5)) * _sigmoid(w_raw)
        iclr = _sigmoid(a0_ref[d:d + 1, :] + _dot(lora_b, a2_ref[d]))
        kd = k * (1.0 + (iclr - 1.0) * ka_ref[...])
        dir_o[:, c:2 * c] = kd
        dir_o[:, 2 * c:] = iclr * kk
        kds.append(kd)

    r, v = shifted(0, c), shifted(2 * c, 3 * c)
    shared_o[:, :c] = r
    shared_o[:, c:2 * c] = v
    shared_o[:, 2 * c:] = kk
    gate_o[...] = gate
    bonus_o[...] = _head_sum(r * (kds[0] + kds[1]) * rk_ref[...], 1) * v


def _inproj_prep(x2, g_mix, w_in, mu_prev, mu_next, w0, w2, a0, a2, g2, k_k, k_a, r_k, seq, tile):
    rows, d = x2.shape
    c = RWKV_WIDTH
    cols = RWKV_COLS
    halo = SUBLANES_BF16
    per = tile // halo
    zeros = jnp.zeros((2, DECAY_LORA, c), F32)
    w2p = jnp.concatenate([w2, zeros], axis=1).astype(BF16)
    a2p = jnp.concatenate([zeros, a2], axis=1).astype(BF16)
    row = lambda i: (i, 0)
    prv = lambda i: (jnp.maximum(i * per - 1, 0), 0)
    nxt = lambda i: (jnp.minimum((i + 1) * per, rows // halo - 1), 0)
    c2 = lambda i: (0, 0)
    c3 = lambda i: (0, 0, 0)
    resident = dict(pipeline_mode=pl.Buffered(1))
    out = jax.ShapeDtypeStruct((rows, c), F32)
    packed = jax.ShapeDtypeStruct((rows, 3 * c), F32)
    body = functools.partial(_inprep_body, tiles_per_seq=seq // tile)
    return pl.pallas_call(
        body,
        grid=(rows // tile,),
        in_specs=[pl.BlockSpec((tile, d), row), pl.BlockSpec((halo, d), prv), pl.BlockSpec((halo, d), nxt),
                  pl.BlockSpec((1, d), c2),
                  pl.BlockSpec(w_in.shape, c2, **resident),
                  pl.BlockSpec((1, cols), c2), pl.BlockSpec((1, cols), c2),
                  pl.BlockSpec((2, c), c2), pl.BlockSpec((2, LANES, c), c3),
                  pl.BlockSpec((2, c), c2), pl.BlockSpec((2, LANES, c), c3),
                  pl.BlockSpec((GATE_LORA, c), c2),
                  pl.BlockSpec((1, c), c2), pl.BlockSpec((1, c), c2), pl.BlockSpec((1, c), c2)],
        out_specs=[pl.BlockSpec((tile, ATTN_WIDTH), row), pl.BlockSpec((tile, 2 * KV_COLS), row)]
                  + [pl.BlockSpec((tile, 3 * c), row)] * 3 + [pl.BlockSpec((tile, c), row)] * 2,
        out_shape=[jax.ShapeDtypeStruct((rows, ATTN_WIDTH), F32), jax.ShapeDtypeStruct((rows, 2 * KV_COLS), F32)]
                  + [packed] * 3 + [out] * 2,
        scratch_shapes=[pltpu.VMEM(w_in.shape, BF16)],
        compiler_params=pltpu.CompilerParams(dimension_semantics=("arbitrary",),
                                             vmem_limit_bytes=VMEM_LIMIT + w_in.size * 2),
        name="inproj_prep",
    )(x2, x2, x2, g_mix.reshape(1, d), w_in, mu_prev.reshape(1, cols), mu_next.reshape(1, cols),
      w0, w2p, a0, a2p, g2.astype(BF16), k_k.reshape(1, c), k_a.reshape(1, c), r_k.reshape(1, c))


def _scan_step(j, *refs, filler=iter(())):
    h_scr = refs[-1]

    @pl.when(j == 0)
    def _():
        h_scr[...] = jnp.zeros_like(h_scr)

    n_batch, rows = refs[0].shape[0], refs[0].shape[1]
    members = [(bi, d, sub) for bi in range(n_batch) for d in range(2) for sub in range(rows // CHUNK)]
    for stage, _ in enumerate(_scan_stages(members, *refs)):
        if stage >= 2:
            next(filler, None)
    for _ in filler:
        pass


def _scan_stages(members, rf, vf, kkf, lwf, kdf, bf, rb, vb, kkb, lwb, kdb, bb, yf_o, yb_o, h_scr):
    cs = CHUNK
    subs = rf.shape[1] // cs
    n_pairs = rf.shape[2] // PAIR

    cs_shift = cs.bit_length() - 1
    assert 1 << cs_shift == cs
    t_row = lax.broadcasted_iota(jnp.int32, (cs, 2 * cs), 0)
    t_col = lax.broadcasted_iota(jnp.int32, (cs, 2 * cs), 1) & (cs - 1)
    eye_cat = (t_row == t_col).astype(F32)
    same_block = lambda log2_size: (t_row >> log2_size) == (t_col >> log2_size)
    ri = lax.broadcasted_iota(jnp.int32, (PAIR, PAIR), 0)
    ci = lax.broadcasted_iota(jnp.int32, (PAIR, PAIR), 1)
    eye = (ri == ci).astype(F32)
    same_head = (ri >> HEAD_SHIFT) == (ci >> HEAD_SHIFT)
    tr = lax.broadcasted_iota(jnp.int32, (cs, cs), 0)
    tc = lax.broadcasted_iota(jnp.int32, (cs, cs), 1)
    lane_even = lax.broadcasted_iota(jnp.int32, (cs, PAIR), 1) < HEAD_DIM
    col_first = lax.broadcasted_iota(jnp.int32, (cs, 2 * cs), 1) < cs

    def stack(x):
        zero = jnp.zeros_like(x)
        return jnp.concatenate([jnp.where(lane_even, x, zero), jnp.where(lane_even, zero, x)], axis=0)

    def block_diag(m):
        zero = jnp.zeros_like(m)
        return jnp.concatenate([jnp.where(col_first, m, zero), jnp.where(col_first, zero, m)], axis=0)

    dirs = ((0, rf, vf, kkf, lwf, kdf, bf, yf_o), (1, rb, vb, kkb, lwb, kdb, bb, yb_o))
    chains = []
    for bi, sub, (d, r_ref, v_ref, kk_ref, lw_ref, kd_ref, b_ref, y_o) in (
            (bi, sub, dirs[d]) for bi, d, sub in members):
        fwd = d == 0
        rows = slice(sub * cs, (sub + 1) * cs)
        strict = (t_col < t_row) if fwd else (t_col > t_row)
        incl = strict | (t_col == t_row)
        cum = ((tc <= tr) if fwd else (tc >= tr)).astype(BF16)

        lw = lw_ref[bi, rows]
        c_in = sum(_dot(cum, piece) for piece in _split_bf16(lw, 2))
        c_ex = c_in - lw
        c_tot = jnp.sum(lw, axis=0, keepdims=True)
        gam = jnp.exp(c_tot)
        kd = kd_ref[bi, rows]
        bv = b_ref[bi, rows]
        e_neg = jnp.exp(-c_in)
        e_end = gam * e_neg
        a_t = -kk_ref[bi, rows] * jnp.exp(c_ex)
        r_t = r_ref[bi, rows] * jnp.exp(c_in)
        b_t = bv * e_neg
        k_t = kd * e_neg
        b_h = bv * e_end
        k_h = kd * e_end
        vv = v_ref[bi, rows]
        for pr in range(n_pairs):
            sl = slice(pr * PAIR, (pr + 1) * PAIR)
            a_b, v_b = a_t[:, sl].astype(BF16), vv[:, sl].astype(BF16)
            chains.append(dict(
                bi=bi, d=d, pr=pr, sl=sl, rows=rows, order=sub if fwd else subs - 1 - sub,
                y_o=y_o, strict=strict, incl=incl, r_t=r_t[:, sl], gam=gam[:, sl],
                a_b=a_b, r_b=r_t[:, sl].astype(BF16), a_sb=stack(a_b), v_b=v_b, v_sb=stack(v_b),
                bt_sb=stack(b_t[:, sl].astype(BF16)), kt_sb=stack(k_t[:, sl].astype(BF16)),
                bk_h=jnp.concatenate([b_h[:, sl].astype(BF16), k_h[:, sl].astype(BF16)], axis=0)))
    yield

    h2 = 2 * cs
    for c in chains:
        lhs = jnp.concatenate([c["a_b"], c["r_b"]], axis=0)
        rhs = jnp.concatenate([c["bt_sb"], c["kt_sb"]], axis=0)
        gram = _dot_nt(lhs, rhs)
        a_ab = jnp.where(c["strict"], gram[:cs, :h2], 0.0)
        c["a_xk"] = jnp.concatenate([jnp.where(c["strict"], gram[:cs, h2:], 0.0),
                                     jnp.where(c["incl"], gram[cs:, h2:], 0.0)], axis=0).astype(BF16)
        c["a_rb"] = jnp.where(c["incl"], gram[cs:, :h2], 0.0).astype(BF16)
        c["n_cat"] = a_ab.astype(BF16)
        c["s_cat"] = (eye_cat + jnp.where(same_block(1), a_ab, 0.0)).astype(BF16)
    yield

    for level in range(1, cs_shift):
        join = same_block(level + 1) & ~same_block(level)
        zero = jnp.zeros((cs, h2), BF16)
        for c in chains:
            tn = _dot(c["s_cat"], block_diag(jnp.where(join, c["n_cat"], zero)))
            c["z_cat"] = (eye_cat + tn).astype(BF16)
        yield
        for c in chains:
            c["s_cat"] = _dot(c["z_cat"], block_diag(c["s_cat"])).astype(BF16)
        yield

    zero_b = jnp.zeros((cs, PAIR), BF16)
    for c in chains:
        av = _dot(c["a_xk"], c["v_sb"])
        c["w1_sb"] = stack(av[:cs].astype(BF16))
        c["av"] = av[cs:]
        c["gam_col"] = jnp.sum(eye * c["gam"], axis=1, keepdims=True)
    yield
    for c in chains:
        c["pq"] = _dot(c["s_cat"], jnp.concatenate([c["a_sb"], c["w1_sb"]], axis=1)).astype(BF16)
    yield
    for c in chains:
        pq = c["pq"]
        ry = _dot(c["a_rb"], jnp.concatenate([stack(pq[:, :PAIR]), stack(pq[:, PAIR:])], axis=1))
        c["r_hat"] = (c["r_t"] + ry[:, :PAIR]).astype(BF16)
        c["y_hat"] = ry[:, PAIR:] + c["av"]
        rhs = jnp.concatenate([pq, jnp.concatenate([zero_b, c["v_b"]], axis=1)], axis=0)
        gd = _dot_tn(c["bk_h"], rhs)
        c["btp"] = jnp.where(same_head, gd[:, :PAIR], 0.0).astype(BF16)
        c["dd"] = jnp.where(same_head, gd[:, PAIR:], 0.0)
    for order in range(subs):
        yield
        for c in (c for c in chains if c["order"] == order):
            h0 = h_scr[c["bi"], c["d"], c["pr"]]
            yh = _dot(jnp.concatenate([c["r_hat"], c["btp"]], axis=0), h0.astype(BF16))
            c["y_o"][c["bi"], c["rows"], c["sl"]] = yh[:cs] + c["y_hat"]
            h_scr[c["bi"], c["d"], c["pr"]] = c["gam_col"] * h0 + yh[cs:] + c["dd"]


def _attn_stages(n, nb, q_ref, rows, kv_window, qg_ref, kg_ref, bias_ref, sink_ref, o_ref):
    blk = rows.stop - rows.start
    head_mean = lambda z: _head_sum(z, 1) * (1.0 / HEAD_DIM)

    k_win = jnp.concatenate([kv[:, :KV_COLS] for kv in kv_window], axis=0)
    v_win = jnp.concatenate([kv[:, KV_COLS:] for kv in kv_window], axis=0)
    kn = k_win * lax.rsqrt(head_mean(k_win * k_win) + NORM_EPS) * kg_ref[...]
    v_b = v_win.astype(BF16)

    lane = lax.broadcasted_iota(jnp.int32, kn.shape, 1)
    k_at = []
    for g in range(KV_HEADS):
        own = jnp.where((lane >> HEAD_SHIFT) == g, kn, 0.0)
        other = pltpu.roll(own, HEAD_DIM, 1)
        k_at.append([own if p == g else other for p in range(2)])

    row = lax.broadcasted_iota(jnp.int32, (blk, 3 * blk), 0)
    col = lax.broadcasted_iota(jnp.int32, (blk, 3 * blk), 1)
    rel = col - blk - row
    valid = (jnp.abs(rel) <= WINDOW)
    valid &= (col >= blk) | (n > 0)
    valid &= (col < 2 * blk) | (n < nb - 1)
    out_lane_even = lax.broadcasted_iota(jnp.int32, (blk, LANES), 1) < HEAD_DIM

    group = Q_HEADS // KV_HEADS
    n_slabs = ATTN_WIDTH // LANES
    slabs_per_group = n_slabs // KV_HEADS
    qn = []
    for s in range(n_slabs):
        q = q_ref[0, rows, s * LANES:(s + 1) * LANES]
        qn.append((q * lax.rsqrt(head_mean(q * q) + NORM_EPS) * qg_ref[...]
                   * (HEAD_DIM ** -0.5 * LOG2E)).astype(BF16))
    scores = []
    for g in range(KV_HEADS):
        kcat = jnp.concatenate([k_at[g][0], k_at[g][1]], axis=0).astype(BF16)
        q_g = jnp.concatenate(qn[g * slabs_per_group:(g + 1) * slabs_per_group], axis=0)
        scores.append(_dot_nt(q_g, kcat))
    yield
    probs, inv_denom = [], []
    for s in range(n_slabs):
        heads = (2 * s, 2 * s + 1)
        g, sg = s // slabs_per_group, s % slabs_per_group
        sc = [jnp.where(valid, scores[g][sg * blk:(sg + 1) * blk, p * 3 * blk:(p + 1) * 3 * blk] + bias_ref[h],
                        -jnp.inf) for p, h in enumerate(heads)]
        sinks = [sink_ref[h] * LOG2E for h in heads]
        m = [jnp.maximum(jnp.max(z, axis=-1, keepdims=True), sk) for z, sk in zip(sc, sinks)]
        e = [jnp.exp2(z - mm) for z, mm in zip(sc, m)]
        inv_denom += [1.0 / (jnp.sum(ee, axis=-1, keepdims=True) + jnp.exp2(sk - mm))
                      for ee, sk, mm in zip(e, sinks, m)]
        probs += [ee.astype(BF16) for ee in e]
        yield
    o_all = _dot(jnp.concatenate(probs, axis=0), v_b)
    for s in range(n_slabs):
        g = (2 * s) // group
        halves = []
        for p in range(2):
            h = 2 * s + p
            o = o_all[h * blk:(h + 1) * blk] * inv_denom[h]
            halves.append(o if p == g else pltpu.roll(o, HEAD_DIM, 1))
        o_ref[0, rows, s * LANES:(s + 1) * LANES] = jnp.where(out_lane_even, halves[0], halves[1])


def _t5_bucket(rel):
    nb = REL_BUCKETS // 2
    max_exact = nb // 2
    ret = jnp.where(rel > 0, nb, 0)
    n = jnp.abs(rel)
    large = max_exact + (jnp.log(jnp.maximum(n, 1).astype(F32) / max_exact)
                         / math.log(REL_MAX_DIST / max_exact) * (nb - max_exact)).astype(jnp.int32)
    large = jnp.minimum(large, nb - 1)
    return ret + jnp.where(n < max_exact, n, large)


BIAS_SPAN = 4 * BLOCK


def _bias_body(tab_ref, o_ref):
    blk = o_ref.shape[1]
    x = jnp.broadcast_to(tab_ref[0], (blk, BIAS_SPAN))
    shifted = pltpu.roll(x, BIAS_SPAN - (blk - 1), 1, stride=1, stride_axis=0)
    o_ref[0] = shifted[:, :3 * blk]


def _bias_table(rel_bias):
    rel = jnp.arange(BIAS_SPAN) - (2 * BLOCK - 1)
    tab = jnp.transpose(rel_bias[_t5_bucket(rel)].astype(F32)) * LOG2E
    shape = (BLOCK, 3 * BLOCK)
    return pl.pallas_call(
        _bias_body,
        grid=(Q_HEADS,),
        in_specs=[pl.BlockSpec((1, 1, BIAS_SPAN), lambda h: (h, 0, 0))],
        out_specs=pl.BlockSpec((1,) + shape, lambda h: (h, 0, 0)),
        out_shape=jax.ShapeDtypeStruct((Q_HEADS,) + shape, F32),
        name="bias_table",
    )(tab.reshape(Q_HEADS, 1, BIAS_SPAN))


class _Columns:
    def __init__(self, ref, lo, hi):
        self.ref, self.lo, self.hi = ref, lo, hi
        self.shape = ref.shape[:2] + (hi - lo,)

    def __getitem__(self, idx):
        bi, rows = idx
        return self.ref[bi, rows, self.lo:self.hi]


def _mixers_body(*refs, n_blocks, per_step):
    c = RWKV_WIDTH
    scan_refs = []
    for shared, direc in (refs[0:2], refs[2:4]):
        scan_refs += [_Columns(shared, 0, c), _Columns(shared, c, 2 * c), _Columns(shared, 2 * c, 3 * c),
                      _Columns(direc, 0, c), _Columns(direc, c, 2 * c), _Columns(direc, 2 * c, 3 * c)]
    q_ref, kvp_ref, kvc_ref, kvn_ref, qg_ref, kg_ref, bias_ref, sink_ref = refs[4:12]
    yf_o, yb_o, attn_o, h_scr = refs[12:]
    j = pl.program_id(0)
    kv_blocks = ([kvp_ref[0]] + [kvc_ref[0, s * BLOCK:(s + 1) * BLOCK] for s in range(per_step)] + [kvn_ref[0]])

    def attention():
        for s in range(per_step):
            n = (j * per_step + s) % n_blocks
            yield from _attn_stages(n, n_blocks, q_ref, slice(s * BLOCK, (s + 1) * BLOCK), kv_blocks[s:s + 3],
                                    qg_ref, kg_ref, bias_ref, sink_ref, attn_o)

    _scan_step(j, *scan_refs, yf_o, yb_o, h_scr, filler=attention())


def _mixers(shared, dir0, dir1, q, kv, q_gain, k_gain, rel_bias, sink):
    b, t, _ = shared.shape
    c = RWKV_WIDTH
    steps = t // (CHUNKS_PER_STEP * CHUNK)
    nb = t // BLOCK
    per_step = (b * nb) // steps
    assert per_step * steps == b * nb and nb % per_step == 0
    nq = nb // per_step
    fw = lambda j: (0, j, 0)
    bw = lambda j: (0, steps - 1 - j, 0)
    blk = (b, CHUNKS_PER_STEP * CHUNK, c)
    blk3 = (b, CHUNKS_PER_STEP * CHUNK, 3 * c)
    cur = lambda j: (j // nq, j % nq, 0)
    prv = lambda j: (j // nq, jnp.maximum((j % nq) * per_step - 1, 0), 0)
    nxt = lambda j: (j // nq, jnp.minimum((j % nq + 1) * per_step, nb - 1), 0)
    c2 = lambda j: (0, 0)
    c3 = lambda j: (0, 0, 0)
    kvblk = (1, BLOCK, 2 * KV_COLS)
    return pl.pallas_call(
        functools.partial(_mixers_body, n_blocks=nb, per_step=per_step),
        grid=(steps,),
        in_specs=[pl.BlockSpec(blk3, fw)] * 2 + [pl.BlockSpec(blk3, bw)] * 2
                 + [pl.BlockSpec((1, per_step * BLOCK, ATTN_WIDTH), cur), pl.BlockSpec(kvblk, prv),
                    pl.BlockSpec((1, per_step * BLOCK, 2 * KV_COLS), cur), pl.BlockSpec(kvblk, nxt),
                    pl.BlockSpec((1, LANES), c2), pl.BlockSpec((1, LANES), c2),
                    pl.BlockSpec((Q_HEADS, BLOCK, 3 * BLOCK), c3), pl.BlockSpec(memory_space=pltpu.SMEM)],
        out_specs=[pl.BlockSpec(blk, fw), pl.BlockSpec(blk, bw),
                   pl.BlockSpec((1, per_step * BLOCK, ATTN_WIDTH), cur)],
        out_shape=[jax.ShapeDtypeStruct((b, t, c), F32)] * 2 + [jax.ShapeDtypeStruct((b, t, ATTN_WIDTH), F32)],
        scratch_shapes=[pltpu.VMEM((b, 2, c // PAIR, PAIR, PAIR), F32)],
        compiler_params=pltpu.CompilerParams(dimension_semantics=("arbitrary",), vmem_limit_bytes=VMEM_LIMIT),
        name="mixers",
    )(shared, dir0, shared, dir1,
      q, kv, kv, kv, jnp.tile(q_gain, 2).reshape(1, LANES), jnp.tile(k_gain, 2).reshape(1, LANES),
      _bias_table(rel_bias), sink.astype(F32))


def _outffn_body(*refs, tiles_per_seq, ff_chunk):
    main, prev, nxt = refs[0:6], refs[6:12], refs[12:18]
    lw_ref, lb_ref, wo_ref, gf_ref, wu_ref, cw_ref, cb_ref, wd_ref, o_ref, act_scr = refs[18:]
    i = pl.program_id(0)
    tile = main[0].shape[0]
    halo = prev[0].shape[0]
    d_ff = wd_ref.shape[0]
    c = RWKV_WIDTH
    total = tile + 2 * halo
    first = (i % tiles_per_seq) == 0
    last = (i % tiles_per_seq) == tiles_per_seq - 1

    def mixed(x, yf, yb, bonus, gate, attn):
        y = yf + yb
        mu = _head_sum(y, 2) * (1.0 / HEAD_DIM)
        yc = y - mu
        var = _head_sum(yc * yc, 1) * (1.0 / HEAD_DIM)
        yn = yc * lax.rsqrt(var + LNX_EPS) * lw_ref[...] + lb_ref[...]
        mix_r = (yn + bonus) * gate
        return x + _dot(mix_r.astype(BF16), wo_ref[:c, :]) + _dot(attn.astype(BF16), wo_ref[c:, :])

    x1cat = mixed(*(jnp.concatenate([p[...], m[...], n[...]], axis=0) for p, m, n in zip(prev, main, nxt)))
    x1 = x1cat[halo:halo + tile]
    hcat = x1cat * lax.rsqrt(jnp.mean(x1cat * x1cat, axis=-1, keepdims=True) + NORM_EPS) * gf_ref[...]
    row = lax.broadcasted_iota(jnp.int32, (total, 1), 0)
    outside = (first & (row < halo)) | (last & (row >= halo + tile))
    hcat = jnp.where(outside, 0.0, hcat).astype(BF16)

    def conv(cols):
        u = _dot(hcat, wu_ref[:, cols])
        up = pltpu.roll(u, 1, 0)[halo:halo + tile]
        un = pltpu.roll(u, total - 1, 0)[halo:halo + tile]
        return (up * cw_ref[0:1, cols] + u[halo:halo + tile] * cw_ref[1:2, cols] + un * cw_ref[2:3, cols]
                + cb_ref[:, cols])

    for lo in range(0, d_ff, ff_chunk):
        hi = min(lo + ff_chunk, d_ff)
        g = conv(slice(lo, hi))
        val = conv(slice(d_ff + lo, d_ff + hi))
        act_scr[:, lo:hi] = (g * _sigmoid(g) * val).astype(BF16)
    o_ref[...] = x1 + _dot(act_scr[...], wd_ref[...])


def _out_ffn(x2, yf, yb, bonus, gate, attn, lnx_w, lnx_b, w_out, g_ffn, w_up, conv_w, conv_b, w_down,
             seq, tile, ff_chunk):
    rows, d = x2.shape
    c = RWKV_WIDTH
    d_ff = w_down.shape[0]
    halo = SUBLANES_F32
    per = tile // halo
    row = lambda i: (i, 0)
    prv = lambda i: (jnp.maximum(i * per - 1, 0), 0)
    nxt = lambda i: (jnp.minimum((i + 1) * per, rows // halo - 1), 0)
    const = lambda i: (0, 0)
    resident = dict(pipeline_mode=pl.Buffered(1))
    streams = (x2, yf, yb, bonus, gate, attn)

    def stream_specs(nrows, index_map):
        return [pl.BlockSpec((nrows, z.shape[1]), index_map) for z in streams]

    body = functools.partial(_outffn_body, tiles_per_seq=seq // tile, ff_chunk=ff_chunk)
    return pl.pallas_call(
        body,
        grid=(rows // tile,),
        in_specs=stream_specs(tile, row) + stream_specs(halo, prv) + stream_specs(halo, nxt)
                 + [pl.BlockSpec((1, c), const), pl.BlockSpec((1, c), const),
                    pl.BlockSpec(w_out.shape, const, **resident), pl.BlockSpec((1, d), const),
                    pl.BlockSpec((d, 2 * d_ff), const, **resident),
                    pl.BlockSpec((3, 2 * d_ff), const), pl.BlockSpec((1, 2 * d_ff), const),
                    pl.BlockSpec((d_ff, d), const, **resident)],
        out_specs=pl.BlockSpec((tile, d), row),
        out_shape=jax.ShapeDtypeStruct((rows, d), F32),
        scratch_shapes=[pltpu.VMEM((tile, d_ff), BF16)],
        compiler_params=pltpu.CompilerParams(dimension_semantics=("parallel",), vmem_limit_bytes=VMEM_LIMIT),
        name="out_ffn",
    )(*streams, *streams, *streams, lnx_w.reshape(1, c), lnx_b.reshape(1, c), w_out.astype(BF16),
      g_ffn.reshape(1, d), w_up.astype(BF16), conv_w, conv_b.reshape(1, 2 * d_ff), w_down.astype(BF16))


def _layer(x, g_mix, w_in, mu_prev, mu_next, w0, w2, a0, a2, g2, k_k, k_a, r_k, lnx_w, lnx_b,
           q_gain, k_gain, rel_bias, sink, w_out, g_ffn, w_up, conv_w, conv_b, w_down):
    b, t, d = x.shape
    rows = b * t
    x2 = x.reshape(rows, d)
    q, kv, shared, dir0, dir1, gate, bonus = _inproj_prep(
        x2, g_mix, w_in, mu_prev, mu_next, w0, w2, a0, a2, g2, k_k, k_a, r_k.reshape(-1),
        seq=t, tile=min(ROW_TILE, t))
    seq3 = lambda z: z.reshape(b, t, z.shape[-1])
    yf, yb, attn = _mixers(seq3(shared), seq3(dir0), seq3(dir1), seq3(q), seq3(kv), q_gain, k_gain, rel_bias, sink)
    flat = lambda z: z.reshape(rows, z.shape[-1])
    out = _out_ffn(x2, flat(yf), flat(yb), bonus, gate, flat(attn), lnx_w, lnx_b, w_out, g_ffn,
                   w_up, conv_w, conv_b, w_down, seq=t, tile=min(ROW_TILE, t), ff_chunk=FF_CHUNK)
    return out.reshape(b, t, d)


def kernel(x, g_mix, w_in, mu_prev, mu_next, w0, w2, a0, a2, g2, k_k, k_a, r_k, lnx_w, lnx_b, q_gain, k_gain,
           rel_bias, sink, w_out, g_ffn, w_up, conv_w, conv_b, w_down):
    depth = g_mix.shape[0]
    for l in range(depth):
        x = _layer(x, g_mix[l], w_in[l], mu_prev[l], mu_next[l], w0[l], w2[l], a0[l], a2[l], g2[l], k_k[l], k_a[l],
                   r_k[l], lnx_w[l], lnx_b[l], q_gain[l], k_gain[l], rel_bias, sink[l], w_out[l], g_ffn[l],
                   w_up[l], conv_w[l], conv_b[l], w_down[l])
    return x
```
